```python
import math
import jax, jax.numpy as jnp
from jax import lax
import numpy as np

D_MODEL = 4096
BATCH = 4
SEQ = 2048
DEPTH = 2
DEC_BATCH = 8
DEC_SEQ = 4
PAST_LEN = 16384
PAGE_SIZE = 128

N_MIXERS = 2
N_A = (DEPTH + 1) // N_MIXERS
N_B = DEPTH // N_MIXERS
HEAD_DIM = 128
DILATED_GROUPS = ((128, 1), (512, 4), (2048, 16))
N_GROUPS = len(DILATED_GROUPS)
A_GROUP_HEADS = 16
A_QKV_COLS = 3 * N_GROUPS * A_GROUP_HEADS * HEAD_DIM
A_OUT_DIM = A_GROUP_HEADS * HEAD_DIM
B_DK = 128
B_HEADS = D_MODEL // B_DK
B_DV = D_MODEL // B_HEADS
B_IN_COLS = 4 * B_HEADS * B_DK
B_CHUNK = 16
D_FF = ((8 * D_MODEL // 3 + 255) // 256) * 256
EPS = 1e-6

kernel_name = 'hybrid_dilated_hgrn2_decode_step'


def rms_norm(x, g):
    xf = x.astype(jnp.float32)
    y = xf * lax.rsqrt(jnp.mean(xf * xf, axis=-1, keepdims=True) + EPS)
    return (y * g.astype(jnp.float32)).astype(x.dtype)


def ada_modulation(c, w, b):
    mod = jax.nn.silu(c) @ w + b
    return jnp.split(mod[:, None, :], 6, axis=-1)


def swiglu(h, w_in, w_out):
    gate, up = jnp.split(h @ w_in, 2, axis=-1)
    return (jax.nn.silu(gate) * up) @ w_out


def dilated_attn_prompt(q, k, v, dil, n_back):
    Bn, S, H, hd = q.shape
    L = S // dil
    nb = -(-L // n_back)
    Lp = nb * n_back

    def to_blocks(t):
        t = jnp.moveaxis(t.reshape(Bn, L, dil, H, hd), 2, 1)
        t = jnp.pad(t, ((0, 0), (0, 0), (0, Lp - L), (0, 0), (0, 0)))
        return t.reshape(Bn, dil, nb, n_back, H, hd)

    def with_prev(t):
        prev = jnp.pad(t, ((0, 0), (0, 0), (1, 0), (0, 0), (0, 0), (0, 0)))[:, :, :-1]
        return jnp.concatenate([prev, t], axis=3)

    qb = to_blocks(q)
    kc = with_prev(to_blocks(k))
    vc = with_prev(to_blocks(v))
    s = jnp.einsum('brnqhd,brnkhd->brnhqk', qb, kc).astype(jnp.float32) * (hd ** -0.5)
    qi = jnp.arange(n_back)[:, None]
    ki = jnp.arange(2 * n_back)[None, :]
    dist = n_back + qi - ki
    band = (dist >= 0) & (dist <= n_back)
    no_prev = (jnp.arange(nb)[:, None, None] == 0) & (ki[None] < n_back)
    mask = band[None] & ~no_prev
    s = jnp.where(mask[None, None, :, None], s, -jnp.inf)
    m = jnp.max(s, axis=-1, keepdims=True)
    p = jnp.exp(s - m)
    den = jnp.sum(p, axis=-1)
    o = jnp.einsum('brnhqk,brnkhd->brnqhd', p, vc.astype(jnp.float32))
    o = o / jnp.moveaxis(den, 3, 4)[..., None]
    lse = jnp.moveaxis(m[..., 0] + jnp.log(den), 3, 4)

    def from_blocks(t):
        rest = t.shape[4:]
        t = t.reshape((Bn, dil, Lp) + rest)[:, :, :L]
        return jnp.moveaxis(t, 1, 2).reshape((Bn, S) + rest)

    return from_blocks(o), from_blocks(lse)


def dilated_attn_cached(q, k_ext, v_ext, dil, n_back, buf_len):
    T, hd = q.shape[1], q.shape[3]
    idx = buf_len + jnp.arange(T)[:, None] - dil * jnp.arange(n_back + 1)[None, :]
    valid = idx >= 0
    idxc = jnp.maximum(idx, 0)
    kg = k_ext[:, idxc]
    vg = v_ext[:, idxc]
    s = jnp.einsum('bthd,btkhd->bthk', q, kg).astype(jnp.float32) * (hd ** -0.5)
    s = jnp.where(valid[None, :, None, :], s, -jnp.inf)
    m = jnp.max(s, axis=-1, keepdims=True)
    p = jnp.exp(s - m)
    den = jnp.sum(p, axis=-1)
    o = jnp.einsum('bthk,btkhd->bthd', p, vg.astype(jnp.float32)) / den[..., None]
    return o, m[..., 0] + jnp.log(den)


def mixer_a(h, bufs, w_qkv, w_o):
    Bn, T, _ = h.shape
    qkv = (h @ w_qkv).reshape(Bn, T, 3, N_GROUPS, A_GROUP_HEADS, HEAD_DIM)
    outs, lses, new_bufs = [], [], []
    for g, (window, dil) in enumerate(DILATED_GROUPS):
        q, k, v = qkv[:, :, 0, g], qkv[:, :, 1, g], qkv[:, :, 2, g]
        kv_new = jnp.stack([k, v], axis=2)
        n_back = window // dil
        if bufs is None:
            o, lse = dilated_attn_prompt(q, k, v, dil, n_back)
            keep = min(window, T)
            new_bufs.append(kv_new[:, T - keep:])
        else:
            buf = bufs[g].astype(kv_new.dtype)
            ext = jnp.concatenate([buf, kv_new], axis=1)
            o, lse = dilated_attn_cached(q, ext[:, :, 0], ext[:, :, 1], dil, n_back, buf.shape[1])
            new_bufs.append(ext[:, T:])
        outs.append(o)
        lses.append(lse)
    wts = jax.nn.softmax(jnp.stack(lses, axis=0), axis=0)
    merged = jnp.einsum('gbth,gbthd->bthd', wts, jnp.stack(outs, axis=0))
    y = merged.reshape(Bn, T, A_OUT_DIM).astype(h.dtype) @ w_o
    return y, new_bufs


def hgrn2_chunked(q, k, v, logf, s0, chunk):
    Bn, T, H, dk = q.shape
    dv = v.shape[-1]
    nc = T // chunk

    def split(t):
        return jnp.moveaxis(t.reshape((Bn, nc, chunk) + t.shape[2:]), 1, 0)

    tri = jnp.arange(chunk)[:, None] >= jnp.arange(chunk)[None, :]

    def step(S, inp):
        qc, kc, vc, lfc = inp
        A = jnp.cumsum(lfc, axis=1)
        o_inter = jnp.einsum('bthk,bhkv->bthv', qc * jnp.exp(A), S)
        diff = A[:, :, None] - A[:, None, :]
        decay = jnp.exp(jnp.where(tri[None, :, :, None, None], diff, -jnp.inf))
        scores = jnp.einsum('bthk,bshk,btshk->btsh', qc, kc, decay)
        o_intra = jnp.einsum('btsh,bshv->bthv', scores, vc)
        a_last = A[:, -1]
        kd = kc * jnp.exp(a_last[:, None] - A)
        S_new = jnp.exp(a_last)[..., None] * S + jnp.einsum('bshk,bshv->bhkv', kd, vc)
        return S_new, o_inter + o_intra

    s_T, o = lax.scan(step, s0, (split(q), split(k), split(v), split(logf)))
    return jnp.moveaxis(o, 0, 1).reshape(Bn, T, H, dv), s_T


def mixer_b(h, s0, w_in, lb, norm_g, w_o):
    Bn, T, _ = h.shape
    z = (h @ w_in).reshape(Bn, T, 4, B_HEADS, B_DK).astype(jnp.float32)
    q = jax.nn.silu(z[:, :, 0])
    lbh = lb.reshape(B_HEADS, B_DK)
    f = lbh + (1.0 - lbh) * jax.nn.sigmoid(z[:, :, 1])
    logf = jnp.log(f)
    k = 1.0 - f
    v = z[:, :, 2]
    g = z[:, :, 3]
    chunk = math.gcd(T, B_CHUNK)
    o, s_T = hgrn2_chunked(q, k, v, logf, s0.astype(jnp.float32), chunk)
    o = o * lax.rsqrt(jnp.mean(o * o, axis=-1, keepdims=True) + EPS)
    o = o * norm_g.astype(jnp.float32).reshape(B_HEADS, B_DV) * jax.nn.silu(g)
    y = o.reshape(Bn, T, B_HEADS * B_DV).astype(h.dtype) @ w_o
    return y, s_T


def setup_inputs(seed: int = 0) -> dict:
    key = jax.random.key(seed)
    ks = jax.random.split(key, 24)
    f32 = jnp.float32

    def nrm(k, shape, scale):
        return jax.random.normal(k, shape, f32) * scale

    def buf_shape(window):
        return (N_A, DEC_BATCH, min(window, PAST_LEN), 2, A_GROUP_HEADS, HEAD_DIM)

    return {
        'x_prompt': nrm(ks[0], (BATCH, SEQ, D_MODEL), 1.0),
        'x_sample': nrm(ks[1], (DEC_BATCH, DEC_SEQ, D_MODEL), 1.0),
        'state_a_kv_w128': nrm(ks[2], buf_shape(DILATED_GROUPS[0][0]), 1.0),
        'state_a_kv_w512': nrm(ks[3], buf_shape(DILATED_GROUPS[1][0]), 1.0),
        'state_a_kv_w2048': nrm(ks[4], buf_shape(DILATED_GROUPS[2][0]), 1.0),
        'state_b_rec': nrm(ks[5], (N_B, DEC_BATCH, B_HEADS, B_DK, B_DV), 0.5),
        'c_prompt': nrm(ks[6], (BATCH, D_MODEL), 1.0),
        'c_sample': nrm(ks[7], (DEC_BATCH, D_MODEL), 1.0),
        'ada_w': nrm(ks[8], (DEPTH, D_MODEL, 6 * D_MODEL), 0.5 * D_MODEL ** -0.5),
        'ada_b': nrm(ks[9], (DEPTH, 6 * D_MODEL), 0.02),
        'norm_mix_g': 1.0 + nrm(ks[10], (DEPTH, D_MODEL), 0.02),
        'norm_ffn_g': 1.0 + nrm(ks[11], (DEPTH, D_MODEL), 0.02),
        'a_w_qkv': nrm(ks[12], (N_A, D_MODEL, A_QKV_COLS), D_MODEL ** -0.5),
        'a_w_o': nrm(ks[13], (N_A, A_OUT_DIM, D_MODEL), A_OUT_DIM ** -0.5),
        'b_w_in': nrm(ks[14], (N_B, D_MODEL, B_IN_COLS), D_MODEL ** -0.5),
        'b_lb_logits': nrm(ks[15], (DEPTH, B_HEADS * B_DK), 0.5),
        'b_norm_g': 1.0 + nrm(ks[16], (N_B, B_HEADS * B_DV), 0.02),
        'b_w_o': nrm(ks[17], (N_B, B_HEADS * B_DV, D_MODEL), (B_HEADS * B_DV) ** -0.5),
        'ffn_w_in': nrm(ks[18], (DEPTH, D_MODEL, 2 * D_FF), D_MODEL ** -0.5),
        'ffn_w_out': nrm(ks[19], (DEPTH, D_FF, D_MODEL), D_FF ** -0.5),
        'final_g': 1.0 + nrm(ks[20], (D_MODEL,), 0.02),
    }


def reference(x_prompt, x_sample, state_a_kv_w128, state_a_kv_w512, state_a_kv_w2048, state_b_rec,
              c_prompt, c_sample, ada_w, ada_b, norm_mix_g, norm_ffn_g, a_w_qkv, a_w_o,
              b_w_in, b_lb_logits, b_norm_g, b_w_o, ffn_w_in, ffn_w_out, final_g):
    lb_all = jnp.cumsum(jax.nn.softmax(b_lb_logits.astype(jnp.float32), axis=0), axis=0)
    lb_all = lb_all - lb_all[0]

    def trunk(x, c, a_bufs, b_states):
        new_a = [[] for _ in range(N_GROUPS)]
        new_b = []
        for layer in range(DEPTH):
            sh_m, sc_m, gt_m, sh_f, sc_f, gt_f = ada_modulation(c, ada_w[layer], ada_b[layer])
            h = rms_norm(x, norm_mix_g[layer]) * (1.0 + sc_m) + sh_m
            i = layer // N_MIXERS
            if layer % N_MIXERS == 0:
                bufs = None if a_bufs is None else [buf[i] for buf in a_bufs]
                y, nbufs = mixer_a(h, bufs, a_w_qkv[i], a_w_o[i])
                for g in range(N_GROUPS):
                    new_a[g].append(nbufs[g])
            else:
                if b_states is None:
                    s0 = jnp.zeros((x.shape[0], B_HEADS, B_DK, B_DV), jnp.float32)
                else:
                    s0 = b_states[i]
                y, s_T = mixer_b(h, s0, b_w_in[i], lb_all[layer], b_norm_g[i], b_w_o[i])
                new_b.append(s_T)
            x = x + gt_m * y
            h = rms_norm(x, norm_ffn_g[layer]) * (1.0 + sc_f) + sh_f
            x = x + gt_f * swiglu(h, ffn_w_in[layer], ffn_w_out[layer])
        a_out = [jnp.stack(v, axis=0) for v in new_a]
        return rms_norm(x, final_g), a_out, jnp.stack(new_b, axis=0)

    y_prompt, a_p, new_rec_prompt = trunk(x_prompt, c_prompt, None, None)
    y_sample, a_s, new_rec_sample = trunk(
        x_sample, c_sample, (state_a_kv_w128, state_a_kv_w512, state_a_kv_w2048), state_b_rec)
    new_kv128_prompt, new_kv512_prompt, new_kv2048_prompt = a_p
    new_kv128_sample, new_kv512_sample, new_kv2048_sample = a_s
    return (y_prompt, y_sample, new_kv128_prompt, new_kv128_sample, new_kv512_prompt,
            new_kv512_sample, new_kv2048_prompt, new_kv2048_sample, new_rec_prompt, new_rec_sample)
```

```python
import functools
import math

import jax
import jax.numpy as jnp
from jax import lax
from jax.experimental import pallas as pl
from jax.experimental.pallas import tpu as pltpu

F32 = jnp.float32
BF16 = jnp.bfloat16

EPS = 1e-6
DILATIONS = (1, 4, 16)
N_BACK = 128
HEAD_DIM = 128
HGRN_DK = 128
HGRN_CHUNK = 16
N_MIXERS = 2
V7X_LANES = 128
V7X_SUBLANES = 8
V7X_SCOPED_VMEM_BYTES = 60000 * 1024


def _params(nbytes, n_axes):
    limit = int(min(V7X_SCOPED_VMEM_BYTES, max(2 * nbytes, 16 * 1024 * 1024)))
    return pltpu.CompilerParams(dimension_semantics=("arbitrary",) * n_axes, vmem_limit_bytes=limit)


def _nbytes(shape, dtype):
    return math.prod(shape) * jnp.dtype(dtype).itemsize


def _silu(x):
    return x * jax.nn.sigmoid(x)


def _pick(total, prefs):
    for p in prefs:
        if total % p == 0:
            return p
    return total


def _ada_kernel(c_ref, w_ref, b_ref, o_ref):
    s = _silu(c_ref[...])
    acc = jnp.dot(s.astype(BF16), w_ref[...].astype(BF16), preferred_element_type=F32)
    o_ref[...] = acc + b_ref[...]


def _ada_modulation(c_all, ada_w, ada_b, *, bn):
    depth, d, n = ada_w.shape
    r = c_all.shape[0]
    est = 2 * (_nbytes((d, bn), F32) + _nbytes((r, bn), F32)) + _nbytes((d, bn), BF16) + _nbytes((r, d), F32)
    return pl.pallas_call(
        _ada_kernel,
        out_shape=jax.ShapeDtypeStruct((depth, r, n), F32),
        grid=(depth, n // bn),
        in_specs=[
            pl.BlockSpec((r, d), lambda l, j: (0, 0)),
            pl.BlockSpec((None, d, bn), lambda l, j: (l, 0, j)),
            pl.BlockSpec((None, 1, bn), lambda l, j: (l, 0, j)),
        ],
        out_specs=pl.BlockSpec((None, r, bn), lambda l, j: (l, 0, j)),
        compiler_params=_params(est, 2),
        name="ada_modulation",
    )(c_all, ada_w, ada_b.reshape(depth, 1, n))


def _norm_mod_kernel(x_ref, g_ref, sc_ref, sh_ref, o_ref):
    x = x_ref[...]
    y = x * lax.rsqrt(jnp.mean(x * x, axis=-1, keepdims=True) + EPS)
    h = (y * g_ref[...]) * (1.0 + sc_ref[...]) + sh_ref[...]
    o_ref[...] = h.astype(o_ref.dtype)


def _norm_mod(x, g, mod, sc_chunk, sh_chunk, *, bm):
    bn_, t, d = x.shape
    tg = mod.shape[1]
    mg = 1 if tg == 1 else bm

    def mod_map(chunk):
        return lambda b, i: (b, 0 if tg == 1 else i, chunk)

    est = 2 * (_nbytes((bm, d), F32) + _nbytes((bm, d), BF16)) + 2 * _nbytes((bm, d), F32)
    return pl.pallas_call(
        _norm_mod_kernel,
        out_shape=jax.ShapeDtypeStruct((bn_, t, d), BF16),
        grid=(bn_, t // bm),
        in_specs=[
            pl.BlockSpec((None, bm, d), lambda b, i: (b, i, 0)),
            pl.BlockSpec((1, d), lambda b, i: (0, 0)),
            pl.BlockSpec((None, mg, d), mod_map(sc_chunk)),
            pl.BlockSpec((None, mg, d), mod_map(sh_chunk)),
        ],
        out_specs=pl.BlockSpec((None, bm, d), lambda b, i: (b, i, 0)),
        compiler_params=_params(est, 2),
        name="norm_mod",
    )(x, g.reshape(1, d), mod, mod)


def _final_norm_kernel(x_ref, g_ref, o_ref):
    x = x_ref[...]
    y = x * lax.rsqrt(jnp.mean(x * x, axis=-1, keepdims=True) + EPS)
    o_ref[...] = y * g_ref[...]


def _final_norm(x, g, *, bm):
    bn_, t, d = x.shape
    est = 6 * _nbytes((bm, d), F32)
    return pl.pallas_call(
        _final_norm_kernel,
        out_shape=jax.ShapeDtypeStruct((bn_, t, d), F32),
        grid=(bn_, t // bm),
        in_specs=[
            pl.BlockSpec((None, bm, d), lambda b, i: (b, i, 0)),
            pl.BlockSpec((1, d), lambda b, i: (0, 0)),
        ],
        out_specs=pl.BlockSpec((None, bm, d), lambda b, i: (b, i, 0)),
        compiler_params=_params(est, 2),
        name="final_norm",
    )(x, g.reshape(1, d))


def _linear_kernel(*refs, nk, mode):
    x_ref, w_ref = refs[0], refs[1]
    pos = 2
    w2_ref = res_ref = gate_ref = None
    if mode == "swiglu":
        w2_ref = refs[pos]
        pos += 1
    if mode == "resid":
        res_ref, gate_ref = refs[pos], refs[pos + 1]
        pos += 2
    o_ref = refs[pos]
    scratch = refs[pos + 1:]

    def finalize(acc, acc2=None):
        if mode == "swiglu":
            o_ref[...] = (_silu(acc) * acc2).astype(o_ref.dtype)
        elif mode == "resid":
            o_ref[...] = (res_ref[...] + gate_ref[...] * acc).astype(o_ref.dtype)
        else:
            o_ref[...] = acc.astype(o_ref.dtype)

    if nk == 1:
        wb_ref = scratch[0]
        first = jnp.logical_and(pl.program_id(1) == 0, pl.program_id(2) == 0)

        @pl.when(first)
        def _():
            wb_ref[...] = w_ref[...].astype(BF16)
            if mode == "swiglu":
                scratch[1][...] = w2_ref[...].astype(BF16)

        x = x_ref[...]
        acc = jnp.dot(x, wb_ref[...], preferred_element_type=F32)
        acc2 = None
        if mode == "swiglu":
            acc2 = jnp.dot(x, scratch[1][...], preferred_element_type=F32)
        finalize(acc, acc2)
    else:
        acc_ref = scratch[0]
        k = pl.program_id(3)
        part = jnp.dot(x_ref[...], w_ref[...].astype(BF16), preferred_element_type=F32)

        @pl.when(k == 0)
        def _():
            acc_ref[...] = part

        @pl.when(k > 0)
        def _():
            acc_ref[...] += part

        @pl.when(k == nk - 1)
        def _():
            finalize(acc_ref[...])


def _linear(x, w, *, layer, n, col_map, out_dtype, bm, bn, bk=None, mode="plain",
            col_map2=None, res=None, gate=None, gate_chunk=0, name="linear"):
    bn_, t, kdim = x.shape
    bk = kdim if bk is None else bk
    nk = kdim // bk
    assert kdim % bk == 0 and t % bm == 0 and n % bn == 0
    assert mode != "swiglu" or nk == 1
    grid = (n // bn, bn_, t // bm, nk)

    in_specs = [
        pl.BlockSpec((None, bm, bk), lambda j, b, i, k: (b, i, k)),
        pl.BlockSpec((None, bk, bn), lambda j, b, i, k: (layer, k, col_map(j))),
    ]
    args = [x, w]
    est = 2 * (_nbytes((bm, bk), BF16) + _nbytes((bk, bn), F32) + _nbytes((bm, bn), out_dtype))
    est += _nbytes((bk, bn), BF16) + 2 * _nbytes((bm, bn), F32)
    if mode == "swiglu":
        in_specs.append(pl.BlockSpec((None, bk, bn), lambda j, b, i, k: (layer, k, col_map2(j))))
        args.append(w)
        est += 2 * _nbytes((bk, bn), F32) + _nbytes((bk, bn), BF16) + _nbytes((bm, bn), F32)
    if mode == "resid":
        tg = gate.shape[1]
        mg = 1 if tg == 1 else bm
        nb = n // bn
        in_specs.append(pl.BlockSpec((None, bm, bn), lambda j, b, i, k: (b, i, j)))
        in_specs.append(pl.BlockSpec(
            (None, mg, bn), lambda j, b, i, k: (b, 0 if tg == 1 else i, gate_chunk * nb + j)))
        args += [res, gate]
        est += 4 * _nbytes((bm, bn), F32)

    if nk == 1:
        scratch = [pltpu.VMEM((bk, bn), BF16)]
        if mode == "swiglu":
            scratch.append(pltpu.VMEM((bk, bn), BF16))
    else:
        scratch = [pltpu.VMEM((bm, bn), F32)]

    return pl.pallas_call(
        functools.partial(_linear_kernel, nk=nk, mode=mode),
        out_shape=jax.ShapeDtypeStruct((bn_, t, n), out_dtype),
        grid=grid,
        in_specs=in_specs,
        out_specs=pl.BlockSpec((None, bm, bn), lambda j, b, i, k: (b, i, j)),
        scratch_shapes=scratch,
        compiler_params=_params(est, 4),
        name=name,
    )(*args)


def _attn_prompt_kernel(q_ref, kc_ref, vc_ref, kp_ref, vp_ref, o_ref, lse_ref, *, heads):
    c = pl.program_id(2)
    hg = pl.program_id(3)
    nblk = q_ref.shape[0] // N_BACK
    scale = HEAD_DIM ** -0.5

    @pl.when(hg == 0)
    def _():
        lse_ref[...] = jnp.zeros_like(lse_ref)

    row = lax.broadcasted_iota(jnp.int32, (N_BACK, N_BACK), 0)
    col = lax.broadcasted_iota(jnp.int32, (N_BACK, N_BACK), 1)
    lane = lax.broadcasted_iota(jnp.int32, (N_BACK, V7X_LANES), 1)
    nt = (((1,), (1,)), ((), ()))
    neg = -jnp.inf

    for i in range(nblk):
        rows = slice(i * N_BACK, (i + 1) * N_BACK)
        prev_ok = (col >= row) if i > 0 else jnp.logical_and(col >= row, c > 0)
        cur_ok = col <= row
        lse_tile = lse_ref[rows, :]
        for h in range(heads):
            lanes = slice(h * HEAD_DIM, (h + 1) * HEAD_DIM)
            q = q_ref[rows, lanes]
            if i == 0:
                kp = kp_ref[:, lanes]
                vp = vp_ref[:, lanes]
            else:
                prows = slice((i - 1) * N_BACK, i * N_BACK)
                kp = kc_ref[prows, lanes]
                vp = vc_ref[prows, lanes]
            kc = kc_ref[rows, lanes]
            vc = vc_ref[rows, lanes]
            s_p = lax.dot_general(q, kp.astype(BF16), nt, preferred_element_type=F32) * scale
            s_c = lax.dot_general(q, kc.astype(BF16), nt, preferred_element_type=F32) * scale
            s_p = jnp.where(prev_ok, s_p, neg)
            s_c = jnp.where(cur_ok, s_c, neg)
            m = jnp.maximum(jnp.max(s_p, axis=-1, keepdims=True), jnp.max(s_c, axis=-1, keepdims=True))
            p_p = jnp.exp(s_p - m)
            p_c = jnp.exp(s_c - m)
            den = jnp.sum(p_p, axis=-1, keepdims=True) + jnp.sum(p_c, axis=-1, keepdims=True)
            acc = jnp.dot(p_p.astype(BF16), vp.astype(BF16), preferred_element_type=F32)
            acc += jnp.dot(p_c.astype(BF16), vc.astype(BF16), preferred_element_type=F32)
            o_ref[rows, lanes] = acc / den
            lse_tile = jnp.where(lane == hg * heads + h, m + jnp.log(den), lse_tile)
        lse_ref[rows, :] = lse_tile


def _attn_prompt(q, kv, g, *, lc, heads):
    dil = DILATIONS[g]
    ngroups = len(DILATIONS)
    b, s, qc = q.shape
    hw = qc // ngroups
    nh = hw // HEAD_DIM
    l = s // dil
    assert l % N_BACK == 0 and l % lc == 0 and nh % heads == 0 and nh <= V7X_LANES
    hb = heads * HEAD_DIM
    nhg = nh // heads
    bpc = lc // N_BACK
    qv = q.reshape(b, l, dil * qc)
    kvv = kv.reshape(b, l, dil * 2 * hw)

    def qmap(bi, r, c, hg):
        return (bi, c, (r * ngroups + g) * nhg + hg)

    def kmap(bi, r, c, hg):
        return (bi, c, (2 * r) * nhg + hg)

    def vmap(bi, r, c, hg):
        return (bi, c, (2 * r + 1) * nhg + hg)

    def kpmap(bi, r, c, hg):
        return (bi, jnp.maximum(c * bpc - 1, 0), (2 * r) * nhg + hg)

    def vpmap(bi, r, c, hg):
        return (bi, jnp.maximum(c * bpc - 1, 0), (2 * r + 1) * nhg + hg)

    est = 2 * (_nbytes((lc, hb), BF16) + 3 * _nbytes((lc, hb), F32) + 2 * _nbytes((N_BACK, hb), F32)
               + _nbytes((lc, V7X_LANES), F32))
    o, lse = pl.pallas_call(
        functools.partial(_attn_prompt_kernel, heads=heads),
        out_shape=(jax.ShapeDtypeStruct((b, l, dil * hw), F32),
                   jax.ShapeDtypeStruct((b, l, dil * V7X_LANES), F32)),
        grid=(b, dil, l // lc, nhg),
        in_specs=[
            pl.BlockSpec((None, lc, hb), qmap),
            pl.BlockSpec((None, lc, hb), kmap),
            pl.BlockSpec((None, lc, hb), vmap),
            pl.BlockSpec((None, N_BACK, hb), kpmap),
            pl.BlockSpec((None, N_BACK, hb), vpmap),
        ],
        out_specs=(pl.BlockSpec((None, lc, hb), lambda bi, r, c, hg: (bi, c, r * nhg + hg)),
                   pl.BlockSpec((None, lc, V7X_LANES), lambda bi, r, c, hg: (bi, c, r))),
        compiler_params=_params(est, 4),
        name=f"attn_prompt_g{g}",
    )(qv, kvv, kvv, kvv, kvv)
    return o.reshape(b, s, hw), lse.reshape(b, s, V7X_LANES)


def _attn_cached_kernel(q_ref, kn_ref, vn_ref, kb_ref, vb_ref, o_ref, lse_ref, *, heads, dil, t_new):
    hg = pl.program_id(1)
    w = kb_ref.shape[0]
    tp = q_ref.shape[0]
    scale = HEAD_DIM ** -0.5
    nt = (((1,), (1,)), ((), ()))
    neg = -jnp.inf

    @pl.when(hg == 0)
    def _():
        lse_ref[...] = jnp.zeros_like(lse_ref)

    dist_b = w + lax.broadcasted_iota(jnp.int32, (tp, w), 0) - lax.broadcasted_iota(jnp.int32, (tp, w), 1)
    ok_b = jnp.logical_and(dist_b % dil == 0, dist_b <= dil * N_BACK)
    kj_n = lax.broadcasted_iota(jnp.int32, (tp, tp), 1)
    dist_n = lax.broadcasted_iota(jnp.int32, (tp, tp), 0) - kj_n
    ok_n = jnp.logical_and(jnp.logical_and(dist_n >= 0, dist_n % dil == 0), kj_n < t_new)
    lane = lax.broadcasted_iota(jnp.int32, (tp, V7X_LANES), 1)

    lse_tile = lse_ref[...]
    for h in range(heads):
        lanes = slice(h * HEAD_DIM, (h + 1) * HEAD_DIM)
        q = q_ref[:, lanes].astype(BF16)
        s_b = lax.dot_general(q, kb_ref[:, lanes].astype(BF16), nt, preferred_element_type=F32) * scale
        s_n = lax.dot_general(q, kn_ref[:, lanes].astype(BF16), nt, preferred_element_type=F32) * scale
        s_b = jnp.where(ok_b, s_b, neg)
        s_n = jnp.where(ok_n, s_n, neg)
        m = jnp.maximum(jnp.max(s_b, axis=-1, keepdims=True), jnp.max(s_n, axis=-1, keepdims=True))
        p_b = jnp.exp(s_b - m)
        p_n = jnp.exp(s_n - m)
        den = jnp.sum(p_b, axis=-1, keepdims=True) + jnp.sum(p_n, axis=-1, keepdims=True)
        acc = jnp.dot(p_b.astype(BF16), vb_ref[:, lanes].astype(BF16), preferred_element_type=F32)
        acc += jnp.dot(p_n.astype(BF16), vn_ref[:, lanes].astype(BF16), preferred_element_type=F32)
        o_ref[:, lanes] = acc / den
        lse_tile = jnp.where(lane == hg * heads + h, m + jnp.log(den), lse_tile)
    lse_ref[...] = lse_tile


def _attn_cached(q8, kvn8, buf, g, *, heads, t_new):
    dil = DILATIONS[g]
    ngroups = len(DILATIONS)
    b, tp, qc = q8.shape
    hw = qc // ngroups
    nh = hw // HEAD_DIM
    w = buf.shape[1]
    hb = heads * HEAD_DIM
    nhg = nh // heads
    est = 2 * (2 * _nbytes((w, hb), F32) + 5 * _nbytes((tp, hb), F32)) + 2 * _nbytes((w, hb), F32)
    return pl.pallas_call(
        functools.partial(_attn_cached_kernel, heads=heads, dil=dil, t_new=t_new),
        out_shape=(jax.ShapeDtypeStruct((b, tp, hw), F32),
                   jax.ShapeDtypeStruct((b, tp, V7X_LANES), F32)),
        grid=(b, nhg),
        in_specs=[
            pl.BlockSpec((None, tp, hb), lambda bi, hg: (bi, 0, g * nhg + hg)),
            pl.BlockSpec((None, tp, hb), lambda bi, hg: (bi, 0, hg)),
            pl.BlockSpec((None, tp, hb), lambda bi, hg: (bi, 0, nhg + hg)),
            pl.BlockSpec((None, w, hb), lambda bi, hg: (bi, 0, hg)),
            pl.BlockSpec((None, w, hb), lambda bi, hg: (bi, 0, nhg + hg)),
        ],
        out_specs=(pl.BlockSpec((None, tp, hb), lambda bi, hg: (bi, 0, hg)),
                   pl.BlockSpec((None, tp, V7X_LANES), lambda bi, hg: (bi, 0, 0))),
        compiler_params=_params(est, 2),
        name=f"attn_cached_g{g}",
    )(q8, kvn8, kvn8, buf, buf)


def _shift_kernel(cur_ref, nxt_ref, new_ref, o_ref, *, t_new):
    j = pl.program_id(1)
    last = j == pl.num_programs(1) - 1
    rb = cur_ref.shape[0]
    sub = V7X_SUBLANES
    rolled = pltpu.roll(cur_ref[...], rb - t_new, 0)
    src = jnp.where(last, new_ref[...], nxt_ref[...])
    tail = pltpu.roll(src, sub - t_new, 0)
    rowi = lax.broadcasted_iota(jnp.int32, (sub, cur_ref.shape[1]), 0)
    o_ref[0:rb - sub, :] = rolled[0:rb - sub, :]
    o_ref[rb - sub:rb, :] = jnp.where(rowi < sub - t_new, rolled[rb - sub:rb, :], tail)


def _shift_append(buf, new8, *, t_new, rb):
    b, w, c = buf.shape
    sub = V7X_SUBLANES
    assert w % rb == 0 and rb % sub == 0 and t_new <= sub
    nsub = w // sub
    est = 4 * _nbytes((rb, c), F32) + 4 * _nbytes((sub, c), F32) + 2 * _nbytes((rb, c), F32)
    return pl.pallas_call(
        functools.partial(_shift_kernel, t_new=t_new),
        out_shape=jax.ShapeDtypeStruct((b, w, c), F32),
        grid=(b, w // rb),
        in_specs=[
            pl.BlockSpec((None, rb, c), lambda bi, j: (bi, j, 0)),
            pl.BlockSpec((None, sub, c), lambda bi, j: (bi, jnp.minimum((j + 1) * (rb // sub), nsub - 1), 0)),
            pl.BlockSpec((None, sub, c), lambda bi, j: (bi, 0, 0)),
        ],
        out_specs=pl.BlockSpec((None, rb, c), lambda bi, j: (bi, j, 0)),
        compiler_params=_params(est, 2),
        name="kv_shift_append",
    )(buf, buf, new8)


def _merge_kernel(o0_ref, o1_ref, o2_ref, l0_ref, l1_ref, l2_ref, out_ref, *, nh):
    l0, l1, l2 = l0_ref[...], l1_ref[...], l2_ref[...]
    m = jnp.maximum(jnp.maximum(l0, l1), l2)
    e0, e1, e2 = jnp.exp(l0 - m), jnp.exp(l1 - m), jnp.exp(l2 - m)
    tot = e0 + e1 + e2
    w0, w1, w2 = e0 / tot, e1 / tot, e2 / tot
    for h in range(nh):
        lanes = slice(h * HEAD_DIM, (h + 1) * HEAD_DIM)
        acc = w0[:, h:h + 1] * o0_ref[:, lanes]
        acc += w1[:, h:h + 1] * o1_ref[:, lanes]
        acc += w2[:, h:h + 1] * o2_ref[:, lanes]
        out_ref[:, lanes] = acc.astype(out_ref.dtype)


def _merge(os_, lses, *, bm, out_dtype):
    b, t, hw = os_[0].shape
    nh = hw // HEAD_DIM
    ospec = pl.BlockSpec((None, bm, hw), lambda bi, i: (bi, i, 0))
    lspec = pl.BlockSpec((None, bm, V7X_LANES), lambda bi, i: (bi, i, 0))
    est = 2 * (3 * _nbytes((bm, hw), F32) + 3 * _nbytes((bm, V7X_LANES), F32) + _nbytes((bm, hw), out_dtype))
    return pl.pallas_call(
        functools.partial(_merge_kernel, nh=nh),
        out_shape=jax.ShapeDtypeStruct((b, t, hw), out_dtype),
        grid=(b, t // bm),
        in_specs=[ospec, ospec, ospec, lspec, lspec, lspec],
        out_specs=ospec,
        compiler_params=_params(est, 2),
        name="attn_merge",
    )(*os_, *lses)


def _hgrn_kernel(*refs, heads, chunk, layer, has_state):
    if has_state:
        zq_ref, zf_ref, zv_ref, zg_ref, lb_ref, ng_ref, s0_ref = refs[:7]
        y_ref, st_out_ref, st_ref = refs[7:]
    else:
        zq_ref, zf_ref, zv_ref, zg_ref, lb_ref, ng_ref = refs[:6]
        s0_ref = None
        y_ref, st_out_ref, st_ref = refs[6:]
    tstep = pl.program_id(2)
    nsteps = pl.num_programs(2)
    tb = zq_ref.shape[0]

    @pl.when(tstep == 0)
    def _():
        for h in range(heads):
            if has_state:
                st_ref[h] = s0_ref[h].T
            else:
                st_ref[h] = jnp.zeros((HGRN_DK, HGRN_DK), F32)

    logits = lb_ref[...]
    ex = jnp.exp(logits - jnp.max(logits, axis=0, keepdims=True))
    sm = ex / jnp.sum(ex, axis=0, keepdims=True)
    lb = jnp.zeros_like(sm[0:1])
    for l in range(1, layer + 1):
        lb = lb + sm[l:l + 1]
    ng = ng_ref[...]

    rowi = lax.broadcasted_iota(jnp.int32, (chunk, HGRN_DK), 0)
    nt = (((1,), (1,)), ((), ()))
    tn = (((0,), (0,)), ((), ()))

    def body(ci, carry):
        rows = pl.ds(pl.multiple_of(ci * chunk, chunk), chunk)
        zq = zq_ref[rows, :]
        zf = zf_ref[rows, :]
        zv = zv_ref[rows, :]
        zg = zg_ref[rows, :]
        q_all = _silu(zq)
        f_all = lb + (1.0 - lb) * jax.nn.sigmoid(zf)
        lf_all = jnp.log(f_all)
        k_all = 1.0 - f_all
        og_all = ng * _silu(zg)
        for h in range(heads):
            lanes = slice(h * HGRN_DK, (h + 1) * HGRN_DK)
            q, k, v, lf = q_all[:, lanes], k_all[:, lanes], zv[:, lanes], lf_all[:, lanes]
            a = jnp.zeros((chunk, HGRN_DK), F32)
            for s in range(chunk):
                a = a + jnp.where(rowi >= s, lf[s:s + 1, :], 0.0)
            st = st_ref[h]
            qe = q * jnp.exp(a)
            o = lax.dot_general(qe.astype(BF16), st.astype(BF16), nt, preferred_element_type=F32)
            for s in range(chunk):
                e = jnp.where(rowi >= s, jnp.exp(a - a[s:s + 1, :]), 0.0)
                wgt = (q * k[s:s + 1, :]) * e
                o = o + jnp.sum(wgt, axis=-1, keepdims=True) * v[s:s + 1, :]
            a_last = a[chunk - 1:chunk, :]
            kd = k * jnp.exp(a_last - a)
            upd = lax.dot_general(v.astype(BF16), kd.astype(BF16), tn, preferred_element_type=F32)
            st_ref[h] = st * jnp.exp(a_last) + upd
            on = o * lax.rsqrt(jnp.mean(o * o, axis=-1, keepdims=True) + EPS)
            y_ref[rows, lanes] = (on * og_all[:, lanes]).astype(y_ref.dtype)
        return carry

    lax.fori_loop(0, tb // chunk, body, 0)

    @pl.when(tstep == nsteps - 1)
    def _():
        for h in range(heads):
            st_out_ref[h] = st_ref[h].T


def _hgrn(z, lb_logits, norm_g, s0, *, layer, heads, tb, out_dtype):
    b, t, d4 = z.shape
    d = d4 // 4
    nh = d // HGRN_DK
    nhg = nh // heads
    hb = heads * HGRN_DK
    chunk = math.gcd(t, HGRN_CHUNK)
    depth = lb_logits.shape[0]
    assert t % tb == 0 and tb % chunk == 0 and nh % heads == 0

    def zspec(part):
        return pl.BlockSpec((None, tb, hb), lambda bi, hg, i: (bi, i, part * nhg + hg))

    in_specs = [zspec(0), zspec(1), zspec(2), zspec(3),
                pl.BlockSpec((depth, hb), lambda bi, hg, i: (0, hg)),
                pl.BlockSpec((1, hb), lambda bi, hg, i: (0, hg))]
    args = [z, z, z, z, lb_logits, norm_g.reshape(1, d)]
    sspec = pl.BlockSpec((None, heads, HGRN_DK, HGRN_DK), lambda bi, hg, i: (bi, hg, 0, 0))
    if s0 is not None:
        in_specs.append(sspec)
        args.append(s0)
    est = (2 * (4 * _nbytes((tb, hb), F32) + _nbytes((tb, hb), out_dtype))
           + 5 * _nbytes((heads, HGRN_DK, HGRN_DK), F32))
    return pl.pallas_call(
        functools.partial(_hgrn_kernel, heads=heads, chunk=chunk, layer=layer, has_state=s0 is not None),
        out_shape=(jax.ShapeDtypeStruct((b, t, d), out_dtype),
                   jax.ShapeDtypeStruct((b, nh, HGRN_DK, HGRN_DK), F32)),
        grid=(b, nhg, t // tb),
        in_specs=in_specs,
        out_specs=(pl.BlockSpec((None, tb, hb), lambda bi, hg, i: (bi, i, hg)), sspec),
        scratch_shapes=[pltpu.VMEM((heads, HGRN_DK, HGRN_DK), F32)],
        compiler_params=_params(est, 3),
        name="hgrn2",
    )(*args)


def _trunk(x, mods, a_bufs, b_states, weights, *, nb, t_seq):
    (norm_mix_g, norm_ffn_g, a_w_qkv, a_w_o, b_w_in, b_lb_logits, b_norm_g, b_w_o,
     ffn_w_in, ffn_w_out, final_g) = weights
    bn_, t, d = x.shape
    cached = a_bufs is not None
    depth = norm_mix_g.shape[0]
    d_ff = ffn_w_out.shape[1]
    ngroups = len(DILATIONS)
    hw = a_w_o.shape[1]
    nheads = hw // HEAD_DIM
    bm = _pick(t, (1024, 512, 256, 128, 64, 32, 16, 8))
    bm_e = _pick(t, (256, 128, 64, 32, 16, 8))
    bn = 512
    ident = lambda j: j
    sub = V7X_SUBLANES
    kv_out, states = [], []

    def per_seq_pad(arr):
        return jnp.pad(arr.reshape(nb, t_seq, arr.shape[-1]), ((0, 0), (0, sub - t_seq), (0, 0)))

    for layer in range(depth):
        i = layer // N_MIXERS
        h = _norm_mod(x, norm_mix_g[layer], mods[layer], 1, 0, bm=bm_e)
        if layer % N_MIXERS == 0:
            q = _linear(h, a_w_qkv, layer=i, n=ngroups * hw, col_map=ident,
                        out_dtype=F32 if cached else BF16, bm=bm, bn=bn, name="a_q")
            half = hw // bn
            os_, lses = [], []
            for g in range(ngroups):
                def kv_cols(j, g=g):
                    return jnp.where(j < half, (ngroups + g) * half + j, (2 * ngroups + g) * half + (j - half))
                kv = _linear(h, a_w_qkv, layer=i, n=2 * hw, col_map=kv_cols, out_dtype=F32,
                             bm=bm, bn=bn, name=f"a_kv{g}")
                if not cached:
                    o, lse = _attn_prompt(q, kv, g, lc=min(t // DILATIONS[g], 512), heads=4)
                    keep = min(DILATIONS[g] * N_BACK, t)
                    kv_keep = kv if keep == t else kv[:, t - keep:]
                    kv_out.append(kv_keep.reshape(1, bn_, keep, 2, nheads, HEAD_DIM))
                else:
                    w = a_bufs[g].shape[2]
                    buf = a_bufs[g][i].reshape(nb, w, 2 * hw)
                    kvn8 = per_seq_pad(kv)
                    o8, lse8 = _attn_cached(per_seq_pad(q), kvn8, buf, g, heads=4, t_new=t_seq)
                    o = o8[:, :t_seq].reshape(1, nb * t_seq, hw)
                    lse = lse8[:, :t_seq].reshape(1, nb * t_seq, V7X_LANES)
                    newbuf = _shift_append(buf, kvn8, t_new=t_seq, rb=min(w, 256))
                    kv_out.append(newbuf.reshape(1, nb, w, 2, nheads, HEAD_DIM))
                os_.append(o)
                lses.append(lse)
            merged = _merge(os_, lses, bm=bm_e, out_dtype=BF16)
            x = _linear(merged, a_w_o, layer=i, n=d, col_map=ident, out_dtype=F32, bm=bm, bn=bn,
                        mode="resid", res=x, gate=mods[layer], gate_chunk=2, name="a_o")
        else:
            z = _linear(h, b_w_in, layer=i, n=4 * d, col_map=ident, out_dtype=F32, bm=bm, bn=bn, name="b_in")
            if not cached:
                y, s_t = _hgrn(z, b_lb_logits, b_norm_g[i], None, layer=layer, heads=4,
                               tb=min(t, 512), out_dtype=BF16)
            else:
                yb, s_t = _hgrn(z.reshape(nb, t_seq, 4 * d), b_lb_logits, b_norm_g[i], b_states[i],
                                layer=layer, heads=4, tb=t_seq, out_dtype=F32)
                y = yb.reshape(1, nb * t_seq, d).astype(BF16)
            states.append(s_t)
            x = _linear(y, b_w_o, layer=i, n=d, col_map=ident, out_dtype=F32, bm=bm, bn=bn,
                        mode="resid", res=x, gate=mods[layer], gate_chunk=2, name="b_o")
        h = _norm_mod(x, norm_ffn_g[layer], mods[layer], 4, 3, bm=bm_e)
        bnf = 256
        nff = d_ff // bnf
        act = _linear(h, ffn_w_in, layer=layer, n=d_ff, col_map=ident, col_map2=lambda j: nff + j,
                      out_dtype=BF16, bm=bm, bn=bnf, mode="swiglu", name="ffn_in")
        half_ff = d_ff // 2
        bk = half_ff if (d_ff % 2 == 0 and half_ff % V7X_LANES == 0) else d_ff
        x = _linear(act, ffn_w_out, layer=layer, n=d, col_map=ident, out_dtype=F32,
                    bm=min(bm, 512), bn=bn, bk=bk, mode="resid", res=x, gate=mods[layer], gate_chunk=5,
                    name="ffn_out")
    y = _final_norm(x, final_g, bm=bm_e)
    return y, kv_out, states


def kernel(x_prompt, x_sample, state_a_kv_w128, state_a_kv_w512, state_a_kv_w2048, state_b_rec,
           c_prompt, c_sample, ada_w, ada_b, norm_mix_g, norm_ffn_g, a_w_qkv, a_w_o,
           b_w_in, b_lb_logits, b_norm_g, b_w_o, ffn_w_in, ffn_w_out, final_g):
    bp, sp, d = x_prompt.shape
    bs, ts, _ = x_sample.shape
    depth = ada_w.shape[0]
    weights = (norm_mix_g, norm_ffn_g, a_w_qkv, a_w_o, b_w_in, b_lb_logits, b_norm_g, b_w_o,
               ffn_w_in, ffn_w_out, final_g)

    rows = bp + bs
    rpad = -rows % V7X_SUBLANES
    c_all = jnp.pad(jnp.concatenate([c_prompt, c_sample], axis=0), ((0, rpad), (0, 0)))
    mod = _ada_modulation(c_all, ada_w, ada_b, bn=512)
    mods_p = [mod[l, :bp].reshape(bp, 1, 6 * d) for l in range(depth)]
    mods_s = [jnp.repeat(mod[l, bp:rows], ts, axis=0).reshape(1, bs * ts, 6 * d) for l in range(depth)]

    y_p, kv_p, st_p = _trunk(x_prompt, mods_p, None, None, weights, nb=bp, t_seq=sp)
    y_s, kv_s, st_s = _trunk(x_sample.reshape(1, bs * ts, d), mods_s,
                             (state_a_kv_w128, state_a_kv_w512, state_a_kv_w2048), state_b_rec,
                             weights, nb=bs, t_seq=ts)
    return (y_p, y_s.reshape(bs, ts, d), kv_p[0], kv_s[0], kv_p[1], kv_s[1], kv_p[2], kv_s[2],
            jnp.stack(st_p, axis=0), jnp.stack(st_s, axis=0))
```

```python
import functools
import math

import jax
import jax.numpy as jnp
from jax import lax
from jax.experimental import pallas as pl
from jax.experimental.pallas import tpu as pltpu

F32 = jnp.float32
BF16 = jnp.bfloat16

EPS = 1e-6
DILATIONS = (1, 4, 16)
N_BACK = 128
HEAD_DIM = 128
HGRN_DK = 128
HGRN_CHUNK = 16
LOG2_E = 1.4426950408889634
N_MIXERS = 2
V7X_LANES = 128
V7X_SUBLANES = 8
V7X_SCOPED_VMEM_BYTES = 60000 * 1024


def _params(nbytes, n_axes):
    limit = int(min(V7X_SCOPED_VMEM_BYTES, max(2 * nbytes, 16 * 1024 * 1024)))
    return pltpu.CompilerParams(dimension_semantics=("arbitrary",) * n_axes, vmem_limit_bytes=limit)


def _nbytes(shape, dtype):
    return math.prod(shape) * jnp.dtype(dtype).itemsize


def _silu(x):
    return x * jax.nn.sigmoid(x)


def _pick(total, prefs):
    for p in prefs:
        if total % p == 0:
            return p
    return total


def _ada_kernel(c_ref, w_ref, b_ref, o_ref):
    s = _silu(c_ref[...])
    acc = jnp.dot(s.astype(BF16), w_ref[...].astype(BF16), preferred_element_type=F32)
    o_ref[...] = acc + b_ref[...]


def _ada_modulation(c_all, ada_w, ada_b, *, bn):
    depth, d, n = ada_w.shape
    r = c_all.shape[0]
    est = 2 * (_nbytes((d, bn), F32) + _nbytes((r, bn), F32)) + _nbytes((d, bn), BF16) + _nbytes((r, d), F32)
    return pl.pallas_call(
        _ada_kernel,
        out_shape=jax.ShapeDtypeStruct((depth, r, n), F32),
        grid=(depth, n // bn),
        in_specs=[
            pl.BlockSpec((r, d), lambda l, j: (0, 0)),
            pl.BlockSpec((None, d, bn), lambda l, j: (l, 0, j)),
            pl.BlockSpec((None, 1, bn), lambda l, j: (l, 0, j)),
        ],
        out_specs=pl.BlockSpec((None, r, bn), lambda l, j: (l, 0, j)),
        compiler_params=_params(est, 2),
        name="ada_modulation",
    )(c_all, ada_w, ada_b.reshape(depth, 1, n))


def _norm_mod_kernel(x_ref, g_ref, sc_ref, sh_ref, o_ref):
    x = x_ref[...]
    y = x * lax.rsqrt(jnp.mean(x * x, axis=-1, keepdims=True) + EPS)
    h = (y * g_ref[...]) * (1.0 + sc_ref[...]) + sh_ref[...]
    o_ref[...] = h.astype(o_ref.dtype)


def _norm_mod(x, g, mod, sc_chunk, sh_chunk, *, bm):
    bn_, t, d = x.shape
    tg = mod.shape[1]
    mg = 1 if tg == 1 else bm

    def mod_map(chunk):
        return lambda b, i: (b, 0 if tg == 1 else i, chunk)

    est = 2 * (_nbytes((bm, d), F32) + _nbytes((bm, d), BF16)) + 2 * _nbytes((bm, d), F32)
    return pl.pallas_call(
        _norm_mod_kernel,
        out_shape=jax.ShapeDtypeStruct((bn_, t, d), BF16),
        grid=(bn_, t // bm),
        in_specs=[
            pl.BlockSpec((None, bm, d), lambda b, i: (b, i, 0)),
            pl.BlockSpec((1, d), lambda b, i: (0, 0)),
            pl.BlockSpec((None, mg, d), mod_map(sc_chunk)),
            pl.BlockSpec((None, mg, d), mod_map(sh_chunk)),
        ],
        out_specs=pl.BlockSpec((None, bm, d), lambda b, i: (b, i, 0)),
        compiler_params=_params(est, 2),
        name="norm_mod",
    )(x, g.reshape(1, d), mod, mod)


def _final_norm_kernel(x_ref, g_ref, o_ref):
    x = x_ref[...]
    y = x * lax.rsqrt(jnp.mean(x * x, axis=-1, keepdims=True) + EPS)
    o_ref[...] = y * g_ref[...]


def _final_norm(x, g, *, bm):
    bn_, t, d = x.shape
    est = 6 * _nbytes((bm, d), F32)
    return pl.pallas_call(
        _final_norm_kernel,
        out_shape=jax.ShapeDtypeStruct((bn_, t, d), F32),
        grid=(bn_, t // bm),
        in_specs=[
            pl.BlockSpec((None, bm, d), lambda b, i: (b, i, 0)),
            pl.BlockSpec((1, d), lambda b, i: (0, 0)),
        ],
        out_specs=pl.BlockSpec((None, bm, d), lambda b, i: (b, i, 0)),
        compiler_params=_params(est, 2),
        name="final_norm",
    )(x, g.reshape(1, d))


def _to_bf16_kernel(w_ref, o_ref):
    o_ref[...] = w_ref[...].astype(BF16)


def _to_bf16(w, *, kb):
    l, k, n = w.shape
    assert k % kb == 0
    est = 2 * (_nbytes((kb, n), F32) + _nbytes((kb, n), BF16))
    spec = pl.BlockSpec((None, kb, n), lambda li, i: (li, i, 0))
    return pl.pallas_call(
        _to_bf16_kernel,
        out_shape=jax.ShapeDtypeStruct(w.shape, BF16),
        grid=(l, k // kb),
        in_specs=[spec],
        out_specs=spec,
        compiler_params=_params(est, 2),
        name="weight_to_bf16",
    )(w)


def _linear_kernel(*refs, nk, mode):
    x_ref, w_ref = refs[0], refs[1]
    pos = 2
    w2_ref = res_ref = gate_ref = None
    if mode == "swiglu":
        w2_ref = refs[pos]
        pos += 1
    if mode == "resid":
        res_ref, gate_ref = refs[pos], refs[pos + 1]
        pos += 2
    o_ref = refs[pos]
    scratch = refs[pos + 1:]

    def finalize(acc, acc2=None):
        if mode == "swiglu":
            o_ref[...] = (_silu(acc) * acc2).astype(o_ref.dtype)
        elif mode == "resid":
            o_ref[...] = (res_ref[...] + gate_ref[...] * acc).astype(o_ref.dtype)
        else:
            o_ref[...] = acc.astype(o_ref.dtype)

    cast = w_ref.dtype != BF16
    if nk == 1:
        wb_ref, wb2_ref = w_ref, w2_ref
        if cast:
            wb_ref = scratch[0]
            wb2_ref = scratch[1] if mode == "swiglu" else None
            first = jnp.logical_and(pl.program_id(1) == 0, pl.program_id(2) == 0)

            @pl.when(first)
            def _():
                wb_ref[...] = w_ref[...].astype(BF16)
                if mode == "swiglu":
                    wb2_ref[...] = w2_ref[...].astype(BF16)

        x = x_ref[...]
        acc = jnp.dot(x, wb_ref[...], preferred_element_type=F32)
        acc2 = None
        if mode == "swiglu":
            acc2 = jnp.dot(x, wb2_ref[...], preferred_element_type=F32)
        finalize(acc, acc2)
    else:
        acc_ref = scratch[0]
        k = pl.program_id(3)
        w = w_ref[...].astype(BF16) if cast else w_ref[...]
        part = jnp.dot(x_ref[...], w, preferred_element_type=F32)

        @pl.when(k == 0)
        def _():
            acc_ref[...] = part

        @pl.when(k > 0)
        def _():
            acc_ref[...] += part

        @pl.when(k == nk - 1)
        def _():
            finalize(acc_ref[...])


def _linear(x, w, *, layer, n, col_map, out_dtype, bm, bn, bk=None, mode="plain",
            col_map2=None, res=None, gate=None, gate_chunk=0, name="linear"):
    bn_, t, kdim = x.shape
    bk = kdim if bk is None else bk
    nk = kdim // bk
    assert kdim % bk == 0 and t % bm == 0 and n % bn == 0
    assert mode != "swiglu" or nk == 1
    grid = (n // bn, bn_, t // bm, nk)

    in_specs = [
        pl.BlockSpec((None, bm, bk), lambda j, b, i, k: (b, i, k)),
        pl.BlockSpec((None, bk, bn), lambda j, b, i, k: (layer, k, col_map(j))),
    ]
    args = [x, w]
    est = 2 * (_nbytes((bm, bk), BF16) + _nbytes((bk, bn), w.dtype) + _nbytes((bm, bn), out_dtype))
    est += _nbytes((bk, bn), BF16) + 2 * _nbytes((bm, bn), F32)
    if mode == "swiglu":
        in_specs.append(pl.BlockSpec((None, bk, bn), lambda j, b, i, k: (layer, k, col_map2(j))))
        args.append(w)
        est += 2 * _nbytes((bk, bn), F32) + _nbytes((bk, bn), BF16) + _nbytes((bm, bn), F32)
    if mode == "resid":
        tg = gate.shape[1]
        mg = 1 if tg == 1 else bm
        nb = n // bn
        in_specs.append(pl.BlockSpec((None, bm, bn), lambda j, b, i, k: (b, i, j)))
        in_specs.append(pl.BlockSpec(
            (None, mg, bn), lambda j, b, i, k: (b, 0 if tg == 1 else i, gate_chunk * nb + j)))
        args += [res, gate]
        est += 4 * _nbytes((bm, bn), F32)

    if nk > 1:
        scratch = [pltpu.VMEM((bm, bn), F32)]
    elif w.dtype != BF16:
        scratch = [pltpu.VMEM((bk, bn), BF16)] * (2 if mode == "swiglu" else 1)
    else:
        scratch = []

    return pl.pallas_call(
        functools.partial(_linear_kernel, nk=nk, mode=mode),
        out_shape=jax.ShapeDtypeStruct((bn_, t, n), out_dtype),
        grid=grid,
        in_specs=in_specs,
        out_specs=pl.BlockSpec((None, bm, bn), lambda j, b, i, k: (b, i, j)),
        scratch_shapes=scratch,
        compiler_params=_params(est, 4),
        name=name,
    )(*args)


def _attn_prompt_kernel(*refs, ngroups):
    q_refs, k_refs, v_refs = refs[:ngroups], refs[ngroups:2 * ngroups], refs[2 * ngroups:3 * ngroups]
    o_ref, og_ref, lg_ref = refs[3 * ngroups:]
    seq = o_ref.shape[0]
    scale = HEAD_DIM ** -0.5
    nt = (((1,), (1,)), ((), ()))
    neg = -jnp.inf
    row1 = lax.broadcasted_iota(jnp.int32, (N_BACK, N_BACK), 0)
    col1 = lax.broadcasted_iota(jnp.int32, (N_BACK, N_BACK), 1)
    row2 = lax.broadcasted_iota(jnp.int32, (N_BACK, 2 * N_BACK), 0)
    col2 = lax.broadcasted_iota(jnp.int32, (N_BACK, 2 * N_BACK), 1)
    ok_first = col1 <= row1
    ok_band = jnp.logical_or(jnp.logical_and(col2 < N_BACK, col2 >= row2),
                             jnp.logical_and(col2 >= N_BACK, col2 - N_BACK <= row2))
    ones = jnp.ones((N_BACK, HEAD_DIM), BF16)

    for g in range(ngroups):
        dil = DILATIONS[g]
        length = seq // dil
        nblk = length // N_BACK
        for r in range(dil):
            sub_rows = slice(0, length) if dil == 1 else pl.ds(r, length, stride=dil)
            q = q_refs[g][sub_rows, :].astype(BF16)
            k = k_refs[g][sub_rows, :].astype(BF16)
            v = v_refs[g][sub_rows, :].astype(BF16)
            scores = []
            for i in range(nblk):
                qi = q[i * N_BACK:(i + 1) * N_BACK]
                keys = k[max(i - 1, 0) * N_BACK:(i + 1) * N_BACK]
                scores.append(lax.dot_general(qi, keys, nt, preferred_element_type=F32) * scale)
            probs, maxes = [], []
            for i in range(nblk):
                s = jnp.where(ok_first if i == 0 else ok_band, scores[i], neg)
                m = jnp.max(s, axis=-1, keepdims=True)
                probs.append(jnp.exp(s - m).astype(BF16))
                maxes.append(m)
            for i in range(nblk):
                vals = v[max(i - 1, 0) * N_BACK:(i + 1) * N_BACK]
                vext = jnp.concatenate([vals, jnp.concatenate([ones] * (vals.shape[0] // N_BACK), axis=0)], axis=1)
                acc = jnp.dot(probs[i], vext, preferred_element_type=F32)
                den = acc[:, HEAD_DIM:]
                start = r + i * N_BACK * dil
                out_rows = slice(start, start + N_BACK) if dil == 1 else pl.ds(start, N_BACK, stride=dil)
                og_ref.at[g][out_rows, :] = acc[:, :HEAD_DIM] / den
                lg_ref.at[g][out_rows, :] = maxes[i] + jnp.log(den)

    rb = 256
    for c in range(seq // rb):
        rows = slice(c * rb, (c + 1) * rb)
        ls = [lg_ref[g, rows, :] for g in range(ngroups)]
        m = functools.reduce(jnp.maximum, ls)
        es = [jnp.exp(l - m) for l in ls]
        tot = functools.reduce(lambda a, b: a + b, es)
        num = functools.reduce(lambda a, b: a + b, [es[g] * og_ref[g, rows, :] for g in range(ngroups)])
        o_ref[rows, :] = (num / tot).astype(o_ref.dtype)


def _attn_prompt(q, kvs):
    ngroups = len(DILATIONS)
    b, s, qc = q.shape
    hw = qc // ngroups
    nh = hw // HEAD_DIM
    assert all(s % (d * N_BACK) == 0 for d in DILATIONS) and s % 256 == 0
    blk = (None, s, HEAD_DIM)
    in_specs = ([pl.BlockSpec(blk, (lambda bi, h, g=g: (bi, 0, g * nh + h))) for g in range(ngroups)]
                + [pl.BlockSpec(blk, lambda bi, h: (bi, 0, h))] * ngroups
                + [pl.BlockSpec(blk, lambda bi, h: (bi, 0, nh + h))] * ngroups)
    est = (2 * (3 * ngroups * _nbytes((s, HEAD_DIM), F32) + _nbytes((s, HEAD_DIM), BF16))
           + 2 * ngroups * _nbytes((s, HEAD_DIM), F32) + 6 * _nbytes((s, HEAD_DIM), F32))
    return pl.pallas_call(
        functools.partial(_attn_prompt_kernel, ngroups=ngroups),
        out_shape=jax.ShapeDtypeStruct((b, s, hw), BF16),
        grid=(b, nh),
        in_specs=in_specs,
        out_specs=pl.BlockSpec(blk, lambda bi, h: (bi, 0, h)),
        scratch_shapes=[pltpu.VMEM((ngroups, s, HEAD_DIM), F32), pltpu.VMEM((ngroups, s, HEAD_DIM), F32)],
        compiler_params=_params(est, 2),
        name="attn_prompt",
    )(*([q] * ngroups), *kvs, *kvs)


def _attn_cached_kernel(*refs, dil, nheads, has_carry, final):
    if has_carry:
        q_ref, kn_ref, vn_ref, kb_ref, vb_ref, m_in, l_in, a_in = refs[:8]
        outs = refs[8:]
    else:
        q_ref, kn_ref, vn_ref, kb_ref, vb_ref = refs[:5]
        outs = refs[5:]
    if final:
        o_ref, m_sc, l_sc, a_sc = outs
    else:
        m_out, l_out, a_out, m_sc, l_sc, a_sc = outs
    c = pl.program_id(1)
    nc = pl.num_programs(1)
    wc = kb_ref.shape[0]
    w_total = wc * nc
    nq = q_ref.shape[0]
    hshift = nheads.bit_length() - 1
    scale = HEAD_DIM ** -0.5
    nt = (((1,), (1,)), ((), ()))
    neg = -jnp.inf
    q = q_ref[...].astype(BF16)

    def attend(k, v, ok):
        s = lax.dot_general(q, k.astype(BF16), nt, preferred_element_type=F32) * scale
        s = jnp.where(ok, s, neg)
        m_old = m_sc[...]
        m_new = jnp.maximum(m_old, jnp.max(s, axis=-1, keepdims=True))
        alpha = jnp.exp(m_old - m_new)
        p = jnp.exp(s - m_new)
        l_sc[...] = alpha * l_sc[...] + jnp.sum(p, axis=-1, keepdims=True)
        a_sc[...] = alpha * a_sc[...] + jnp.dot(p.astype(BF16), v.astype(BF16), preferred_element_type=F32)
        m_sc[...] = m_new

    @pl.when(c == 0)
    def _():
        if has_carry:
            m_sc[...] = m_in[...]
            l_sc[...] = l_in[...]
            a_sc[...] = a_in[...]
        else:
            m_sc[...] = jnp.full(m_sc.shape, neg, F32)
            l_sc[...] = jnp.zeros(l_sc.shape, F32)
            a_sc[...] = jnp.zeros(a_sc.shape, F32)
        qr = lax.broadcasted_iota(jnp.int32, (nq, nq), 0)
        kr = lax.broadcasted_iota(jnp.int32, (nq, nq), 1)
        dt = (qr >> hshift) - (kr >> hshift)
        ok = ((qr & (nheads - 1)) == (kr & (nheads - 1))) & (dt >= 0) & ((dt & (dil - 1)) == 0)
        attend(kn_ref[...], vn_ref[...], ok)

    qr = lax.broadcasted_iota(jnp.int32, (nq, wc * nheads), 0)
    kr = lax.broadcasted_iota(jnp.int32, (nq, wc * nheads), 1)
    dist = w_total + (qr >> hshift) - (c * wc + (kr >> hshift))
    ok = (((qr & (nheads - 1)) == (kr & (nheads - 1))) & ((dist & (dil - 1)) == 0)
          & (dist <= dil * N_BACK))
    attend(kb_ref[...].reshape(wc * nheads, HEAD_DIM), vb_ref[...].reshape(wc * nheads, HEAD_DIM), ok)

    @pl.when(c == nc - 1)
    def _():
        if final:
            o_ref[...] = a_sc[...] / l_sc[...]
        else:
            m_out[...] = m_sc[...]
            l_out[...] = l_sc[...]
            a_out[...] = a_sc[...]


def _attn_cached(qf, knf, vnf, buf, g, carry, *, final, wc):
    dil = DILATIONS[g]
    b, nq, hd = qf.shape
    w, nheads = buf.shape[1], buf.shape[3]
    assert w % wc == 0 and nheads & (nheads - 1) == 0 and dil & (dil - 1) == 0
    row = pl.BlockSpec((None, nq, hd), lambda bi, c: (bi, 0, 0))
    stat = pl.BlockSpec((None, nq, 1), lambda bi, c: (bi, 0, 0))
    in_specs = [row, row, row,
                pl.BlockSpec((None, wc, None, nheads, hd), lambda bi, c: (bi, c, 0, 0, 0)),
                pl.BlockSpec((None, wc, None, nheads, hd), lambda bi, c: (bi, c, 1, 0, 0))]
    args = [qf, knf, vnf, buf, buf]
    if carry is not None:
        in_specs += [stat, stat, row]
        args += list(carry)
    if final:
        out_shape = jax.ShapeDtypeStruct((b, nq, hd), F32)
        out_specs = row
    else:
        out_shape = (jax.ShapeDtypeStruct((b, nq, 1), F32), jax.ShapeDtypeStruct((b, nq, 1), F32),
                     jax.ShapeDtypeStruct((b, nq, hd), F32))
        out_specs = (stat, stat, row)
    est = 4 * _nbytes((wc, nheads, hd), F32) + 4 * _nbytes((nq, wc * nheads), F32) + 16 * _nbytes((nq, hd), F32)
    return pl.pallas_call(
        functools.partial(_attn_cached_kernel, dil=dil, nheads=nheads, has_carry=carry is not None, final=final),
        out_shape=out_shape,
        grid=(b, w // wc),
        in_specs=in_specs,
        out_specs=out_specs,
        scratch_shapes=[pltpu.VMEM((nq, 1), F32), pltpu.VMEM((nq, 1), F32), pltpu.VMEM((nq, hd), F32)],
        compiler_params=_params(est, 2),
        name=f"attn_cached_g{g}",
    )(*args)


def _shift_kernel(cur_ref, nxt_ref, new_ref, o_ref, *, t_new):
    last = pl.program_id(1) == pl.num_programs(1) - 1
    rb = cur_ref.shape[0]
    o_ref[0:rb - t_new] = cur_ref[t_new:rb]
    o_ref[rb - t_new:rb] = jnp.where(last, new_ref[...], nxt_ref[...])


def _shift_append(buf, new, *, rb):
    b, w, r, lanes = buf.shape
    t_new = new.shape[1]
    assert w % rb == 0 and rb % t_new == 0 and rb > t_new
    last_blk = w // t_new - 1
    est = 4 * _nbytes((rb, r, lanes), F32) + 6 * _nbytes((t_new, r, lanes), F32)
    return pl.pallas_call(
        functools.partial(_shift_kernel, t_new=t_new),
        out_shape=jax.ShapeDtypeStruct(buf.shape, F32),
        grid=(b, w // rb),
        in_specs=[
            pl.BlockSpec((None, rb, r, lanes), lambda bi, j: (bi, j, 0, 0)),
            pl.BlockSpec((None, t_new, r, lanes),
                         lambda bi, j: (bi, jnp.minimum((j + 1) * (rb // t_new), last_blk), 0, 0)),
            pl.BlockSpec((None, t_new, r, lanes), lambda bi, j: (bi, 0, 0, 0)),
        ],
        out_specs=pl.BlockSpec((None, rb, r, lanes), lambda bi, j: (bi, j, 0, 0)),
        compiler_params=_params(est, 2),
        name="kv_shift_append",
    )(buf, buf, new)


def _hgrn_kernel(*refs, heads, chunk, layer, has_state):
    if has_state:
        zq_ref, zf_ref, zv_ref, zg_ref, lb_ref, ng_ref, s0_ref = refs[:7]
        y_ref, st_out_ref, st_ref = refs[7:]
    else:
        zq_ref, zf_ref, zv_ref, zg_ref, lb_ref, ng_ref = refs[:6]
        s0_ref = None
        y_ref, st_out_ref, st_ref = refs[6:]
    tstep = pl.program_id(2)
    nsteps = pl.num_programs(2)
    tb = zq_ref.shape[0]

    @pl.when(tstep == 0)
    def _():
        for h in range(heads):
            if has_state:
                st_ref[h] = s0_ref[h].T
            else:
                st_ref[h] = jnp.zeros((HGRN_DK, HGRN_DK), F32)

    logits = lb_ref[...]
    ex = jnp.exp(logits - jnp.max(logits, axis=0, keepdims=True))
    sm = ex / jnp.sum(ex, axis=0, keepdims=True)
    lb = jnp.zeros_like(sm[0:1])
    for l in range(1, layer + 1):
        lb = lb + sm[l:l + 1]
    ng = ng_ref[...]

    sub = V7X_SUBLANES
    rblk = sub if chunk % sub == 0 else chunk
    rowi = lax.broadcasted_iota(jnp.int32, (rblk, HGRN_DK), 0)
    rowi_all = lax.broadcasted_iota(jnp.int32, (chunk, zq_ref.shape[1]), 0)
    nt = (((1,), (1,)), ((), ()))
    tn = (((0,), (0,)), ((), ()))

    def body(ci, carry):
        rows = pl.ds(pl.multiple_of(ci * chunk, chunk), chunk)
        zq = zq_ref[rows, :]
        zf = zf_ref[rows, :]
        zv = zv_ref[rows, :]
        zg = zg_ref[rows, :]
        q_all = _silu(zq)
        f_all = lb + (1.0 - lb) * jax.nn.sigmoid(zf)
        lf_all = jnp.log(f_all)
        k_all = 1.0 - f_all
        og_all = ng * _silu(zg)
        if chunk % sub == 0:
            a_all = lf_all
            shift = 1
            while shift < chunk:
                a_all = a_all + jnp.where(rowi_all >= shift, pltpu.roll(a_all, shift, 0), 0.0)
                shift *= 2
        else:
            a_all = jnp.zeros_like(lf_all)
            for s in range(chunk):
                a_all = a_all + jnp.where(rowi_all >= s, lf_all[s:s + 1, :], 0.0)
        a_all = a_all * LOG2_E
        qe_all = q_all * jnp.exp2(a_all)
        a_last_all = a_all[chunk - 1:chunk, :]
        kd_all = k_all * jnp.exp2(a_last_all - a_all)
        dec_all = jnp.exp2(a_last_all)
        for h in range(heads):
            lanes = slice(h * HGRN_DK, (h + 1) * HGRN_DK)
            q, k, v, a = q_all[:, lanes], k_all[:, lanes], zv[:, lanes], a_all[:, lanes]
            st = st_ref[h]
            o = lax.dot_general(qe_all[:, lanes].astype(BF16), st.astype(BF16), nt,
                                preferred_element_type=F32)
            ob = [o[r0:r0 + rblk] for r0 in range(0, chunk, rblk)]
            for s in range(chunk):
                a_s, k_s, v_s = a[s:s + 1, :], k[s:s + 1, :], v[s:s + 1, :]
                for bi in range(s // rblk, len(ob)):
                    r0 = bi * rblk
                    e = jnp.exp2(a[r0:r0 + rblk] - a_s)
                    if bi == s // rblk:
                        e = jnp.where(rowi >= s - r0, e, 0.0)
                    wgt = (q[r0:r0 + rblk] * k_s) * e
                    ob[bi] = ob[bi] + jnp.sum(wgt, axis=-1, keepdims=True) * v_s
            o = ob[0] if len(ob) == 1 else jnp.concatenate(ob, axis=0)
            upd = lax.dot_general(v.astype(BF16), kd_all[:, lanes].astype(BF16), tn,
                                  preferred_element_type=F32)
            st_ref[h] = st * dec_all[:, lanes] + upd
            on = o * lax.rsqrt(jnp.mean(o * o, axis=-1, keepdims=True) + EPS)
            y_ref[rows, lanes] = (on * og_all[:, lanes]).astype(y_ref.dtype)
        return carry

    lax.fori_loop(0, tb // chunk, body, 0)

    @pl.when(tstep == nsteps - 1)
    def _():
        for h in range(heads):
            st_out_ref[h] = st_ref[h].T


def _hgrn(z, lb_logits, norm_g, s0, *, layer, heads, tb, out_dtype):
    b, t, d4 = z.shape
    d = d4 // 4
    nh = d // HGRN_DK
    nhg = nh // heads
    hb = heads * HGRN_DK
    chunk = math.gcd(t, HGRN_CHUNK)
    depth = lb_logits.shape[0]
    assert t % tb == 0 and tb % chunk == 0 and nh % heads == 0

    def zspec(part):
        return pl.BlockSpec((None, tb, hb), lambda bi, hg, i: (bi, i, part * nhg + hg))

    in_specs = [zspec(0), zspec(1), zspec(2), zspec(3),
                pl.BlockSpec((depth, hb), lambda bi, hg, i: (0, hg)),
                pl.BlockSpec((1, hb), lambda bi, hg, i: (0, hg))]
    args = [z, z, z, z, lb_logits, norm_g.reshape(1, d)]
    sspec = pl.BlockSpec((None, heads, HGRN_DK, HGRN_DK), lambda bi, hg, i: (bi, hg, 0, 0))
    if s0 is not None:
        in_specs.append(sspec)
        args.append(s0)
    est = (2 * (4 * _nbytes((tb, hb), F32) + _nbytes((tb, hb), out_dtype))
           + 5 * _nbytes((heads, HGRN_DK, HGRN_DK), F32))
    return pl.pallas_call(
        functools.partial(_hgrn_kernel, heads=heads, chunk=chunk, layer=layer, has_state=s0 is not None),
        out_shape=(jax.ShapeDtypeStruct((b, t, d), out_dtype),
                   jax.ShapeDtypeStruct((b, nh, HGRN_DK, HGRN_DK), F32)),
        grid=(b, nhg, t // tb),
        in_specs=in_specs,
        out_specs=(pl.BlockSpec((None, tb, hb), lambda bi, hg, i: (bi, i, hg)), sspec),
        scratch_shapes=[pltpu.VMEM((heads, HGRN_DK, HGRN_DK), F32)],
        compiler_params=_params(est, 3),
        name="hgrn2",
    )(*args)


def _trunk(x, mods, a_bufs, b_states, weights, *, nb, t_seq):
    (norm_mix_g, norm_ffn_g, a_w_qkv, a_w_o, b_w_in, b_lb_logits, b_norm_g, b_w_o,
     ffn_w_in, ffn_w_out, ffn_w_out_bf16, final_g) = weights
    bn_, t, d = x.shape
    cached = a_bufs is not None
    depth = norm_mix_g.shape[0]
    d_ff = ffn_w_out.shape[1]
    ngroups = len(DILATIONS)
    hw = a_w_o.shape[1]
    nheads = hw // HEAD_DIM
    bm = _pick(t, (1024, 512, 256, 128, 64, 32, 16, 8))
    bm_e = _pick(t, (256, 128, 64, 32, 16, 8))
    bn = 512
    ident = lambda j: j
    kv_out, states = [], []

    for layer in range(depth):
        i = layer // N_MIXERS
        h = _norm_mod(x, norm_mix_g[layer], mods[layer], 1, 0, bm=bm_e)
        if layer % N_MIXERS == 0:
            q = _linear(h, a_w_qkv, layer=i, n=ngroups * hw, col_map=ident, out_dtype=F32,
                        bm=bm, bn=bn, name="a_q")
            half = hw // bn
            kvs = []
            carry = None
            if cached:
                q5 = q.reshape(nb, t_seq, ngroups, nheads, HEAD_DIM)
            for g in range(ngroups):
                def kv_cols(j, g=g):
                    return jnp.where(j < half, (ngroups + g) * half + j, (2 * ngroups + g) * half + (j - half))
                kv = _linear(h, a_w_qkv, layer=i, n=2 * hw, col_map=kv_cols, out_dtype=F32,
                             bm=bm, bn=bn, name=f"a_kv{g}")
                if not cached:
                    kvs.append(kv)
                    keep = min(DILATIONS[g] * N_BACK, t)
                    kv_keep = kv if keep == t else kv[:, t - keep:]
                    kv_out.append(kv_keep.reshape(1, bn_, keep, 2, nheads, HEAD_DIM))
                else:
                    buf = a_bufs[g][i]
                    w = buf.shape[1]
                    kvn = kv.reshape(nb, t_seq, 2, nheads, HEAD_DIM)
                    flat = lambda a: a.reshape(nb, t_seq * nheads, HEAD_DIM)
                    carry = _attn_cached(flat(q5[:, :, g]), flat(kvn[:, :, 0]), flat(kvn[:, :, 1]), buf, g,
                                         carry, final=g == ngroups - 1, wc=min(w, 256))
                    newbuf = _shift_append(buf.reshape(nb, w, 2 * nheads, HEAD_DIM),
                                           kvn.reshape(nb, t_seq, 2 * nheads, HEAD_DIM), rb=min(w, 256))
                    kv_out.append(newbuf.reshape(1, nb, w, 2, nheads, HEAD_DIM))
            if cached:
                merged = carry.reshape(1, nb * t_seq, hw).astype(BF16)
            else:
                merged = _attn_prompt(q, kvs)
            x = _linear(merged, a_w_o, layer=i, n=d, col_map=ident, out_dtype=F32, bm=bm, bn=bn,
                        mode="resid", res=x, gate=mods[layer], gate_chunk=2, name="a_o")
        else:
            z = _linear(h, b_w_in, layer=i, n=4 * d, col_map=ident, out_dtype=F32, bm=bm, bn=bn, name="b_in")
            hg_heads = _pick(d // HGRN_DK, (8, 4, 2, 1))
            if not cached:
                y, s_t = _hgrn(z, b_lb_logits, b_norm_g[i], None, layer=layer, heads=hg_heads,
                               tb=min(t, 512), out_dtype=BF16)
            else:
                yb, s_t = _hgrn(z.reshape(nb, t_seq, 4 * d), b_lb_logits, b_norm_g[i], b_states[i],
                                layer=layer, heads=hg_heads, tb=t_seq, out_dtype=F32)
                y = yb.reshape(1, nb * t_seq, d).astype(BF16)
            states.append(s_t)
            x = _linear(y, b_w_o, layer=i, n=d, col_map=ident, out_dtype=F32, bm=bm, bn=bn,
                        mode="resid", res=x, gate=mods[layer], gate_chunk=2, name="b_o")
        h = _norm_mod(x, norm_ffn_g[layer], mods[layer], 4, 3, bm=bm_e)
        bnf = 256
        nff = d_ff // bnf
        act = _linear(h, ffn_w_in, layer=layer, n=d_ff, col_map=ident, col_map2=lambda j: nff + j,
                      out_dtype=BF16, bm=bm, bn=bnf, mode="swiglu", name="ffn_in")
        half_ff = d_ff // 2
        bk = half_ff if (d_ff % 2 == 0 and half_ff % V7X_LANES == 0) else d_ff
        x = _linear(act, ffn_w_out_bf16, layer=layer, n=d, col_map=ident, out_dtype=F32,
                    bm=bm, bn=bn, bk=bk, mode="resid", res=x, gate=mods[layer], gate_chunk=5,
                    name="ffn_out")
    y = _final_norm(x, final_g, bm=bm_e)
    return y, kv_out, states


def kernel(x_prompt, x_sample, state_a_kv_w128, state_a_kv_w512, state_a_kv_w2048, state_b_rec,
           c_prompt, c_sample, ada_w, ada_b, norm_mix_g, norm_ffn_g, a_w_qkv, a_w_o,
           b_w_in, b_lb_logits, b_norm_g, b_w_o, ffn_w_in, ffn_w_out, final_g):
    bp, sp, d = x_prompt.shape
    bs, ts, _ = x_sample.shape
    depth = ada_w.shape[0]
    d_ff = ffn_w_out.shape[1]
    ffn_w_out_bf16 = _to_bf16(ffn_w_out, kb=_pick(d_ff, (d_ff // 16, d_ff // 8, d_ff // 4, d_ff // 2)))
    weights = (norm_mix_g, norm_ffn_g, a_w_qkv, a_w_o, b_w_in, b_lb_logits, b_norm_g, b_w_o,
               ffn_w_in, ffn_w_out, ffn_w_out_bf16, final_g)

    rows = bp + bs
    rpad = -rows % V7X_SUBLANES
    c_all = jnp.pad(jnp.concatenate([c_prompt, c_sample], axis=0), ((0, rpad), (0, 0)))
    mod = _ada_modulation(c_all, ada_w, ada_b, bn=512)
    mods_p = [mod[l, :bp].reshape(bp, 1, 6 * d) for l in range(depth)]
    mods_s = [jnp.repeat(mod[l, bp:rows], ts, axis=0).reshape(1, bs * ts, 6 * d) for l in range(depth)]

    y_p, kv_p, st_p = _trunk(x_prompt, mods_p, None, None, weights, nb=bp, t_seq=sp)
    y_s, kv_s, st_s = _trunk(x_sample.reshape(1, bs * ts, d), mods_s,
                             (state_a_kv_w128, state_a_kv_w512, state_a_kv_w2048), state_b_rec,
                             weights, nb=bs, t_seq=ts)
    return (y_p, y_s.reshape(bs, ts, d), kv_p[0], kv_s[0], kv_p[1], kv_s[1], kv_p[2], kv_s[2],
            jnp.stack(st_p, axis=0), jnp.stack(st_s, axis=0))
```

```python
import functools
import math

import jax
import jax.numpy as jnp
from jax import lax
from jax.experimental import pallas as pl
from jax.experimental.pallas import tpu as pltpu

F32 = jnp.float32
BF16 = jnp.bfloat16

EPS = 1e-6
DILATIONS = (1, 4, 16)
N_BACK = 128
HEAD_DIM = 128
HGRN_DK = 128
HGRN_CHUNK = 16
LOG2_E = 1.4426950408889634
N_MIXERS = 2
V7X_LANES = 128
V7X_SUBLANES = 8
V7X_SCOPED_VMEM_BYTES = 60000 * 1024


def _params(nbytes, n_axes):
    limit = int(min(V7X_SCOPED_VMEM_BYTES, max(2 * nbytes, 16 * 1024 * 1024)))
    return pltpu.CompilerParams(dimension_semantics=("arbitrary",) * n_axes, vmem_limit_bytes=limit)


def _nbytes(shape, dtype):
    return math.prod(shape) * jnp.dtype(dtype).itemsize


def _silu(x):
    return x * jax.nn.sigmoid(x)


def _pick(total, prefs):
    for p in prefs:
        if total % p == 0:
            return p
    return total


def _ada_kernel(c_ref, w_ref, b_ref, o_ref):
    s = _silu(c_ref[...])
    acc = jnp.dot(s.astype(BF16), w_ref[...].astype(BF16), preferred_element_type=F32)
    o_ref[...] = acc + b_ref[...]


def _ada_modulation(c_all, ada_w, ada_b, *, bn):
    depth, d, n = ada_w.shape
    r = c_all.shape[0]
    est = 2 * (_nbytes((d, bn), F32) + _nbytes((r, bn), F32)) + _nbytes((d, bn), BF16) + _nbytes((r, d), F32)
    return pl.pallas_call(
        _ada_kernel,
        out_shape=jax.ShapeDtypeStruct((depth, r, n), F32),
        grid=(depth, n // bn),
        in_specs=[
            pl.BlockSpec((r, d), lambda l, j: (0, 0)),
            pl.BlockSpec((None, d, bn), lambda l, j: (l, 0, j)),
            pl.BlockSpec((None, 1, bn), lambda l, j: (l, 0, j)),
        ],
        out_specs=pl.BlockSpec((None, r, bn), lambda l, j: (l, 0, j)),
        compiler_params=_params(est, 2),
        name="ada_modulation",
    )(c_all, ada_w, ada_b.reshape(depth, 1, n))


def _norm_mod_kernel(x_ref, g_ref, sc_ref, sh_ref, o_ref):
    x = x_ref[...]
    y = x * lax.rsqrt(jnp.mean(x * x, axis=-1, keepdims=True) + EPS)
    h = (y * g_ref[...]) * (1.0 + sc_ref[...]) + sh_ref[...]
    o_ref[...] = h.astype(o_ref.dtype)


def _norm_mod(x, g, mod, sc_chunk, sh_chunk, *, bm):
    bn_, t, d = x.shape
    tg = mod.shape[1]
    mg = 1 if tg == 1 else bm

    def mod_map(chunk):
        return lambda b, i: (b, 0 if tg == 1 else i, chunk)

    est = 2 * (_nbytes((bm, d), F32) + _nbytes((bm, d), BF16)) + 2 * _nbytes((bm, d), F32)
    return pl.pallas_call(
        _norm_mod_kernel,
        out_shape=jax.ShapeDtypeStruct((bn_, t, d), BF16),
        grid=(bn_, t // bm),
        in_specs=[
            pl.BlockSpec((None, bm, d), lambda b, i: (b, i, 0)),
            pl.BlockSpec((1, d), lambda b, i: (0, 0)),
            pl.BlockSpec((None, mg, d), mod_map(sc_chunk)),
            pl.BlockSpec((None, mg, d), mod_map(sh_chunk)),
        ],
        out_specs=pl.BlockSpec((None, bm, d), lambda b, i: (b, i, 0)),
        compiler_params=_params(est, 2),
        name="norm_mod",
    )(x, g.reshape(1, d), mod, mod)


def _final_norm_kernel(x_ref, g_ref, o_ref):
    x = x_ref[...]
    y = x * lax.rsqrt(jnp.mean(x * x, axis=-1, keepdims=True) + EPS)
    o_ref[...] = y * g_ref[...]


def _final_norm(x, g, *, bm):
    bn_, t, d = x.shape
    est = 6 * _nbytes((bm, d), F32)
    return pl.pallas_call(
        _final_norm_kernel,
        out_shape=jax.ShapeDtypeStruct((bn_, t, d), F32),
        grid=(bn_, t // bm),
        in_specs=[
            pl.BlockSpec((None, bm, d), lambda b, i: (b, i, 0)),
            pl.BlockSpec((1, d), lambda b, i: (0, 0)),
        ],
        out_specs=pl.BlockSpec((None, bm, d), lambda b, i: (b, i, 0)),
        compiler_params=_params(est, 2),
        name="final_norm",
    )(x, g.reshape(1, d))


def _to_bf16_kernel(w_ref, o_ref):
    o_ref[...] = w_ref[...].astype(BF16)


def _to_bf16(w, *, kb):
    l, k, n = w.shape
    assert k % kb == 0
    est = 2 * (_nbytes((kb, n), F32) + _nbytes((kb, n), BF16))
    spec = pl.BlockSpec((None, kb, n), lambda li, i: (li, i, 0))
    return pl.pallas_call(
        _to_bf16_kernel,
        out_shape=jax.ShapeDtypeStruct(w.shape, BF16),
        grid=(l, k // kb),
        in_specs=[spec],
        out_specs=spec,
        compiler_params=_params(est, 2),
        name="weight_to_bf16",
    )(w)


def _linear_kernel(*refs, nk, mode, extra):
    refs = list(refs)
    x_ref, w_ref = refs[0], refs[1]
    pos = 2
    w2_ref = res_ref = gate_ref = xs_ref = ress_ref = gates_ref = os_ref = None
    if mode == "swiglu":
        w2_ref = refs[pos]
        pos += 1
    if mode == "resid":
        res_ref, gate_ref = refs[pos], refs[pos + 1]
        pos += 2
    if extra:
        xs_ref = refs[pos]
        pos += 1
        if mode == "resid":
            ress_ref, gates_ref = refs[pos], refs[pos + 1]
            pos += 2
    o_ref = refs[pos]
    pos += 1
    if extra:
        os_ref = refs[pos]
        pos += 1
    scratch = refs[pos:]

    def finalize(out_ref, r_ref, g_ref, acc, acc2=None):
        if mode == "swiglu":
            out_ref[...] = (_silu(acc) * acc2).astype(out_ref.dtype)
        elif mode == "resid":
            out_ref[...] = (r_ref[...] + g_ref[...] * acc).astype(out_ref.dtype)
        else:
            out_ref[...] = acc.astype(out_ref.dtype)

    cast = w_ref.dtype != BF16
    first = jnp.logical_and(pl.program_id(1) == 0, pl.program_id(2) == 0)
    if nk == 1:
        wb_ref, wb2_ref = w_ref, w2_ref
        if cast:
            wb_ref = scratch[0]
            wb2_ref = scratch[1] if mode == "swiglu" else None

        def first_step():
            if cast:
                wb_ref[...] = w_ref[...].astype(BF16)
                if mode == "swiglu":
                    wb2_ref[...] = w2_ref[...].astype(BF16)
            if extra:
                xs = xs_ref[...]
                accs = jnp.dot(xs, wb_ref[...], preferred_element_type=F32)
                accs2 = None
                if mode == "swiglu":
                    accs2 = jnp.dot(xs, wb2_ref[...], preferred_element_type=F32)
                finalize(os_ref, ress_ref, gates_ref, accs, accs2)

        if cast or extra:
            pl.when(first)(first_step)

        x = x_ref[...]
        acc = jnp.dot(x, wb_ref[...], preferred_element_type=F32)
        acc2 = None
        if mode == "swiglu":
            acc2 = jnp.dot(x, wb2_ref[...], preferred_element_type=F32)
        finalize(o_ref, res_ref, gate_ref, acc, acc2)
    else:
        acc_ref = scratch[0]
        k = pl.program_id(3)
        w = w_ref[...].astype(BF16) if cast else w_ref[...]
        part = jnp.dot(x_ref[...], w, preferred_element_type=F32)

        @pl.when(k == 0)
        def _():
            acc_ref[...] = part

        @pl.when(k > 0)
        def _():
            acc_ref[...] += part

        @pl.when(k == nk - 1)
        def _():
            finalize(o_ref, res_ref, gate_ref, acc_ref[...])

        if extra:
            accs_ref = scratch[1]

            @pl.when(first)
            def _():
                parts = jnp.dot(xs_ref[...], w, preferred_element_type=F32)

                @pl.when(k == 0)
                def _():
                    accs_ref[...] = parts

                @pl.when(k > 0)
                def _():
                    accs_ref[...] += parts

                @pl.when(k == nk - 1)
                def _():
                    finalize(os_ref, ress_ref, gates_ref, accs_ref[...])


def _linear(x, w, *, layer, n, col_map, out_dtype, bm, bn, bk=None, mode="plain",
            col_map2=None, res=None, gate=None, gate_chunk=0, xs=None, res_s=None, gate_s=None,
            name="linear"):
    bn_, t, kdim = x.shape
    bk = kdim if bk is None else bk
    nk = kdim // bk
    assert kdim % bk == 0 and t % bm == 0 and n % bn == 0
    assert mode != "swiglu" or nk == 1
    grid = (n // bn, bn_, t // bm, nk)

    in_specs = [
        pl.BlockSpec((None, bm, bk), lambda j, b, i, k: (b, i, k)),
        pl.BlockSpec((None, bk, bn), lambda j, b, i, k: (layer, k, col_map(j))),
    ]
    args = [x, w]
    est = 2 * (_nbytes((bm, bk), BF16) + _nbytes((bk, bn), w.dtype) + _nbytes((bm, bn), out_dtype))
    est += _nbytes((bk, bn), BF16) + 2 * _nbytes((bm, bn), F32)
    if mode == "swiglu":
        in_specs.append(pl.BlockSpec((None, bk, bn), lambda j, b, i, k: (layer, k, col_map2(j))))
        args.append(w)
        est += 2 * _nbytes((bk, bn), F32) + _nbytes((bk, bn), BF16) + _nbytes((bm, bn), F32)
    if mode == "resid":
        tg = gate.shape[1]
        mg = 1 if tg == 1 else bm
        nb = n // bn
        in_specs.append(pl.BlockSpec((None, bm, bn), lambda j, b, i, k: (b, i, j)))
        in_specs.append(pl.BlockSpec(
            (None, mg, bn), lambda j, b, i, k: (b, 0 if tg == 1 else i, gate_chunk * nb + j)))
        args += [res, gate]
        est += 4 * _nbytes((bm, bn), F32)

    out_shape = jax.ShapeDtypeStruct((bn_, t, n), out_dtype)
    out_specs = pl.BlockSpec((None, bm, bn), lambda j, b, i, k: (b, i, j))
    ms = 0
    if xs is not None:
        ms = xs.shape[1]
        in_specs.append(pl.BlockSpec((None, ms, bk), lambda j, b, i, k: (0, 0, k)))
        args.append(xs)
        if mode == "resid":
            nb = n // bn
            in_specs.append(pl.BlockSpec((None, ms, bn), lambda j, b, i, k: (0, 0, j)))
            in_specs.append(pl.BlockSpec((None, ms, bn), lambda j, b, i, k: (0, 0, gate_chunk * nb + j)))
            args += [res_s, gate_s]
        out_shape = (out_shape, jax.ShapeDtypeStruct((1, ms, n), out_dtype))
        out_specs = (out_specs, pl.BlockSpec((None, ms, bn), lambda j, b, i, k: (0, 0, j)))
        est += 2 * _nbytes((ms, bk), BF16) + 8 * _nbytes((ms, bn), F32)

    if nk > 1:
        scratch = [pltpu.VMEM((bm, bn), F32)] + ([pltpu.VMEM((ms, bn), F32)] if ms else [])
    elif w.dtype != BF16:
        scratch = [pltpu.VMEM((bk, bn), BF16)] * (2 if mode == "swiglu" else 1)
    else:
        scratch = []

    return pl.pallas_call(
        functools.partial(_linear_kernel, nk=nk, mode=mode, extra=xs is not None),
        out_shape=out_shape,
        grid=grid,
        in_specs=in_specs,
        out_specs=out_specs,
        scratch_shapes=scratch,
        compiler_params=_params(est, 4),
        name=name,
    )(*args)


def _attn_prompt_kernel(*refs, ngroups):
    q_refs, k_refs, v_refs = refs[:ngroups], refs[ngroups:2 * ngroups], refs[2 * ngroups:3 * ngroups]
    o_ref, og_ref, lg_ref = refs[3 * ngroups:]
    seq = o_ref.shape[0]
    scale = HEAD_DIM ** -0.5
    nt = (((1,), (1,)), ((), ()))
    neg = -jnp.inf
    row1 = lax.broadcasted_iota(jnp.int32, (N_BACK, N_BACK), 0)
    col1 = lax.broadcasted_iota(jnp.int32, (N_BACK, N_BACK), 1)
    row2 = lax.broadcasted_iota(jnp.int32, (N_BACK, 2 * N_BACK), 0)
    col2 = lax.broadcasted_iota(jnp.int32, (N_BACK, 2 * N_BACK), 1)
    ok_first = col1 <= row1
    ok_band = jnp.logical_or(jnp.logical_and(col2 < N_BACK, col2 >= row2),
                             jnp.logical_and(col2 >= N_BACK, col2 - N_BACK <= row2))
    ones = jnp.ones((N_BACK, HEAD_DIM), BF16)

    for g in range(ngroups):
        dil = DILATIONS[g]
        length = seq // dil
        nblk = length // N_BACK
        for r in range(dil):
            sub_rows = slice(0, length) if dil == 1 else pl.ds(r, length, stride=dil)
            q = q_refs[g][sub_rows, :].astype(BF16)
            k = k_refs[g][sub_rows, :].astype(BF16)
            v = v_refs[g][sub_rows, :].astype(BF16)
            scores = []
            for i in range(nblk):
                qi = q[i * N_BACK:(i + 1) * N_BACK]
                keys = k[max(i - 1, 0) * N_BACK:(i + 1) * N_BACK]
                scores.append(lax.dot_general(qi, keys, nt, preferred_element_type=F32) * scale)
            probs, maxes = [], []
            for i in range(nblk):
                s = jnp.where(ok_first if i == 0 else ok_band, scores[i], neg)
                m = jnp.max(s, axis=-1, keepdims=True)
                probs.append(jnp.exp(s - m).astype(BF16))
                maxes.append(m)
            for i in range(nblk):
                vals = v[max(i - 1, 0) * N_BACK:(i + 1) * N_BACK]
                vext = jnp.concatenate([vals, jnp.concatenate([ones] * (vals.shape[0] // N_BACK), axis=0)], axis=1)
                acc = jnp.dot(probs[i], vext, preferred_element_type=F32)
                den = acc[:, HEAD_DIM:]
                start = r + i * N_BACK * dil
                out_rows = slice(start, start + N_BACK) if dil == 1 else pl.ds(start, N_BACK, stride=dil)
                og_ref.at[g][out_rows, :] = acc[:, :HEAD_DIM] / den
                lg_ref.at[g][out_rows, :] = maxes[i] + jnp.log(den)

    rb = 256
    for c in range(seq // rb):
        rows = slice(c * rb, (c + 1) * rb)
        ls = [lg_ref[g, rows, :] for g in range(ngroups)]
        m = functools.reduce(jnp.maximum, ls)
        es = [jnp.exp(l - m) for l in ls]
        tot = functools.reduce(lambda a, b: a + b, es)
        num = functools.reduce(lambda a, b: a + b, [es[g] * og_ref[g, rows, :] for g in range(ngroups)])
        o_ref[rows, :] = (num / tot).astype(o_ref.dtype)


def _attn_prompt(q, kvs):
    ngroups = len(DILATIONS)
    b, s, qc = q.shape
    hw = qc // ngroups
    nh = hw // HEAD_DIM
    assert all(s % (d * N_BACK) == 0 for d in DILATIONS) and s % 256 == 0
    blk = (None, s, HEAD_DIM)
    in_specs = ([pl.BlockSpec(blk, (lambda bi, h, g=g: (bi, 0, g * nh + h))) for g in range(ngroups)]
                + [pl.BlockSpec(blk, lambda bi, h: (bi, 0, h))] * ngroups
                + [pl.BlockSpec(blk, lambda bi, h: (bi, 0, nh + h))] * ngroups)
    est = (2 * (3 * ngroups * _nbytes((s, HEAD_DIM), F32) + _nbytes((s, HEAD_DIM), BF16))
           + 2 * ngroups * _nbytes((s, HEAD_DIM), F32) + 6 * _nbytes((s, HEAD_DIM), F32))
    return pl.pallas_call(
        functools.partial(_attn_prompt_kernel, ngroups=ngroups),
        out_shape=jax.ShapeDtypeStruct((b, s, hw), BF16),
        grid=(b, nh),
        in_specs=in_specs,
        out_specs=pl.BlockSpec(blk, lambda bi, h: (bi, 0, h)),
        scratch_shapes=[pltpu.VMEM((ngroups, s, HEAD_DIM), F32), pltpu.VMEM((ngroups, s, HEAD_DIM), F32)],
        compiler_params=_params(est, 2),
        name="attn_prompt",
    )(*([q] * ngroups), *kvs, *kvs)


def _attn_cached_kernel(*refs, dil, nheads, has_carry, final):
    if has_carry:
        q_ref, kn_ref, vn_ref, kb_ref, vb_ref, m_in, l_in, a_in = refs[:8]
        outs = refs[8:]
    else:
        q_ref, kn_ref, vn_ref, kb_ref, vb_ref = refs[:5]
        outs = refs[5:]
    if final:
        o_ref, m_sc, l_sc, a_sc = outs
    else:
        m_out, l_out, a_out, m_sc, l_sc, a_sc = outs
    c = pl.program_id(1)
    nc = pl.num_programs(1)
    wc = kb_ref.shape[0]
    w_total = wc * nc
    nq = q_ref.shape[0]
    hshift = nheads.bit_length() - 1
    scale = HEAD_DIM ** -0.5
    nt = (((1,), (1,)), ((), ()))
    neg = -jnp.inf
    q = q_ref[...].astype(BF16)

    def attend(k, v, ok):
        s = lax.dot_general(q, k.astype(BF16), nt, preferred_element_type=F32) * scale
        s = jnp.where(ok, s, neg)
        m_old = m_sc[...]
        m_new = jnp.maximum(m_old, jnp.max(s, axis=-1, keepdims=True))
        alpha = jnp.exp(m_old - m_new)
        p = jnp.exp(s - m_new)
        l_sc[...] = alpha * l_sc[...] + jnp.sum(p, axis=-1, keepdims=True)
        a_sc[...] = alpha * a_sc[...] + jnp.dot(p.astype(BF16), v.astype(BF16), preferred_element_type=F32)
        m_sc[...] = m_new

    @pl.when(c == 0)
    def _():
        if has_carry:
            m_sc[...] = m_in[...]
            l_sc[...] = l_in[...]
            a_sc[...] = a_in[...]
        else:
            m_sc[...] = jnp.full(m_sc.shape, neg, F32)
            l_sc[...] = jnp.zeros(l_sc.shape, F32)
            a_sc[...] = jnp.zeros(a_sc.shape, F32)
        qr = lax.broadcasted_iota(jnp.int32, (nq, nq), 0)
        kr = lax.broadcasted_iota(jnp.int32, (nq, nq), 1)
        dt = (qr >> hshift) - (kr >> hshift)
        ok = ((qr & (nheads - 1)) == (kr & (nheads - 1))) & (dt >= 0) & ((dt & (dil - 1)) == 0)
        attend(kn_ref[...], vn_ref[...], ok)

    qr = lax.broadcasted_iota(jnp.int32, (nq, wc * nheads), 0)
    kr = lax.broadcasted_iota(jnp.int32, (nq, wc * nheads), 1)
    dist = w_total + (qr >> hshift) - (c * wc + (kr >> hshift))
    ok = (((qr & (nheads - 1)) == (kr & (nheads - 1))) & ((dist & (dil - 1)) == 0)
          & (dist <= dil * N_BACK))
    attend(kb_ref[...].reshape(wc * nheads, HEAD_DIM), vb_ref[...].reshape(wc * nheads, HEAD_DIM), ok)

    @pl.when(c == nc - 1)
    def _():
        if final:
            o_ref[...] = a_sc[...] / l_sc[...]
        else:
            m_out[...] = m_sc[...]
            l_out[...] = l_sc[...]
            a_out[...] = a_sc[...]


def _attn_cached(qf, knf, vnf, buf, g, carry, *, final, wc):
    dil = DILATIONS[g]
    b, nq, hd = qf.shape
    w, nheads = buf.shape[1], buf.shape[3]
    assert w % wc == 0 and nheads & (nheads - 1) == 0 and dil & (dil - 1) == 0
    row = pl.BlockSpec((None, nq, hd), lambda bi, c: (bi, 0, 0))
    stat = pl.BlockSpec((None, nq, 1), lambda bi, c: (bi, 0, 0))
    in_specs = [row, row, row,
                pl.BlockSpec((None, wc, None, nheads, hd), lambda bi, c: (bi, c, 0, 0, 0)),
                pl.BlockSpec((None, wc, None, nheads, hd), lambda bi, c: (bi, c, 1, 0, 0))]
    args = [qf, knf, vnf, buf, buf]
    if carry is not None:
        in_specs += [stat, stat, row]
        args += list(carry)
    if final:
        out_shape = jax.ShapeDtypeStruct((b, nq, hd), F32)
        out_specs = row
    else:
        out_shape = (jax.ShapeDtypeStruct((b, nq, 1), F32), jax.ShapeDtypeStruct((b, nq, 1), F32),
                     jax.ShapeDtypeStruct((b, nq, hd), F32))
        out_specs = (stat, stat, row)
    est = 4 * _nbytes((wc, nheads, hd), F32) + 4 * _nbytes((nq, wc * nheads), F32) + 16 * _nbytes((nq, hd), F32)
    return pl.pallas_call(
        functools.partial(_attn_cached_kernel, dil=dil, nheads=nheads, has_carry=carry is not None, final=final),
        out_shape=out_shape,
        grid=(b, w // wc),
        in_specs=in_specs,
        out_specs=out_specs,
        scratch_shapes=[pltpu.VMEM((nq, 1), F32), pltpu.VMEM((nq, 1), F32), pltpu.VMEM((nq, hd), F32)],
        compiler_params=_params(est, 2),
        name=f"attn_cached_g{g}",
    )(*args)


def _shift_kernel(cur_ref, nxt_ref, new_ref, o_ref, *, t_new):
    last = pl.program_id(1) == pl.num_programs(1) - 1
    rb = cur_ref.shape[0]
    o_ref[0:rb - t_new] = cur_ref[t_new:rb]
    o_ref[rb - t_new:rb] = jnp.where(last, new_ref[...], nxt_ref[...])


def _shift_append(buf, new, *, rb):
    b, w, r, lanes = buf.shape
    t_new = new.shape[1]
    assert w % rb == 0 and rb % t_new == 0 and rb > t_new
    last_blk = w // t_new - 1
    est = 4 * _nbytes((rb, r, lanes), F32) + 6 * _nbytes((t_new, r, lanes), F32)
    return pl.pallas_call(
        functools.partial(_shift_kernel, t_new=t_new),
        out_shape=jax.ShapeDtypeStruct(buf.shape, F32),
        grid=(b, w // rb),
        in_specs=[
            pl.BlockSpec((None, rb, r, lanes), lambda bi, j: (bi, j, 0, 0)),
            pl.BlockSpec((None, t_new, r, lanes),
                         lambda bi, j: (bi, jnp.minimum((j + 1) * (rb // t_new), last_blk), 0, 0)),
            pl.BlockSpec((None, t_new, r, lanes), lambda bi, j: (bi, 0, 0, 0)),
        ],
        out_specs=pl.BlockSpec((None, rb, r, lanes), lambda bi, j: (bi, j, 0, 0)),
        compiler_params=_params(est, 2),
        name="kv_shift_append",
    )(buf, buf, new)


def _hgrn_kernel(*refs, heads, chunk, layer, has_state):
    if has_state:
        zq_ref, zf_ref, zv_ref, zg_ref, lb_ref, ng_ref, s0_ref = refs[:7]
        y_ref, st_out_ref, st_ref = refs[7:]
    else:
        zq_ref, zf_ref, zv_ref, zg_ref, lb_ref, ng_ref = refs[:6]
        s0_ref = None
        y_ref, st_out_ref, st_ref = refs[6:]
    tstep = pl.program_id(2)
    nsteps = pl.num_programs(2)
    tb = zq_ref.shape[0]

    @pl.when(tstep == 0)
    def _():
        for h in range(heads):
            if has_state:
                st_ref[h] = s0_ref[h].T
            else:
                st_ref[h] = jnp.zeros((HGRN_DK, HGRN_DK), F32)

    logits = lb_ref[...]
    ex = jnp.exp(logits - jnp.max(logits, axis=0, keepdims=True))
    sm = ex / jnp.sum(ex, axis=0, keepdims=True)
    lb = jnp.zeros_like(sm[0:1])
    for l in range(1, layer + 1):
        lb = lb + sm[l:l + 1]
    ng = ng_ref[...]

    sub = V7X_SUBLANES
    rblk = sub if chunk % sub == 0 else chunk
    rowi = lax.broadcasted_iota(jnp.int32, (rblk, HGRN_DK), 0)
    rowi_all = lax.broadcasted_iota(jnp.int32, (chunk, zq_ref.shape[1]), 0)
    nt = (((1,), (1,)), ((), ()))
    tn = (((0,), (0,)), ((), ()))

    def body(ci, carry):
        rows = pl.ds(pl.multiple_of(ci * chunk, chunk), chunk)
        zq = zq_ref[rows, :]
        zf = zf_ref[rows, :]
        zv = zv_ref[rows, :]
        zg = zg_ref[rows, :]
        q_all = _silu(zq)
        f_all = lb + (1.0 - lb) * jax.nn.sigmoid(zf)
        lf_all = jnp.log(f_all)
        k_all = 1.0 - f_all
        og_all = ng * _silu(zg)
        if chunk % sub == 0:
            a_all = lf_all
            shift = 1
            while shift < chunk:
                a_all = a_all + jnp.where(rowi_all >= shift, pltpu.roll(a_all, shift, 0), 0.0)
                shift *= 2
        else:
            a_all = jnp.zeros_like(lf_all)
            for s in range(chunk):
                a_all = a_all + jnp.where(rowi_all >= s, lf_all[s:s + 1, :], 0.0)
        a_all = a_all * LOG2_E
        qe_all = q_all * jnp.exp2(a_all)
        a_last_all = a_all[chunk - 1:chunk, :]
        kd_all = k_all * jnp.exp2(a_last_all - a_all)
        dec_all = jnp.exp2(a_last_all)
        for h in range(heads):
            lanes = slice(h * HGRN_DK, (h + 1) * HGRN_DK)
            q, k, v, a = q_all[:, lanes], k_all[:, lanes], zv[:, lanes], a_all[:, lanes]
            st = st_ref[h]
            o = lax.dot_general(qe_all[:, lanes].astype(BF16), st.astype(BF16), nt,
                                preferred_element_type=F32)
            ob = [o[r0:r0 + rblk] for r0 in range(0, chunk, rblk)]
            for s in range(chunk):
                a_s, k_s, v_s = a[s:s + 1, :], k[s:s + 1, :], v[s:s + 1, :]
                for bi in range(s // rblk, len(ob)):
                    r0 = bi * rblk
                    e = jnp.exp2(a[r0:r0 + rblk] - a_s)
                    if bi == s // rblk:
                        e = jnp.where(rowi >= s - r0, e, 0.0)
                    wgt = (q[r0:r0 + rblk] * k_s) * e
                    ob[bi] = ob[bi] + jnp.sum(wgt, axis=-1, keepdims=True) * v_s
            o = ob[0] if len(ob) == 1 else jnp.concatenate(ob, axis=0)
            upd = lax.dot_general(v.astype(BF16), kd_all[:, lanes].astype(BF16), tn,
                                  preferred_element_type=F32)
            st_ref[h] = st * dec_all[:, lanes] + upd
            on = o * lax.rsqrt(jnp.mean(o * o, axis=-1, keepdims=True) + EPS)
            y_ref[rows, lanes] = (on * og_all[:, lanes]).astype(y_ref.dtype)
        return carry

    lax.fori_loop(0, tb // chunk, body, 0)

    @pl.when(tstep == nsteps - 1)
    def _():
        for h in range(heads):
            st_out_ref[h] = st_ref[h].T


def _hgrn(z, lb_logits, norm_g, s0, *, layer, heads, tb, out_dtype):
    b, t, d4 = z.shape
    d = d4 // 4
    nh = d // HGRN_DK
    nhg = nh // heads
    hb = heads * HGRN_DK
    chunk = math.gcd(t, HGRN_CHUNK)
    depth = lb_logits.shape[0]
    assert t % tb == 0 and tb % chunk == 0 and nh % heads == 0

    def zspec(part):
        return pl.BlockSpec((None, tb, hb), lambda bi, hg, i: (bi, i, part * nhg + hg))

    in_specs = [zspec(0), zspec(1), zspec(2), zspec(3),
                pl.BlockSpec((depth, hb), lambda bi, hg, i: (0, hg)),
                pl.BlockSpec((1, hb), lambda bi, hg, i: (0, hg))]
    args = [z, z, z, z, lb_logits, norm_g.reshape(1, d)]
    sspec = pl.BlockSpec((None, heads, HGRN_DK, HGRN_DK), lambda bi, hg, i: (bi, hg, 0, 0))
    if s0 is not None:
        in_specs.append(sspec)
        args.append(s0)
    est = (2 * (4 * _nbytes((tb, hb), F32) + _nbytes((tb, hb), out_dtype))
           + 5 * _nbytes((heads, HGRN_DK, HGRN_DK), F32))
    return pl.pallas_call(
        functools.partial(_hgrn_kernel, heads=heads, chunk=chunk, layer=layer, has_state=s0 is not None),
        out_shape=(jax.ShapeDtypeStruct((b, t, d), out_dtype),
                   jax.ShapeDtypeStruct((b, nh, HGRN_DK, HGRN_DK), F32)),
        grid=(b, nhg, t // tb),
        in_specs=in_specs,
        out_specs=(pl.BlockSpec((None, tb, hb), lambda bi, hg, i: (bi, i, hg)), sspec),
        scratch_shapes=[pltpu.VMEM((heads, HGRN_DK, HGRN_DK), F32)],
        compiler_params=_params(est, 3),
        name="hgrn2",
    )(*args)


def _forward(xp, xs, mods_p, mods_s, a_bufs, b_states, weights, *, nb, t_seq):
    (norm_mix_g, norm_ffn_g, a_w_qkv, a_w_o, b_w_in, b_lb_logits, b_norm_g, b_w_o,
     ffn_w_in, ffn_w_out_bf16, final_g) = weights
    bp, t, d = xp.shape
    ms = xs.shape[1]
    depth = norm_mix_g.shape[0]
    d_ff = ffn_w_out_bf16.shape[1]
    ngroups = len(DILATIONS)
    hw = a_w_o.shape[1]
    nheads = hw // HEAD_DIM
    bm = _pick(t, (1024, 512, 256, 128, 64, 32, 16, 8))
    bm_e = _pick(t, (256, 128, 64, 32, 16, 8))
    bn = 512
    ident = lambda j: j
    kv_p, kv_s, st_p, st_s = [], [], [], []

    for layer in range(depth):
        i = layer // N_MIXERS
        mp, msd = mods_p[layer], mods_s[layer]
        hp = _norm_mod(xp, norm_mix_g[layer], mp, 1, 0, bm=bm_e)
        hs = _norm_mod(xs, norm_mix_g[layer], msd, 1, 0, bm=ms)
        if layer % N_MIXERS == 0:
            qp, qs = _linear(hp, a_w_qkv, layer=i, n=ngroups * hw, col_map=ident, out_dtype=F32,
                             bm=bm, bn=bn, xs=hs, name="a_q")
            half = hw // bn
            q5 = qs.reshape(nb, t_seq, ngroups, nheads, HEAD_DIM)
            flat = lambda a: a.reshape(nb, t_seq * nheads, HEAD_DIM)
            kvs = []
            carry = None
            for g in range(ngroups):
                def kv_cols(j, g=g):
                    return jnp.where(j < half, (ngroups + g) * half + j, (2 * ngroups + g) * half + (j - half))
                kvp, kvn = _linear(hp, a_w_qkv, layer=i, n=2 * hw, col_map=kv_cols, out_dtype=F32,
                                   bm=bm, bn=bn, xs=hs, name=f"a_kv{g}")
                kvs.append(kvp)
                keep = min(DILATIONS[g] * N_BACK, t)
                kv_keep = kvp if keep == t else kvp[:, t - keep:]
                kv_p.append(kv_keep.reshape(1, bp, keep, 2, nheads, HEAD_DIM))
                buf = a_bufs[g][i]
                w = buf.shape[1]
                kvn = kvn.reshape(nb, t_seq, 2, nheads, HEAD_DIM)
                carry = _attn_cached(flat(q5[:, :, g]), flat(kvn[:, :, 0]), flat(kvn[:, :, 1]), buf, g,
                                     carry, final=g == ngroups - 1, wc=min(w, 256))
                newbuf = _shift_append(buf.reshape(nb, w, 2 * nheads, HEAD_DIM),
                                       kvn.reshape(nb, t_seq, 2 * nheads, HEAD_DIM), rb=min(w, 256))
                kv_s.append(newbuf.reshape(1, nb, w, 2, nheads, HEAD_DIM))
            yp = _attn_prompt(qp, kvs)
            ys = carry.reshape(1, ms, hw).astype(BF16)
            w_o = a_w_o
        else:
            zp, zs = _linear(hp, b_w_in, layer=i, n=4 * d, col_map=ident, out_dtype=F32, bm=bm, bn=bn,
                             xs=hs, name="b_in")
            hg_heads = _pick(d // HGRN_DK, (8, 4, 2, 1))
            yp, s_p = _hgrn(zp, b_lb_logits, b_norm_g[i], None, layer=layer, heads=hg_heads,
                            tb=min(t, 512), out_dtype=BF16)
            ysb, s_s = _hgrn(zs.reshape(nb, t_seq, 4 * d), b_lb_logits, b_norm_g[i], b_states[i],
                             layer=layer, heads=hg_heads, tb=t_seq, out_dtype=F32)
            ys = ysb.reshape(1, ms, d).astype(BF16)
            st_p.append(s_p)
            st_s.append(s_s)
            w_o = b_w_o
        xp, xs = _linear(yp, w_o, layer=i, n=d, col_map=ident, out_dtype=F32, bm=bm, bn=bn,
                         mode="resid", res=xp, gate=mp, gate_chunk=2, xs=ys, res_s=xs, gate_s=msd,
                         name="mix_o")
        hp = _norm_mod(xp, norm_ffn_g[layer], mp, 4, 3, bm=bm_e)
        hs = _norm_mod(xs, norm_ffn_g[layer], msd, 4, 3, bm=ms)
        bnf = 256
        nff = d_ff // bnf
        actp, acts = _linear(hp, ffn_w_in, layer=layer, n=d_ff, col_map=ident, col_map2=lambda j: nff + j,
                             out_dtype=BF16, bm=bm, bn=bnf, mode="swiglu", xs=hs, name="ffn_in")
        half_ff = d_ff // 2
        bk = half_ff if (d_ff % 2 == 0 and half_ff % V7X_LANES == 0) else d_ff
        xp, xs = _linear(actp, ffn_w_out_bf16, layer=layer, n=d, col_map=ident, out_dtype=F32,
                         bm=bm, bn=bn, bk=bk, mode="resid", res=xp, gate=mp, gate_chunk=5,
                         xs=acts, res_s=xs, gate_s=msd, name="ffn_out")
    yp = _final_norm(xp, final_g, bm=bm_e)
    ys = _final_norm(xs, final_g, bm=ms)
    return yp, ys, kv_p, kv_s, st_p, st_s


def kernel(x_prompt, x_sample, state_a_kv_w128, state_a_kv_w512, state_a_kv_w2048, state_b_rec,
           c_prompt, c_sample, ada_w, ada_b, norm_mix_g, norm_ffn_g, a_w_qkv, a_w_o,
           b_w_in, b_lb_logits, b_norm_g, b_w_o, ffn_w_in, ffn_w_out, final_g):
    bp, sp, d = x_prompt.shape
    bs, ts, _ = x_sample.shape
    depth = ada_w.shape[0]
    d_ff = ffn_w_out.shape[1]
    ffn_w_out_bf16 = _to_bf16(ffn_w_out, kb=_pick(d_ff, (d_ff // 16, d_ff // 8, d_ff // 4, d_ff // 2)))
    weights = (norm_mix_g, norm_ffn_g, a_w_qkv, a_w_o, b_w_in, b_lb_logits, b_norm_g, b_w_o,
               ffn_w_in, ffn_w_out_bf16, final_g)

    rows = bp + bs
    rpad = -rows % V7X_SUBLANES
    c_all = jnp.pad(jnp.concatenate([c_prompt, c_sample], axis=0), ((0, rpad), (0, 0)))
    mod = _ada_modulation(c_all, ada_w, ada_b, bn=512)
    mods_p = [mod[l, :bp].reshape(bp, 1, 6 * d) for l in range(depth)]
    mods_s = [jnp.repeat(mod[l, bp:rows], ts, axis=0).reshape(1, bs * ts, 6 * d) for l in range(depth)]

    y_p, y_s, kv_p, kv_s, st_p, st_s = _forward(
        x_prompt, x_sample.reshape(1, bs * ts, d), mods_p, mods_s,
        (state_a_kv_w128, state_a_kv_w512, state_a_kv_w2048), state_b_rec, weights, nb=bs, t_seq=ts)
    return (y_p, y_s.reshape(bs, ts, d), kv_p[0], kv_s[0], kv_p[1], kv_s[1], kv_p[2], kv_s[2],
            jnp.stack(st_p, axis=0), jnp.stack(st_s, axis=0))
```

```python
import functools
import math

import jax
import jax.numpy as jnp
from jax import lax
from jax.experimental import pallas as pl
from jax.experimental.pallas import tpu as pltpu

F32 = jnp.float32
BF16 = jnp.bfloat16

EPS = 1e-6
DILATIONS = (1, 4, 16)
N_BACK = 128
HEAD_DIM = 128
HGRN_DK = 128
HGRN_CHUNK = 16
LOG2_E = 1.4426950408889634
N_MIXERS = 2
V7X_LANES = 128
V7X_SUBLANES = 8
V7X_SCOPED_VMEM_BYTES = 60000 * 1024


def _params(nbytes, n_axes):
    limit = int(min(V7X_SCOPED_VMEM_BYTES, max(2 * nbytes, 16 * 1024 * 1024)))
    return pltpu.CompilerParams(dimension_semantics=("arbitrary",) * n_axes, vmem_limit_bytes=limit)


def _nbytes(shape, dtype):
    return math.prod(shape) * jnp.dtype(dtype).itemsize


def _silu(x):
    return x * jax.nn.sigmoid(x)


def _pick(total, prefs):
    for p in prefs:
        if total % p == 0:
            return p
    return total


def _ada_kernel(c_ref, w_ref, b_ref, o_ref):
    s = _silu(c_ref[...])
    acc = jnp.dot(s.astype(BF16), w_ref[...].astype(BF16), preferred_element_type=F32)
    o_ref[...] = acc + b_ref[...]


def _ada_modulation(c_all, ada_w, ada_b, *, bn):
    depth, d, n = ada_w.shape
    r = c_all.shape[0]
    est = 2 * (_nbytes((d, bn), F32) + _nbytes((r, bn), F32)) + _nbytes((d, bn), BF16) + _nbytes((r, d), F32)
    return pl.pallas_call(
        _ada_kernel,
        out_shape=jax.ShapeDtypeStruct((depth, r, n), F32),
        grid=(depth, n // bn),
        in_specs=[
            pl.BlockSpec((r, d), lambda l, j: (0, 0)),
            pl.BlockSpec((None, d, bn), lambda l, j: (l, 0, j)),
            pl.BlockSpec((None, 1, bn), lambda l, j: (l, 0, j)),
        ],
        out_specs=pl.BlockSpec((None, r, bn), lambda l, j: (l, 0, j)),
        compiler_params=_params(est, 2),
        name="ada_modulation",
    )(c_all, ada_w, ada_b.reshape(depth, 1, n))


def _norm_mod_kernel(x_ref, g_ref, sc_ref, sh_ref, o_ref):
    x = x_ref[...]
    y = x * lax.rsqrt(jnp.mean(x * x, axis=-1, keepdims=True) + EPS)
    h = (y * g_ref[...]) * (1.0 + sc_ref[...]) + sh_ref[...]
    o_ref[...] = h.astype(o_ref.dtype)


def _norm_mod(x, g, mod, sc_chunk, sh_chunk, *, bm):
    bn_, t, d = x.shape
    tg = mod.shape[1]
    mg = 1 if tg == 1 else bm

    def mod_map(chunk):
        return lambda b, i: (b, 0 if tg == 1 else i, chunk)

    est = 2 * (_nbytes((bm, d), F32) + _nbytes((bm, d), BF16)) + 2 * _nbytes((bm, d), F32)
    return pl.pallas_call(
        _norm_mod_kernel,
        out_shape=jax.ShapeDtypeStruct((bn_, t, d), BF16),
        grid=(bn_, t // bm),
        in_specs=[
            pl.BlockSpec((None, bm, d), lambda b, i: (b, i, 0)),
            pl.BlockSpec((1, d), lambda b, i: (0, 0)),
            pl.BlockSpec((None, mg, d), mod_map(sc_chunk)),
            pl.BlockSpec((None, mg, d), mod_map(sh_chunk)),
        ],
        out_specs=pl.BlockSpec((None, bm, d), lambda b, i: (b, i, 0)),
        compiler_params=_params(est, 2),
        name="norm_mod",
    )(x, g.reshape(1, d), mod, mod)


def _final_norm_kernel(x_ref, g_ref, o_ref):
    x = x_ref[...]
    y = x * lax.rsqrt(jnp.mean(x * x, axis=-1, keepdims=True) + EPS)
    o_ref[...] = y * g_ref[...]


def _final_norm(x, g, *, bm):
    bn_, t, d = x.shape
    est = 6 * _nbytes((bm, d), F32)
    return pl.pallas_call(
        _final_norm_kernel,
        out_shape=jax.ShapeDtypeStruct((bn_, t, d), F32),
        grid=(bn_, t // bm),
        in_specs=[
            pl.BlockSpec((None, bm, d), lambda b, i: (b, i, 0)),
            pl.BlockSpec((1, d), lambda b, i: (0, 0)),
        ],
        out_specs=pl.BlockSpec((None, bm, d), lambda b, i: (b, i, 0)),
        compiler_params=_params(est, 2),
        name="final_norm",
    )(x, g.reshape(1, d))


def _to_bf16_kernel(w_ref, o_ref):
    o_ref[...] = w_ref[...].astype(BF16)


def _to_bf16(w, *, kb):
    l, k, n = w.shape
    assert k % kb == 0
    est = 2 * (_nbytes((kb, n), F32) + _nbytes((kb, n), BF16))
    spec = pl.BlockSpec((None, kb, n), lambda li, i: (li, i, 0))
    return pl.pallas_call(
        _to_bf16_kernel,
        out_shape=jax.ShapeDtypeStruct(w.shape, BF16),
        grid=(l, k // kb),
        in_specs=[spec],
        out_specs=spec,
        compiler_params=_params(est, 2),
        name="weight_to_bf16",
    )(w)


def _linear_kernel(*refs, nk, mode, extra, side_cast):
    refs = list(refs)
    x_ref, w_ref = refs[0], refs[1]
    pos = 2
    w2_ref = res_ref = gate_ref = xs_ref = ress_ref = gates_ref = os_ref = None
    if mode == "swiglu":
        w2_ref = refs[pos]
        pos += 1
    if mode == "resid":
        res_ref, gate_ref = refs[pos], refs[pos + 1]
        pos += 2
    if extra:
        xs_ref = refs[pos]
        pos += 1
        if mode == "resid":
            ress_ref, gates_ref = refs[pos], refs[pos + 1]
            pos += 2
    if side_cast:
        cast_in_ref = refs[pos]
        pos += 1
    o_ref = refs[pos]
    pos += 1
    if extra:
        os_ref = refs[pos]
        pos += 1
    if side_cast:
        refs[pos][...] = cast_in_ref[...].astype(BF16)
        pos += 1
    scratch = refs[pos:]

    def finalize(out_ref, r_ref, g_ref, acc, acc2=None):
        if mode == "swiglu":
            out_ref[...] = (_silu(acc) * acc2).astype(out_ref.dtype)
        elif mode == "resid":
            out_ref[...] = (r_ref[...] + g_ref[...] * acc).astype(out_ref.dtype)
        else:
            out_ref[...] = acc.astype(out_ref.dtype)

    cast = w_ref.dtype != BF16
    first = jnp.logical_and(pl.program_id(1) == 0, pl.program_id(2) == 0)
    if nk == 1:
        wb_ref, wb2_ref = w_ref, w2_ref
        if cast:
            wb_ref = scratch[0]
            wb2_ref = scratch[1] if mode == "swiglu" else None

        def first_step():
            if cast:
                wb_ref[...] = w_ref[...].astype(BF16)
                if mode == "swiglu":
                    wb2_ref[...] = w2_ref[...].astype(BF16)
            if extra:
                xs = xs_ref[...]
                accs = jnp.dot(xs, wb_ref[...], preferred_element_type=F32)
                accs2 = None
                if mode == "swiglu":
                    accs2 = jnp.dot(xs, wb2_ref[...], preferred_element_type=F32)
                finalize(os_ref, ress_ref, gates_ref, accs, accs2)

        if cast or extra:
            pl.when(first)(first_step)

        x = x_ref[...]
        acc = jnp.dot(x, wb_ref[...], preferred_element_type=F32)
        acc2 = None
        if mode == "swiglu":
            acc2 = jnp.dot(x, wb2_ref[...], preferred_element_type=F32)
        finalize(o_ref, res_ref, gate_ref, acc, acc2)
    else:
        acc_ref = scratch[0]
        k = pl.program_id(3)
        w = w_ref[...].astype(BF16) if cast else w_ref[...]
        part = jnp.dot(x_ref[...], w, preferred_element_type=F32)

        @pl.when(k == 0)
        def _():
            acc_ref[...] = part

        @pl.when(k > 0)
        def _():
            acc_ref[...] += part

        @pl.when(k == nk - 1)
        def _():
            finalize(o_ref, res_ref, gate_ref, acc_ref[...])

        if extra:
            accs_ref = scratch[1]

            @pl.when(first)
            def _():
                parts = jnp.dot(xs_ref[...], w, preferred_element_type=F32)

                @pl.when(k == 0)
                def _():
                    accs_ref[...] = parts

                @pl.when(k > 0)
                def _():
                    accs_ref[...] += parts

                @pl.when(k == nk - 1)
                def _():
                    finalize(os_ref, ress_ref, gates_ref, accs_ref[...])


def _linear(x, w, *, layer, n, col_map, out_dtype, bm, bn, bk=None, mode="plain",
            col_map2=None, res=None, gate=None, gate_chunk=0, xs=None, res_s=None, gate_s=None,
            cast_src=None, cast_layer=0, name="linear"):
    bn_, t, kdim = x.shape
    bk = kdim if bk is None else bk
    nk = kdim // bk
    assert kdim % bk == 0 and t % bm == 0 and n % bn == 0
    assert mode != "swiglu" or nk == 1
    grid = (n // bn, bn_, t // bm, nk)

    in_specs = [
        pl.BlockSpec((None, bm, bk), lambda j, b, i, k: (b, i, k)),
        pl.BlockSpec((None, bk, bn), lambda j, b, i, k: (layer, k, col_map(j))),
    ]
    args = [x, w]
    est = 2 * (_nbytes((bm, bk), BF16) + _nbytes((bk, bn), w.dtype) + _nbytes((bm, bn), out_dtype))
    est += _nbytes((bk, bn), BF16) + 2 * _nbytes((bm, bn), F32)
    if mode == "swiglu":
        in_specs.append(pl.BlockSpec((None, bk, bn), lambda j, b, i, k: (layer, k, col_map2(j))))
        args.append(w)
        est += 2 * _nbytes((bk, bn), F32) + _nbytes((bk, bn), BF16) + _nbytes((bm, bn), F32)
    if mode == "resid":
        tg = gate.shape[1]
        mg = 1 if tg == 1 else bm
        nb = n // bn
        in_specs.append(pl.BlockSpec((None, bm, bn), lambda j, b, i, k: (b, i, j)))
        in_specs.append(pl.BlockSpec(
            (None, mg, bn), lambda j, b, i, k: (b, 0 if tg == 1 else i, gate_chunk * nb + j)))
        args += [res, gate]
        est += 4 * _nbytes((bm, bn), F32)

    out_shape = jax.ShapeDtypeStruct((bn_, t, n), out_dtype)
    out_specs = pl.BlockSpec((None, bm, bn), lambda j, b, i, k: (b, i, j))
    ms = 0
    if xs is not None:
        ms = xs.shape[1]
        in_specs.append(pl.BlockSpec((None, ms, bk), lambda j, b, i, k: (0, 0, k)))
        args.append(xs)
        if mode == "resid":
            nb = n // bn
            in_specs.append(pl.BlockSpec((None, ms, bn), lambda j, b, i, k: (0, 0, j)))
            in_specs.append(pl.BlockSpec((None, ms, bn), lambda j, b, i, k: (0, 0, gate_chunk * nb + j)))
            args += [res_s, gate_s]
        out_shape = (out_shape, jax.ShapeDtypeStruct((1, ms, n), out_dtype))
        out_specs = (out_specs, pl.BlockSpec((None, ms, bn), lambda j, b, i, k: (0, 0, j)))
        est += 2 * _nbytes((ms, bk), BF16) + 8 * _nbytes((ms, bn), F32)

    if cast_src is not None:
        steps = math.prod(grid)
        kc, nc = cast_src.shape[1:]
        slab = kc // steps
        assert nk == 1 and kc % steps == 0 and slab % 16 == 0
        nt_ = t // bm
        step_map = lambda j, b, i, k: (j * bn_ + b) * nt_ + i
        in_specs.append(pl.BlockSpec((None, slab, nc), lambda j, b, i, k: (cast_layer, step_map(j, b, i, k), 0)))
        args.append(cast_src)
        out_shape = tuple(out_shape) if isinstance(out_shape, tuple) else (out_shape,)
        out_specs = tuple(out_specs) if isinstance(out_specs, tuple) else (out_specs,)
        out_shape += (jax.ShapeDtypeStruct((kc, nc), BF16),)
        out_specs += (pl.BlockSpec((slab, nc), lambda j, b, i, k: (step_map(j, b, i, k), 0)),)
        est += 2 * (_nbytes((slab, nc), F32) + _nbytes((slab, nc), BF16))

    if nk > 1:
        scratch = [pltpu.VMEM((bm, bn), F32)] + ([pltpu.VMEM((ms, bn), F32)] if ms else [])
    elif w.dtype != BF16:
        scratch = [pltpu.VMEM((bk, bn), BF16)] * (2 if mode == "swiglu" else 1)
    else:
        scratch = []

    return pl.pallas_call(
        functools.partial(_linear_kernel, nk=nk, mode=mode, extra=xs is not None,
                          side_cast=cast_src is not None),
        out_shape=out_shape,
        grid=grid,
        in_specs=in_specs,
        out_specs=out_specs,
        scratch_shapes=scratch,
        compiler_params=_params(est, 4),
        name=name,
    )(*args)


def _attn_prompt_kernel(*refs, ngroups):
    q_refs, k_refs, v_refs = refs[:ngroups], refs[ngroups:2 * ngroups], refs[2 * ngroups:3 * ngroups]
    o_ref, og_ref, lg_ref = refs[3 * ngroups:]
    seq = o_ref.shape[0]
    scale = HEAD_DIM ** -0.5
    nt = (((1,), (1,)), ((), ()))
    neg = -jnp.inf
    row1 = lax.broadcasted_iota(jnp.int32, (N_BACK, N_BACK), 0)
    col1 = lax.broadcasted_iota(jnp.int32, (N_BACK, N_BACK), 1)
    row2 = lax.broadcasted_iota(jnp.int32, (N_BACK, 2 * N_BACK), 0)
    col2 = lax.broadcasted_iota(jnp.int32, (N_BACK, 2 * N_BACK), 1)
    ok_first = col1 <= row1
    ok_band = jnp.logical_or(jnp.logical_and(col2 < N_BACK, col2 >= row2),
                             jnp.logical_and(col2 >= N_BACK, col2 - N_BACK <= row2))
    ones = jnp.ones((N_BACK, HEAD_DIM), BF16)

    for g in range(ngroups):
        dil = DILATIONS[g]
        length = seq // dil
        nblk = length // N_BACK
        for r in range(dil):
            sub_rows = slice(0, length) if dil == 1 else pl.ds(r, length, stride=dil)
            q = q_refs[g][sub_rows, :].astype(BF16)
            k = k_refs[g][sub_rows, :].astype(BF16)
            v = v_refs[g][sub_rows, :].astype(BF16)
            scores = []
            for i in range(nblk):
                qi = q[i * N_BACK:(i + 1) * N_BACK]
                keys = k[max(i - 1, 0) * N_BACK:(i + 1) * N_BACK]
                scores.append(lax.dot_general(qi, keys, nt, preferred_element_type=F32) * scale)
            probs, maxes = [], []
            for i in range(nblk):
                s = jnp.where(ok_first if i == 0 else ok_band, scores[i], neg)
                m = jnp.max(s, axis=-1, keepdims=True)
                probs.append(jnp.exp(s - m).astype(BF16))
                maxes.append(m)
            for i in range(nblk):
                vals = v[max(i - 1, 0) * N_BACK:(i + 1) * N_BACK]
                vext = jnp.concatenate([vals, jnp.concatenate([ones] * (vals.shape[0] // N_BACK), axis=0)], axis=1)
                acc = jnp.dot(probs[i], vext, preferred_element_type=F32)
                den = acc[:, HEAD_DIM:]
                start = r + i * N_BACK * dil
                out_rows = slice(start, start + N_BACK) if dil == 1 else pl.ds(start, N_BACK, stride=dil)
                og_ref.at[g][out_rows, :] = acc[:, :HEAD_DIM] / den
                lg_ref.at[g][out_rows, :] = maxes[i] + jnp.log(den)

    rb = 256
    for c in range(seq // rb):
        rows = slice(c * rb, (c + 1) * rb)
        ls = [lg_ref[g, rows, :] for g in range(ngroups)]
        m = functools.reduce(jnp.maximum, ls)
        es = [jnp.exp(l - m) for l in ls]
        tot = functools.reduce(lambda a, b: a + b, es)
        num = functools.reduce(lambda a, b: a + b, [es[g] * og_ref[g, rows, :] for g in range(ngroups)])
        o_ref[rows, :] = (num / tot).astype(o_ref.dtype)


def _attn_prompt(q, kvs):
    ngroups = len(DILATIONS)
    b, s, qc = q.shape
    hw = qc // ngroups
    nh = hw // HEAD_DIM
    assert all(s % (d * N_BACK) == 0 for d in DILATIONS) and s % 256 == 0
    blk = (None, s, HEAD_DIM)
    in_specs = ([pl.BlockSpec(blk, (lambda bi, h, g=g: (bi, 0, g * nh + h))) for g in range(ngroups)]
                + [pl.BlockSpec(blk, lambda bi, h: (bi, 0, h))] * ngroups
                + [pl.BlockSpec(blk, lambda bi, h: (bi, 0, nh + h))] * ngroups)
    est = (2 * (3 * ngroups * _nbytes((s, HEAD_DIM), F32) + _nbytes((s, HEAD_DIM), BF16))
           + 2 * ngroups * _nbytes((s, HEAD_DIM), F32) + 6 * _nbytes((s, HEAD_DIM), F32))
    return pl.pallas_call(
        functools.partial(_attn_prompt_kernel, ngroups=ngroups),
        out_shape=jax.ShapeDtypeStruct((b, s, hw), BF16),
        grid=(b, nh),
        in_specs=in_specs,
        out_specs=pl.BlockSpec(blk, lambda bi, h: (bi, 0, h)),
        scratch_shapes=[pltpu.VMEM((ngroups, s, HEAD_DIM), F32), pltpu.VMEM((ngroups, s, HEAD_DIM), F32)],
        compiler_params=_params(est, 2),
        name="attn_prompt",
    )(*([q] * ngroups), *kvs, *kvs)


def _attn_cached_kernel(*refs, dil, nheads, has_carry, final):
    if has_carry:
        q_ref, kn_ref, vn_ref, kb_ref, vb_ref, m_in, l_in, a_in = refs[:8]
        outs = refs[8:]
    else:
        q_ref, kn_ref, vn_ref, kb_ref, vb_ref = refs[:5]
        outs = refs[5:]
    if final:
        o_ref, m_sc, l_sc, a_sc = outs
    else:
        m_out, l_out, a_out, m_sc, l_sc, a_sc = outs
    c = pl.program_id(1)
    nc = pl.num_programs(1)
    wc = kb_ref.shape[0]
    w_total = wc * nc
    nq = q_ref.shape[0]
    hshift = nheads.bit_length() - 1
    scale = HEAD_DIM ** -0.5
    nt = (((1,), (1,)), ((), ()))
    neg = -jnp.inf
    q = q_ref[...].astype(BF16)

    def attend(k, v, ok):
        s = lax.dot_general(q, k.astype(BF16), nt, preferred_element_type=F32) * scale
        s = jnp.where(ok, s, neg)
        m_old = m_sc[...]
        m_new = jnp.maximum(m_old, jnp.max(s, axis=-1, keepdims=True))
        alpha = jnp.exp(m_old - m_new)
        p = jnp.exp(s - m_new)
        l_sc[...] = alpha * l_sc[...] + jnp.sum(p, axis=-1, keepdims=True)
        a_sc[...] = alpha * a_sc[...] + jnp.dot(p.astype(BF16), v.astype(BF16), preferred_element_type=F32)
        m_sc[...] = m_new

    @pl.when(c == 0)
    def _():
        if has_carry:
            m_sc[...] = m_in[...]
            l_sc[...] = l_in[...]
            a_sc[...] = a_in[...]
        else:
            m_sc[...] = jnp.full(m_sc.shape, neg, F32)
            l_sc[...] = jnp.zeros(l_sc.shape, F32)
            a_sc[...] = jnp.zeros(a_sc.shape, F32)
        qr = lax.broadcasted_iota(jnp.int32, (nq, nq), 0)
        kr = lax.broadcasted_iota(jnp.int32, (nq, nq), 1)
        dt = (qr >> hshift) - (kr >> hshift)
        ok = ((qr & (nheads - 1)) == (kr & (nheads - 1))) & (dt >= 0) & ((dt & (dil - 1)) == 0)
        attend(kn_ref[...], vn_ref[...], ok)

    qr = lax.broadcasted_iota(jnp.int32, (nq, wc * nheads), 0)
    kr = lax.broadcasted_iota(jnp.int32, (nq, wc * nheads), 1)
    dist = w_total + (qr >> hshift) - (c * wc + (kr >> hshift))
    ok = (((qr & (nheads - 1)) == (kr & (nheads - 1))) & ((dist & (dil - 1)) == 0)
          & (dist <= dil * N_BACK))
    attend(kb_ref[...].reshape(wc * nheads, HEAD_DIM), vb_ref[...].reshape(wc * nheads, HEAD_DIM), ok)

    @pl.when(c == nc - 1)
    def _():
        if final:
            o_ref[...] = a_sc[...] / l_sc[...]
        else:
            m_out[...] = m_sc[...]
            l_out[...] = l_sc[...]
            a_out[...] = a_sc[...]


def _attn_cached(qf, knf, vnf, buf, g, carry, *, final, wc):
    dil = DILATIONS[g]
    b, nq, hd = qf.shape
    w, nheads = buf.shape[1], buf.shape[3]
    assert w % wc == 0 and nheads & (nheads - 1) == 0 and dil & (dil - 1) == 0
    row = pl.BlockSpec((None, nq, hd), lambda bi, c: (bi, 0, 0))
    stat = pl.BlockSpec((None, nq, 1), lambda bi, c: (bi, 0, 0))
    in_specs = [row, row, row,
                pl.BlockSpec((None, wc, None, nheads, hd), lambda bi, c: (bi, c, 0, 0, 0)),
                pl.BlockSpec((None, wc, None, nheads, hd), lambda bi, c: (bi, c, 1, 0, 0))]
    args = [qf, knf, vnf, buf, buf]
    if carry is not None:
        in_specs += [stat, stat, row]
        args += list(carry)
    if final:
        out_shape = jax.ShapeDtypeStruct((b, nq, hd), F32)
        out_specs = row
    else:
        out_shape = (jax.ShapeDtypeStruct((b, nq, 1), F32), jax.ShapeDtypeStruct((b, nq, 1), F32),
                     jax.ShapeDtypeStruct((b, nq, hd), F32))
        out_specs = (stat, stat, row)
    est = 4 * _nbytes((wc, nheads, hd), F32) + 4 * _nbytes((nq, wc * nheads), F32) + 16 * _nbytes((nq, hd), F32)
    return pl.pallas_call(
        functools.partial(_attn_cached_kernel, dil=dil, nheads=nheads, has_carry=carry is not None, final=final),
        out_shape=out_shape,
        grid=(b, w // wc),
        in_specs=in_specs,
        out_specs=out_specs,
        scratch_shapes=[pltpu.VMEM((nq, 1), F32), pltpu.VMEM((nq, 1), F32), pltpu.VMEM((nq, hd), F32)],
        compiler_params=_params(est, 2),
        name=f"attn_cached_g{g}",
    )(*args)


def _shift_kernel(cur_ref, nxt_ref, new_ref, o_ref, *, t_new):
    last = pl.program_id(1) == pl.num_programs(1) - 1
    rb = cur_ref.shape[0]
    o_ref[0:rb - t_new] = cur_ref[t_new:rb]
    o_ref[rb - t_new:rb] = jnp.where(last, new_ref[...], nxt_ref[...])


def _shift_append(buf, new, *, rb):
    b, w, r, lanes = buf.shape
    t_new = new.shape[1]
    assert w % rb == 0 and rb % t_new == 0 and rb > t_new
    last_blk = w // t_new - 1
    est = 4 * _nbytes((rb, r, lanes), F32) + 6 * _nbytes((t_new, r, lanes), F32)
    return pl.pallas_call(
        functools.partial(_shift_kernel, t_new=t_new),
        out_shape=jax.ShapeDtypeStruct(buf.shape, F32),
        grid=(b, w // rb),
        in_specs=[
            pl.BlockSpec((None, rb, r, lanes), lambda bi, j: (bi, j, 0, 0)),
            pl.BlockSpec((None, t_new, r, lanes),
                         lambda bi, j: (bi, jnp.minimum((j + 1) * (rb // t_new), last_blk), 0, 0)),
            pl.BlockSpec((None, t_new, r, lanes), lambda bi, j: (bi, 0, 0, 0)),
        ],
        out_specs=pl.BlockSpec((None, rb, r, lanes), lambda bi, j: (bi, j, 0, 0)),
        compiler_params=_params(est, 2),
        name="kv_shift_append",
    )(buf, buf, new)


def _hgrn_kernel(*refs, heads, chunk, layer, has_state):
    if has_state:
        zq_ref, zf_ref, zv_ref, zg_ref, lb_ref, ng_ref, s0_ref = refs[:7]
        rest = refs[7:]
    else:
        zq_ref, zf_ref, zv_ref, zg_ref, lb_ref, ng_ref = refs[:6]
        s0_ref = None
        rest = refs[6:]
    y_ref, st_out_ref, st_ref = rest[:3]
    bc_ref = rest[3] if len(rest) > 3 else None
    tstep = pl.program_id(2)
    nsteps = pl.num_programs(2)
    tb = zq_ref.shape[0]

    @pl.when(tstep == 0)
    def _():
        for h in range(heads):
            if has_state:
                st_ref[h] = s0_ref[h].T
            else:
                st_ref[h] = jnp.zeros((HGRN_DK, HGRN_DK), F32)

    logits = lb_ref[...]
    ex = jnp.exp(logits - jnp.max(logits, axis=0, keepdims=True))
    sm = ex / jnp.sum(ex, axis=0, keepdims=True)
    lb = jnp.zeros_like(sm[0:1])
    for l in range(1, layer + 1):
        lb = lb + sm[l:l + 1]
    ng = ng_ref[...]

    sub = V7X_SUBLANES
    rblk = sub if chunk % sub == 0 else chunk
    nblk = chunk // rblk
    rowi = lax.broadcasted_iota(jnp.int32, (rblk, HGRN_DK), 0)
    rowi_all = lax.broadcasted_iota(jnp.int32, (chunk, zq_ref.shape[1]), 0)
    nt = (((1,), (1,)), ((), ()))
    tn = (((0,), (0,)), ((), ()))

    def body(ci, carry):
        rows = pl.ds(pl.multiple_of(ci * chunk, chunk), chunk)
        zq = zq_ref[rows, :]
        zf = zf_ref[rows, :]
        zv = zv_ref[rows, :]
        zg = zg_ref[rows, :]
        q_all = _silu(zq)
        f_all = lb + (1.0 - lb) * jax.nn.sigmoid(zf)
        lf_all = jnp.log(f_all)
        k_all = 1.0 - f_all
        og_all = ng * _silu(zg)
        if chunk % sub == 0:
            a_all = lf_all
            shift = 1
            while shift < chunk:
                a_all = a_all + jnp.where(rowi_all >= shift, pltpu.roll(a_all, shift, 0), 0.0)
                shift *= 2
        else:
            a_all = jnp.zeros_like(lf_all)
            for s in range(chunk):
                a_all = a_all + jnp.where(rowi_all >= s, lf_all[s:s + 1, :], 0.0)
        a_all = a_all * LOG2_E
        qe_all = q_all * jnp.exp2(a_all)
        a_last_all = a_all[chunk - 1:chunk, :]
        kd_all = k_all * jnp.exp2(a_last_all - a_all)
        dec_all = jnp.exp2(a_last_all)
        if bc_ref is not None:
            for h in range(heads):
                lanes = slice(h * HGRN_DK, (h + 1) * HGRN_DK)
                bc_ref[0, h] = a_all[:, lanes]
                bc_ref[1, h] = k_all[:, lanes]
                bc_ref[2, h] = zv[:, lanes]
        o_inter, scores, states_in = [], [], []
        for h in range(heads):
            lanes = slice(h * HGRN_DK, (h + 1) * HGRN_DK)
            q, k, a = q_all[:, lanes], k_all[:, lanes], a_all[:, lanes]
            st = st_ref[h]
            states_in.append(st)
            o_inter.append(lax.dot_general(qe_all[:, lanes].astype(BF16), st.astype(BF16), nt,
                                           preferred_element_type=F32))
            sc = None
            for j in range(nblk - 1):
                r0, r1 = j * rblk, (j + 1) * rblk
                b_j = a[r1 - 1:r1, :]
                qt = jnp.concatenate([jnp.zeros((r1, HGRN_DK), F32), q[r1:] * jnp.exp2(a[r1:] - b_j)], axis=0)
                kh = [k[r0:r1] * jnp.exp2(b_j - a[r0:r1])]
                if r0 > 0:
                    kh.insert(0, jnp.zeros((r0, HGRN_DK), F32))
                kh.append(jnp.zeros((chunk - r1, HGRN_DK), F32))
                sc_j = lax.dot_general(qt.astype(BF16), jnp.concatenate(kh, axis=0).astype(BF16), nt,
                                       preferred_element_type=F32)
                sc = sc_j if sc is None else sc + sc_j
            scores.append(sc)
        for h in range(heads):
            lanes = slice(h * HGRN_DK, (h + 1) * HGRN_DK)
            q, k, v, a = q_all[:, lanes], k_all[:, lanes], zv[:, lanes], a_all[:, lanes]
            st = states_in[h]
            o = o_inter[h]
            if scores[h] is not None:
                o = o + jnp.dot(scores[h].astype(BF16), v.astype(BF16), preferred_element_type=F32)
            ob = [None] * nblk
            for s in range(chunk):
                bi = s // rblk
                r0 = bi * rblk
                if bc_ref is not None:
                    a_s, k_s, v_s = (jnp.broadcast_to(bc_ref[c, h, s:s + 1, :], (rblk, HGRN_DK))
                                     for c in range(3))
                else:
                    a_s, k_s, v_s = a[s:s + 1, :], k[s:s + 1, :], v[s:s + 1, :]
                e = jnp.where(rowi >= s - r0, jnp.exp2(a[r0:r0 + rblk] - a_s), 0.0)
                wgt = (q[r0:r0 + rblk] * k_s) * e
                term = jnp.sum(wgt, axis=-1, keepdims=True) * v_s
                ob[bi] = term if ob[bi] is None else ob[bi] + term
            o = o + (ob[0] if nblk == 1 else jnp.concatenate(ob, axis=0))
            upd = lax.dot_general(v.astype(BF16), kd_all[:, lanes].astype(BF16), tn,
                                  preferred_element_type=F32)
            st_ref[h] = st * dec_all[:, lanes] + upd
            on = o * lax.rsqrt(jnp.mean(o * o, axis=-1, keepdims=True) + EPS)
            y_ref[rows, lanes] = (on * og_all[:, lanes]).astype(y_ref.dtype)
        return carry

    lax.fori_loop(0, tb // chunk, body, 0)

    @pl.when(tstep == nsteps - 1)
    def _():
        for h in range(heads):
            st_out_ref[h] = st_ref[h].T


def _hgrn(z, lb_logits, norm_g, s0, *, layer, heads, tb, out_dtype):
    b, t, d4 = z.shape
    d = d4 // 4
    nh = d // HGRN_DK
    nhg = nh // heads
    hb = heads * HGRN_DK
    chunk = math.gcd(t, HGRN_CHUNK)
    depth = lb_logits.shape[0]
    assert t % tb == 0 and tb % chunk == 0 and nh % heads == 0

    def zspec(part):
        return pl.BlockSpec((None, tb, hb), lambda bi, hg, i: (bi, i, part * nhg + hg))

    in_specs = [zspec(0), zspec(1), zspec(2), zspec(3),
                pl.BlockSpec((depth, hb), lambda bi, hg, i: (0, hg)),
                pl.BlockSpec((1, hb), lambda bi, hg, i: (0, hg))]
    args = [z, z, z, z, lb_logits, norm_g.reshape(1, d)]
    sspec = pl.BlockSpec((None, heads, HGRN_DK, HGRN_DK), lambda bi, hg, i: (bi, hg, 0, 0))
    if s0 is not None:
        in_specs.append(sspec)
        args.append(s0)
    est = (2 * (4 * _nbytes((tb, hb), F32) + _nbytes((tb, hb), out_dtype))
           + 5 * _nbytes((heads, HGRN_DK, HGRN_DK), F32))
    return pl.pallas_call(
        functools.partial(_hgrn_kernel, heads=heads, chunk=chunk, layer=layer, has_state=s0 is not None),
        out_shape=(jax.ShapeDtypeStruct((b, t, d), out_dtype),
                   jax.ShapeDtypeStruct((b, nh, HGRN_DK, HGRN_DK), F32)),
        grid=(b, nhg, t // tb),
        in_specs=in_specs,
        out_specs=(pl.BlockSpec((None, tb, hb), lambda bi, hg, i: (bi, i, hg)), sspec),
        scratch_shapes=([pltpu.VMEM((heads, HGRN_DK, HGRN_DK), F32)]
                        + ([pltpu.VMEM((3, heads, chunk, HGRN_DK), F32)] if chunk % V7X_SUBLANES == 0 else [])),
        compiler_params=_params(est, 3),
        name="hgrn2",
    )(*args)


def _forward(xp, xs, mods_p, mods_s, a_bufs, b_states, weights, *, nb, t_seq):
    (norm_mix_g, norm_ffn_g, a_w_qkv, a_w_o, b_w_in, b_lb_logits, b_norm_g, b_w_o,
     ffn_w_in, ffn_w_out, final_g) = weights
    bp, t, d = xp.shape
    ms = xs.shape[1]
    depth = norm_mix_g.shape[0]
    d_ff = ffn_w_out.shape[1]
    ngroups = len(DILATIONS)
    hw = a_w_o.shape[1]
    nheads = hw // HEAD_DIM
    bm = _pick(t, (1024, 512, 256, 128, 64, 32, 16, 8))
    bm_e = _pick(t, (256, 128, 64, 32, 16, 8))
    bn = 512
    ident = lambda j: j
    kv_p, kv_s, st_p, st_s = [], [], [], []
    w_dn_all = None

    for layer in range(depth):
        i = layer // N_MIXERS
        mp, msd = mods_p[layer], mods_s[layer]
        hp = _norm_mod(xp, norm_mix_g[layer], mp, 1, 0, bm=bm_e)
        hs = _norm_mod(xs, norm_mix_g[layer], msd, 1, 0, bm=ms)
        if layer % N_MIXERS == 0:
            qp, qs = _linear(hp, a_w_qkv, layer=i, n=ngroups * hw, col_map=ident, out_dtype=F32,
                             bm=bm, bn=bn, xs=hs, name="a_q")
            half = hw // bn
            q5 = qs.reshape(nb, t_seq, ngroups, nheads, HEAD_DIM)
            flat = lambda a: a.reshape(nb, t_seq * nheads, HEAD_DIM)
            kvs = []
            carry = None
            for g in range(ngroups):
                def kv_cols(j, g=g):
                    return jnp.where(j < half, (ngroups + g) * half + j, (2 * ngroups + g) * half + (j - half))
                kvp, kvn = _linear(hp, a_w_qkv, layer=i, n=2 * hw, col_map=kv_cols, out_dtype=F32,
                                   bm=bm, bn=bn, xs=hs, name=f"a_kv{g}")
                kvs.append(kvp)
                keep = min(DILATIONS[g] * N_BACK, t)
                kv_keep = kvp if keep == t else kvp[:, t - keep:]
                kv_p.append(kv_keep.reshape(1, bp, keep, 2, nheads, HEAD_DIM))
                buf = a_bufs[g][i]
                w = buf.shape[1]
                kvn = kvn.reshape(nb, t_seq, 2, nheads, HEAD_DIM)
                carry = _attn_cached(flat(q5[:, :, g]), flat(kvn[:, :, 0]), flat(kvn[:, :, 1]), buf, g,
                                     carry, final=g == ngroups - 1, wc=min(w, 256))
                newbuf = _shift_append(buf.reshape(nb, w, 2 * nheads, HEAD_DIM),
                                       kvn.reshape(nb, t_seq, 2 * nheads, HEAD_DIM), rb=min(w, 256))
                kv_s.append(newbuf.reshape(1, nb, w, 2, nheads, HEAD_DIM))
            yp = _attn_prompt(qp, kvs)
            ys = carry.reshape(1, ms, hw).astype(BF16)
            w_o = a_w_o
        else:
            zp, zs = _linear(hp, b_w_in, layer=i, n=4 * d, col_map=ident, out_dtype=F32, bm=bm, bn=bn,
                             xs=hs, name="b_in")
            hg_heads = _pick(d // HGRN_DK, (8, 4, 2, 1))
            yp, s_p = _hgrn(zp, b_lb_logits, b_norm_g[i], None, layer=layer, heads=hg_heads,
                            tb=min(t, 512), out_dtype=BF16)
            ysb, s_s = _hgrn(zs.reshape(nb, t_seq, 4 * d), b_lb_logits, b_norm_g[i], b_states[i],
                             layer=layer, heads=hg_heads, tb=t_seq, out_dtype=F32)
            ys = ysb.reshape(1, ms, d).astype(BF16)
            st_p.append(s_p)
            st_s.append(s_s)
            w_o = b_w_o
        xp, xs = _linear(yp, w_o, layer=i, n=d, col_map=ident, out_dtype=F32, bm=bm, bn=bn,
                         mode="resid", res=xp, gate=mp, gate_chunk=2, xs=ys, res_s=xs, gate_s=msd,
                         name="mix_o")
        hp = _norm_mod(xp, norm_ffn_g[layer], mp, 4, 3, bm=bm_e)
        hs = _norm_mod(xs, norm_ffn_g[layer], msd, 4, 3, bm=ms)
        bnf = 256
        nff = d_ff // bnf
        steps = nff * bp * (t // bm)
        if d_ff % steps == 0 and (d_ff // steps) % 16 == 0:
            actp, acts, w_dn = _linear(hp, ffn_w_in, layer=layer, n=d_ff, col_map=ident,
                                       col_map2=lambda j: nff + j, out_dtype=BF16, bm=bm, bn=bnf,
                                       mode="swiglu", xs=hs, cast_src=ffn_w_out, cast_layer=layer,
                                       name="ffn_in")
            w_dn, dn_layer = w_dn[None], 0
        else:
            actp, acts = _linear(hp, ffn_w_in, layer=layer, n=d_ff, col_map=ident,
                                 col_map2=lambda j: nff + j, out_dtype=BF16, bm=bm, bn=bnf,
                                 mode="swiglu", xs=hs, name="ffn_in")
            if w_dn_all is None:
                w_dn_all = _to_bf16(ffn_w_out, kb=_pick(d_ff, (d_ff // 16, d_ff // 8, d_ff // 4, d_ff // 2)))
            w_dn, dn_layer = w_dn_all, layer
        half_ff = d_ff // 2
        bk = half_ff if (d_ff % 2 == 0 and half_ff % V7X_LANES == 0) else d_ff
        xp, xs = _linear(actp, w_dn, layer=dn_layer, n=d, col_map=ident, out_dtype=F32,
                         bm=bm, bn=bn, bk=bk, mode="resid", res=xp, gate=mp, gate_chunk=5,
                         xs=acts, res_s=xs, gate_s=msd, name="ffn_out")
    yp = _final_norm(xp, final_g, bm=bm_e)
    ys = _final_norm(xs, final_g, bm=ms)
    return yp, ys, kv_p, kv_s, st_p, st_s


def kernel(x_prompt, x_sample, state_a_kv_w128, state_a_kv_w512, state_a_kv_w2048, state_b_rec,
           c_prompt, c_sample, ada_w, ada_b, norm_mix_g, norm_ffn_g, a_w_qkv, a_w_o,
           b_w_in, b_lb_logits, b_norm_g, b_w_o, ffn_w_in, ffn_w_out, final_g):
    bp, sp, d = x_prompt.shape
    bs, ts, _ = x_sample.shape
    depth = ada_w.shape[0]
    weights = (norm_mix_g, norm_ffn_g, a_w_qkv, a_w_o, b_w_in, b_lb_logits, b_norm_g, b_w_o,
               ffn_w_in, ffn_w_out, final_g)

    rows = bp + bs
    rpad = -rows % V7X_SUBLANES
    c_all = jnp.pad(jnp.concatenate([c_prompt, c_sample], axis=0), ((0, rpad), (0, 0)))
    mod = _ada_modulation(c_all, ada_w, ada_b, bn=512)
    mods_p = [mod[l, :bp].reshape(bp, 1, 6 * d) for l in range(depth)]
    mods_s = [jnp.repeat(mod[l, bp:rows], ts, axis=0).reshape(1, bs * ts, 6 * d) for l in range(depth)]

    y_p, y_s, kv_p, kv_s, st_p, st_s = _forward(
        x_prompt, x_sample.reshape(1, bs * ts, d), mods_p, mods_s,
        (state_a_kv_w128, state_a_kv_w512, state_a_kv_w2048), state_b_rec, weights, nb=bs, t_seq=ts)
    return (y_p, y_s.reshape(bs, ts, d), kv_p[0], kv_s[0], kv_p[1], kv_s[1], kv_p[2], kv_s[2],
            jnp.stack(st_p, axis=0), jnp.stack(st_s, axis=0))
```

```python
import functools
import math
from typing import Callable, NamedTuple

import jax
import jax.numpy as jnp
from jax import lax
from jax.experimental import pallas as pl
from jax.experimental.pallas import tpu as pltpu

F32 = jnp.float32
BF16 = jnp.bfloat16

EPS = 1e-6
DILATIONS = (1, 4, 16)
N_BACK = 128
HEAD_DIM = 128
HGRN_DK = 128
HGRN_CHUNK = 16
LOG2_E = 1.4426950408889634
N_MIXERS = 2
V7X_LANES = 128
V7X_SUBLANES = 8
V7X_SCOPED_VMEM_BYTES = 60000 * 1024


def _params(nbytes, n_axes):
    limit = int(min(V7X_SCOPED_VMEM_BYTES, max(2 * nbytes, 16 * 1024 * 1024)))
    return pltpu.CompilerParams(dimension_semantics=("arbitrary",) * n_axes, vmem_limit_bytes=limit)


def _nbytes(shape, dtype):
    return math.prod(shape) * jnp.dtype(dtype).itemsize


def _silu(x):
    return x * jax.nn.sigmoid(x)


def _pick(total, prefs):
    for p in prefs:
        if total % p == 0:
            return p
    return total


class _SideJob(NamedTuple):
    args: tuple
    in_blocks: tuple
    out_shapes: tuple
    out_blocks: tuple
    body: Callable
    nbytes: int

    @property
    def n_in(self):
        return len(self.in_blocks)

    @property
    def n_out(self):
        return len(self.out_blocks)


def _ada_kernel(c_ref, w_ref, b_ref, o_ref):
    s = _silu(c_ref[...])
    acc = jnp.dot(s.astype(BF16), w_ref[...].astype(BF16), preferred_element_type=F32)
    o_ref[...] = acc + b_ref[...]


def _ada_job(c_all, ada_w, ada_b3, layer, steps):
    _, d, n = ada_w.shape
    r = c_all.shape[0]
    cw = next((c for c in range(V7X_LANES, 4 * V7X_LANES + 1, V7X_LANES) if n % c == 0 and n // c <= steps), None)
    if cw is None:
        return None
    nact = n // cw
    col = lambda s: jnp.minimum(s, nact - 1)

    def body(step, ins, outs):
        @pl.when(step < nact)
        def _():
            _ada_kernel(*ins, outs[0])

    return _SideJob(
        args=(c_all, ada_w, ada_b3),
        in_blocks=(((r, d), lambda s: (0, 0)),
                   ((None, d, cw), lambda s: (layer, 0, col(s))),
                   ((None, 1, cw), lambda s: (layer, 0, col(s)))),
        out_shapes=(jax.ShapeDtypeStruct((r, n), F32),),
        out_blocks=(((r, cw), lambda s: (0, col(s))),),
        body=body,
        nbytes=2 * (_nbytes((d, cw), F32) + _nbytes((r, cw), F32)) + _nbytes((d, cw), BF16) + _nbytes((r, d), F32))


def _ada_modulation(c_all, ada_w, ada_b3, *, nlayers, bn):
    _, d, n = ada_w.shape
    depth = nlayers
    r = c_all.shape[0]
    est = 2 * (_nbytes((d, bn), F32) + _nbytes((r, bn), F32)) + _nbytes((d, bn), BF16) + _nbytes((r, d), F32)
    return pl.pallas_call(
        _ada_kernel,
        out_shape=jax.ShapeDtypeStruct((depth, r, n), F32),
        grid=(depth, n // bn),
        in_specs=[
            pl.BlockSpec((r, d), lambda l, j: (0, 0)),
            pl.BlockSpec((None, d, bn), lambda l, j: (l, 0, j)),
            pl.BlockSpec((None, 1, bn), lambda l, j: (l, 0, j)),
        ],
        out_specs=pl.BlockSpec((None, r, bn), lambda l, j: (l, 0, j)),
        compiler_params=_params(est, 2),
        name="ada_modulation",
    )(c_all, ada_w, ada_b3)


def _norm_mod_kernel(x_ref, g_ref, sc_ref, sh_ref, o_ref):
    x = x_ref[...]
    y = x * lax.rsqrt(jnp.mean(x * x, axis=-1, keepdims=True) + EPS)
    h = (y * g_ref[...]) * (1.0 + sc_ref[...]) + sh_ref[...]
    o_ref[...] = h.astype(o_ref.dtype)


def _norm_mod(x, g, mod, sc_chunk, sh_chunk, *, bm):
    bn_, t, d = x.shape
    tg = mod.shape[1]
    mg = 1 if tg == 1 else bm

    def mod_map(chunk):
        return lambda b, i: (b, 0 if tg == 1 else i, chunk)

    est = 2 * (_nbytes((bm, d), F32) + _nbytes((bm, d), BF16)) + 2 * _nbytes((bm, d), F32)
    return pl.pallas_call(
        _norm_mod_kernel,
        out_shape=jax.ShapeDtypeStruct((bn_, t, d), BF16),
        grid=(bn_, t // bm),
        in_specs=[
            pl.BlockSpec((None, bm, d), lambda b, i: (b, i, 0)),
            pl.BlockSpec((1, d), lambda b, i: (0, 0)),
            pl.BlockSpec((None, mg, d), mod_map(sc_chunk)),
            pl.BlockSpec((None, mg, d), mod_map(sh_chunk)),
        ],
        out_specs=pl.BlockSpec((None, bm, d), lambda b, i: (b, i, 0)),
        compiler_params=_params(est, 2),
        name="norm_mod",
    )(x, g.reshape(1, d), mod, mod)


def _final_norm_kernel(x_ref, g_ref, o_ref):
    x = x_ref[...]
    y = x * lax.rsqrt(jnp.mean(x * x, axis=-1, keepdims=True) + EPS)
    o_ref[...] = y * g_ref[...]


def _final_norm(x, g, *, bm):
    bn_, t, d = x.shape
    est = 6 * _nbytes((bm, d), F32)
    return pl.pallas_call(
        _final_norm_kernel,
        out_shape=jax.ShapeDtypeStruct((bn_, t, d), F32),
        grid=(bn_, t // bm),
        in_specs=[
            pl.BlockSpec((None, bm, d), lambda b, i: (b, i, 0)),
            pl.BlockSpec((1, d), lambda b, i: (0, 0)),
        ],
        out_specs=pl.BlockSpec((None, bm, d), lambda b, i: (b, i, 0)),
        compiler_params=_params(est, 2),
        name="final_norm",
    )(x, g.reshape(1, d))


def _to_bf16_kernel(w_ref, o_ref):
    o_ref[...] = w_ref[...].astype(BF16)


def _to_bf16(w, *, kb):
    l, k, n = w.shape
    assert k % kb == 0
    est = 2 * (_nbytes((kb, n), F32) + _nbytes((kb, n), BF16))
    spec = pl.BlockSpec((None, kb, n), lambda li, i: (li, i, 0))
    return pl.pallas_call(
        _to_bf16_kernel,
        out_shape=jax.ShapeDtypeStruct(w.shape, BF16),
        grid=(l, k // kb),
        in_specs=[spec],
        out_specs=spec,
        compiler_params=_params(est, 2),
        name="weight_to_bf16",
    )(w)


def _cast_job(w, layer, steps):
    kc, nc = w.shape[1:]
    if kc % steps or (kc // steps) % 16:
        return None
    slab = kc // steps

    def body(step, ins, outs):
        _to_bf16_kernel(ins[0], outs[0])

    return _SideJob(
        args=(w,),
        in_blocks=(((None, slab, nc), lambda s: (layer, s, 0)),),
        out_shapes=(jax.ShapeDtypeStruct((kc, nc), BF16),),
        out_blocks=(((slab, nc), lambda s: (s, 0)),),
        body=body,
        nbytes=2 * (_nbytes((slab, nc), F32) + _nbytes((slab, nc), BF16)))


def _linear_kernel(*refs, nk, mode, extra, side_jobs):
    refs = list(refs)
    x_ref, w_ref = refs[0], refs[1]
    pos = 2
    w2_ref = res_ref = gate_ref = xs_ref = ress_ref = gates_ref = os_ref = None
    if mode == "swiglu":
        w2_ref = refs[pos]
        pos += 1
    if mode == "resid":
        res_ref, gate_ref = refs[pos], refs[pos + 1]
        pos += 2
    if extra:
        xs_ref = refs[pos]
        pos += 1
        if mode == "resid":
            ress_ref, gates_ref = refs[pos], refs[pos + 1]
            pos += 2
    job_ins = []
    for job in side_jobs:
        job_ins.append(refs[pos:pos + job.n_in])
        pos += job.n_in
    o_ref = refs[pos]
    pos += 1
    if extra:
        os_ref = refs[pos]
        pos += 1
    step = (pl.program_id(0) * pl.num_programs(1) + pl.program_id(1)) * pl.num_programs(2) + pl.program_id(2)
    for job, ins in zip(side_jobs, job_ins):
        job.body(step, ins, refs[pos:pos + job.n_out])
        pos += job.n_out
    scratch = refs[pos:]

    def finalize(out_ref, r_ref, g_ref, acc, acc2=None):
        if mode == "swiglu":
            out_ref[...] = (_silu(acc) * acc2).astype(out_ref.dtype)
        elif mode == "resid":
            out_ref[...] = (r_ref[...] + g_ref[...] * acc).astype(out_ref.dtype)
        else:
            out_ref[...] = acc.astype(out_ref.dtype)

    cast = w_ref.dtype != BF16
    first = jnp.logical_and(pl.program_id(1) == 0, pl.program_id(2) == 0)
    if nk == 1:
        wb_ref, wb2_ref = w_ref, w2_ref
        if cast:
            wb_ref = scratch[0]
            wb2_ref = scratch[1] if mode == "swiglu" else None

        def first_step():
            if cast:
                wb_ref[...] = w_ref[...].astype(BF16)
                if mode == "swiglu":
                    wb2_ref[...] = w2_ref[...].astype(BF16)
            if extra:
                xs = xs_ref[...]
                accs = jnp.dot(xs, wb_ref[...], preferred_element_type=F32)
                accs2 = None
                if mode == "swiglu":
                    accs2 = jnp.dot(xs, wb2_ref[...], preferred_element_type=F32)
                finalize(os_ref, ress_ref, gates_ref, accs, accs2)

        if cast or extra:
            pl.when(first)(first_step)

        x = x_ref[...]
        acc = jnp.dot(x, wb_ref[...], preferred_element_type=F32)
        acc2 = None
        if mode == "swiglu":
            acc2 = jnp.dot(x, wb2_ref[...], preferred_element_type=F32)
        finalize(o_ref, res_ref, gate_ref, acc, acc2)
    else:
        acc_ref = scratch[0]
        k = pl.program_id(3)
        w = w_ref[...].astype(BF16) if cast else w_ref[...]
        part = jnp.dot(x_ref[...], w, preferred_element_type=F32)

        @pl.when(k == 0)
        def _():
            acc_ref[...] = part

        @pl.when(k > 0)
        def _():
            acc_ref[...] += part

        @pl.when(k == nk - 1)
        def _():
            finalize(o_ref, res_ref, gate_ref, acc_ref[...])

        if extra:
            accs_ref = scratch[1]

            @pl.when(first)
            def _():
                parts = jnp.dot(xs_ref[...], w, preferred_element_type=F32)

                @pl.when(k == 0)
                def _():
                    accs_ref[...] = parts

                @pl.when(k > 0)
                def _():
                    accs_ref[...] += parts

                @pl.when(k == nk - 1)
                def _():
                    finalize(os_ref, ress_ref, gates_ref, accs_ref[...])


def _linear(x, w, *, layer, n, col_map, out_dtype, bm, bn, bk=None, mode="plain",
            col_map2=None, res=None, gate=None, gate_chunk=0, xs=None, res_s=None, gate_s=None,
            side_jobs=(), name="linear"):
    bn_, t, kdim = x.shape
    bk = kdim if bk is None else bk
    nk = kdim // bk
    assert kdim % bk == 0 and t % bm == 0 and n % bn == 0
    assert mode != "swiglu" or nk == 1
    grid = (n // bn, bn_, t // bm, nk)

    in_specs = [
        pl.BlockSpec((None, bm, bk), lambda j, b, i, k: (b, i, k)),
        pl.BlockSpec((None, bk, bn), lambda j, b, i, k: (layer, k, col_map(j))),
    ]
    args = [x, w]
    est = 2 * (_nbytes((bm, bk), BF16) + _nbytes((bk, bn), w.dtype) + _nbytes((bm, bn), out_dtype))
    est += _nbytes((bk, bn), BF16) + 2 * _nbytes((bm, bn), F32)
    if mode == "swiglu":
        in_specs.append(pl.BlockSpec((None, bk, bn), lambda j, b, i, k: (layer, k, col_map2(j))))
        args.append(w)
        est += 2 * _nbytes((bk, bn), F32) + _nbytes((bk, bn), BF16) + _nbytes((bm, bn), F32)
    if mode == "resid":
        tg = gate.shape[1]
        mg = 1 if tg == 1 else bm
        nb = n // bn
        in_specs.append(pl.BlockSpec((None, bm, bn), lambda j, b, i, k: (b, i, j)))
        in_specs.append(pl.BlockSpec(
            (None, mg, bn), lambda j, b, i, k: (b, 0 if tg == 1 else i, gate_chunk * nb + j)))
        args += [res, gate]
        est += 4 * _nbytes((bm, bn), F32)

    out_shape = jax.ShapeDtypeStruct((bn_, t, n), out_dtype)
    out_specs = pl.BlockSpec((None, bm, bn), lambda j, b, i, k: (b, i, j))
    ms = 0
    if xs is not None:
        ms = xs.shape[1]
        in_specs.append(pl.BlockSpec((None, ms, bk), lambda j, b, i, k: (0, 0, k)))
        args.append(xs)
        if mode == "resid":
            nb = n // bn
            in_specs.append(pl.BlockSpec((None, ms, bn), lambda j, b, i, k: (0, 0, j)))
            in_specs.append(pl.BlockSpec((None, ms, bn), lambda j, b, i, k: (0, 0, gate_chunk * nb + j)))
            args += [res_s, gate_s]
        out_shape = (out_shape, jax.ShapeDtypeStruct((1, ms, n), out_dtype))
        out_specs = (out_specs, pl.BlockSpec((None, ms, bn), lambda j, b, i, k: (0, 0, j)))
        est += 2 * _nbytes((ms, bk), BF16) + 8 * _nbytes((ms, bn), F32)

    if side_jobs:
        assert nk == 1
        nt_ = t // bm
        out_shape = tuple(out_shape) if isinstance(out_shape, tuple) else (out_shape,)
        out_specs = tuple(out_specs) if isinstance(out_specs, tuple) else (out_specs,)

        def at_step(index_of_step):
            return lambda j, b, i, k: index_of_step((j * bn_ + b) * nt_ + i)

        for job in side_jobs:
            in_specs += [pl.BlockSpec(blk, at_step(f)) for blk, f in job.in_blocks]
            args += list(job.args)
            out_shape += tuple(job.out_shapes)
            out_specs += tuple(pl.BlockSpec(blk, at_step(f)) for blk, f in job.out_blocks)
            est += job.nbytes

    if nk > 1:
        scratch = [pltpu.VMEM((bm, bn), F32)] + ([pltpu.VMEM((ms, bn), F32)] if ms else [])
    elif w.dtype != BF16:
        scratch = [pltpu.VMEM((bk, bn), BF16)] * (2 if mode == "swiglu" else 1)
    else:
        scratch = []

    return pl.pallas_call(
        functools.partial(_linear_kernel, nk=nk, mode=mode, extra=xs is not None,
                          side_jobs=tuple(side_jobs)),
        out_shape=out_shape,
        grid=grid,
        in_specs=in_specs,
        out_specs=out_specs,
        scratch_shapes=scratch,
        compiler_params=_params(est, 4),
        name=name,
    )(*args)


def _attn_prompt_kernel(*refs, ngroups):
    q_refs, k_refs, v_refs = refs[:ngroups], refs[ngroups:2 * ngroups], refs[2 * ngroups:3 * ngroups]
    o_ref, og_ref, lg_ref = refs[3 * ngroups:]
    seq = o_ref.shape[0]
    scale = HEAD_DIM ** -0.5
    nt = (((1,), (1,)), ((), ()))
    neg = -jnp.inf
    row1 = lax.broadcasted_iota(jnp.int32, (N_BACK, N_BACK), 0)
    col1 = lax.broadcasted_iota(jnp.int32, (N_BACK, N_BACK), 1)
    row2 = lax.broadcasted_iota(jnp.int32, (N_BACK, 2 * N_BACK), 0)
    col2 = lax.broadcasted_iota(jnp.int32, (N_BACK, 2 * N_BACK), 1)
    ok_first = col1 <= row1
    ok_band = jnp.logical_or(jnp.logical_and(col2 < N_BACK, col2 >= row2),
                             jnp.logical_and(col2 >= N_BACK, col2 - N_BACK <= row2))
    ones = jnp.ones((N_BACK, HEAD_DIM), BF16)

    for g in range(ngroups):
        dil = DILATIONS[g]
        length = seq // dil
        nblk = length // N_BACK
        for r in range(dil):
            sub_rows = slice(0, length) if dil == 1 else pl.ds(r, length, stride=dil)
            q = q_refs[g][sub_rows, :].astype(BF16)
            k = k_refs[g][sub_rows, :].astype(BF16)
            v = v_refs[g][sub_rows, :].astype(BF16)
            scores = []
            for i in range(nblk):
                qi = q[i * N_BACK:(i + 1) * N_BACK]
                keys = k[max(i - 1, 0) * N_BACK:(i + 1) * N_BACK]
                scores.append(lax.dot_general(qi, keys, nt, preferred_element_type=F32) * scale)
            probs, maxes = [], []
            for i in range(nblk):
                s = jnp.where(ok_first if i == 0 else ok_band, scores[i], neg)
                m = jnp.max(s, axis=-1, keepdims=True)
                probs.append(jnp.exp(s - m).astype(BF16))
                maxes.append(m)
            for i in range(nblk):
                vals = v[max(i - 1, 0) * N_BACK:(i + 1) * N_BACK]
                vext = jnp.concatenate([vals, jnp.concatenate([ones] * (vals.shape[0] // N_BACK), axis=0)], axis=1)
                acc = jnp.dot(probs[i], vext, preferred_element_type=F32)
                den = acc[:, HEAD_DIM:]
                start = r + i * N_BACK * dil
                out_rows = slice(start, start + N_BACK) if dil == 1 else pl.ds(start, N_BACK, stride=dil)
                og_ref.at[g][out_rows, :] = acc[:, :HEAD_DIM] / den
                lg_ref.at[g][out_rows, :] = maxes[i] + jnp.log(den)

    rb = 256
    for c in range(seq // rb):
        rows = slice(c * rb, (c + 1) * rb)
        ls = [lg_ref[g, rows, :] for g in range(ngroups)]
        m = functools.reduce(jnp.maximum, ls)
        es = [jnp.exp(l - m) for l in ls]
        tot = functools.reduce(lambda a, b: a + b, es)
        num = functools.reduce(lambda a, b: a + b, [es[g] * og_ref[g, rows, :] for g in range(ngroups)])
        o_ref[rows, :] = (num / tot).astype(o_ref.dtype)


def _attn_prompt(q, kvs):
    ngroups = len(DILATIONS)
    b, s, qc = q.shape
    hw = qc // ngroups
    nh = hw // HEAD_DIM
    assert all(s % (d * N_BACK) == 0 for d in DILATIONS) and s % 256 == 0
    blk = (None, s, HEAD_DIM)
    in_specs = ([pl.BlockSpec(blk, (lambda bi, h, g=g: (bi, 0, g * nh + h))) for g in range(ngroups)]
                + [pl.BlockSpec(blk, lambda bi, h: (bi, 0, h))] * ngroups
                + [pl.BlockSpec(blk, lambda bi, h: (bi, 0, nh + h))] * ngroups)
    est = (2 * (3 * ngroups * _nbytes((s, HEAD_DIM), F32) + _nbytes((s, HEAD_DIM), BF16))
           + 2 * ngroups * _nbytes((s, HEAD_DIM), F32) + 6 * _nbytes((s, HEAD_DIM), F32))
    return pl.pallas_call(
        functools.partial(_attn_prompt_kernel, ngroups=ngroups),
        out_shape=jax.ShapeDtypeStruct((b, s, hw), BF16),
        grid=(b, nh),
        in_specs=in_specs,
        out_specs=pl.BlockSpec(blk, lambda bi, h: (bi, 0, h)),
        scratch_shapes=[pltpu.VMEM((ngroups, s, HEAD_DIM), F32), pltpu.VMEM((ngroups, s, HEAD_DIM), F32)],
        compiler_params=_params(est, 2),
        name="attn_prompt",
    )(*([q] * ngroups), *kvs, *kvs)


def _attn_cached_kernel(*refs, dil, nheads, has_carry, final):
    if has_carry:
        q_ref, kn_ref, vn_ref, kb_ref, vb_ref, m_in, l_in, a_in = refs[:8]
        outs = refs[8:]
    else:
        q_ref, kn_ref, vn_ref, kb_ref, vb_ref = refs[:5]
        outs = refs[5:]
    if final:
        o_ref, m_sc, l_sc, a_sc = outs
    else:
        m_out, l_out, a_out, m_sc, l_sc, a_sc = outs
    c = pl.program_id(1)
    nc = pl.num_programs(1)
    wc = kb_ref.shape[0]
    w_total = wc * nc
    nq = q_ref.shape[0]
    hshift = nheads.bit_length() - 1
    scale = HEAD_DIM ** -0.5
    nt = (((1,), (1,)), ((), ()))
    neg = -jnp.inf
    q = q_ref[...].astype(BF16)

    def attend(k, v, ok):
        s = lax.dot_general(q, k.astype(BF16), nt, preferred_element_type=F32) * scale
        s = jnp.where(ok, s, neg)
        m_old = m_sc[...]
        m_new = jnp.maximum(m_old, jnp.max(s, axis=-1, keepdims=True))
        alpha = jnp.exp(m_old - m_new)
        p = jnp.exp(s - m_new)
        l_sc[...] = alpha * l_sc[...] + jnp.sum(p, axis=-1, keepdims=True)
        a_sc[...] = alpha * a_sc[...] + jnp.dot(p.astype(BF16), v.astype(BF16), preferred_element_type=F32)
        m_sc[...] = m_new

    @pl.when(c == 0)
    def _():
        if has_carry:
            m_sc[...] = m_in[...]
            l_sc[...] = l_in[...]
            a_sc[...] = a_in[...]
        else:
            m_sc[...] = jnp.full(m_sc.shape, neg, F32)
            l_sc[...] = jnp.zeros(l_sc.shape, F32)
            a_sc[...] = jnp.zeros(a_sc.shape, F32)
        qr = lax.broadcasted_iota(jnp.int32, (nq, nq), 0)
        kr = lax.broadcasted_iota(jnp.int32, (nq, nq), 1)
        dt = (qr >> hshift) - (kr >> hshift)
        ok = ((qr & (nheads - 1)) == (kr & (nheads - 1))) & (dt >= 0) & ((dt & (dil - 1)) == 0)
        attend(kn_ref[...], vn_ref[...], ok)

    qr = lax.broadcasted_iota(jnp.int32, (nq, wc * nheads), 0)
    kr = lax.broadcasted_iota(jnp.int32, (nq, wc * nheads), 1)
    dist = w_total + (qr >> hshift) - (c * wc + (kr >> hshift))
    ok = (((qr & (nheads - 1)) == (kr & (nheads - 1))) & ((dist & (dil - 1)) == 0)
          & (dist <= dil * N_BACK))
    attend(kb_ref[...].reshape(wc * nheads, HEAD_DIM), vb_ref[...].reshape(wc * nheads, HEAD_DIM), ok)

    @pl.when(c == nc - 1)
    def _():
        if final:
            o_ref[...] = a_sc[...] / l_sc[...]
        else:
            m_out[...] = m_sc[...]
            l_out[...] = l_sc[...]
            a_out[...] = a_sc[...]


def _attn_cached(qf, knf, vnf, buf, g, carry, *, final, wc):
    dil = DILATIONS[g]
    b, nq, hd = qf.shape
    w, nheads = buf.shape[1], buf.shape[3]
    assert w % wc == 0 and nheads & (nheads - 1) == 0 and dil & (dil - 1) == 0
    row = pl.BlockSpec((None, nq, hd), lambda bi, c: (bi, 0, 0))
    stat = pl.BlockSpec((None, nq, 1), lambda bi, c: (bi, 0, 0))
    in_specs = [row, row, row,
                pl.BlockSpec((None, wc, None, nheads, hd), lambda bi, c: (bi, c, 0, 0, 0)),
                pl.BlockSpec((None, wc, None, nheads, hd), lambda bi, c: (bi, c, 1, 0, 0))]
    args = [qf, knf, vnf, buf, buf]
    if carry is not None:
        in_specs += [stat, stat, row]
        args += list(carry)
    if final:
        out_shape = jax.ShapeDtypeStruct((b, nq, hd), F32)
        out_specs = row
    else:
        out_shape = (jax.ShapeDtypeStruct((b, nq, 1), F32), jax.ShapeDtypeStruct((b, nq, 1), F32),
                     jax.ShapeDtypeStruct((b, nq, hd), F32))
        out_specs = (stat, stat, row)
    est = 4 * _nbytes((wc, nheads, hd), F32) + 4 * _nbytes((nq, wc * nheads), F32) + 16 * _nbytes((nq, hd), F32)
    return pl.pallas_call(
        functools.partial(_attn_cached_kernel, dil=dil, nheads=nheads, has_carry=carry is not None, final=final),
        out_shape=out_shape,
        grid=(b, w // wc),
        in_specs=in_specs,
        out_specs=out_specs,
        scratch_shapes=[pltpu.VMEM((nq, 1), F32), pltpu.VMEM((nq, 1), F32), pltpu.VMEM((nq, hd), F32)],
        compiler_params=_params(est, 2),
        name=f"attn_cached_g{g}",
    )(*args)


def _shift_block(cur_ref, nxt_ref, new_ref, o_ref, last):
    rb, t_new = cur_ref.shape[0], new_ref.shape[0]
    o_ref[0:rb - t_new] = cur_ref[t_new:rb]
    o_ref[rb - t_new:rb] = jnp.where(last, new_ref[...], nxt_ref[...])


def _shift_kernel(cur_ref, nxt_ref, new_ref, o_ref):
    _shift_block(cur_ref, nxt_ref, new_ref, o_ref, pl.program_id(1) == pl.num_programs(1) - 1)


def _shift_job(buf, new, steps, *, rb):
    b, w, r, lanes = buf.shape
    t_new = new.shape[1]
    if w % rb or rb % t_new or rb <= t_new or b * (w // rb) > steps:
        return None
    nblk = w // rb
    nact = b * nblk
    last_blk = w // t_new - 1

    def where(s):
        s = jnp.minimum(s, nact - 1)
        return lax.div(s, nblk), lax.rem(s, nblk)

    def body(step, ins, outs):
        @pl.when(step < nact)
        def _():
            _shift_block(*ins, outs[0], lax.rem(step, nblk) == nblk - 1)

    blk = (None, rb, r, lanes)
    small = (None, t_new, r, lanes)
    return _SideJob(
        args=(buf, buf, new),
        in_blocks=((blk, lambda s: (*where(s), 0, 0)),
                   (small, lambda s: (where(s)[0],
                                      jnp.minimum((where(s)[1] + 1) * (rb // t_new), last_blk), 0, 0)),
                   (small, lambda s: (where(s)[0], 0, 0, 0))),
        out_shapes=(jax.ShapeDtypeStruct(buf.shape, F32),),
        out_blocks=((blk, lambda s: (*where(s), 0, 0)),),
        body=body,
        nbytes=4 * _nbytes((rb, r, lanes), F32) + 4 * _nbytes((t_new, r, lanes), F32))


def _shift_append(buf, new, *, rb):
    b, w, r, lanes = buf.shape
    t_new = new.shape[1]
    assert w % rb == 0 and rb % t_new == 0 and rb > t_new
    last_blk = w // t_new - 1
    est = 4 * _nbytes((rb, r, lanes), F32) + 6 * _nbytes((t_new, r, lanes), F32)
    return pl.pallas_call(
        _shift_kernel,
        out_shape=jax.ShapeDtypeStruct(buf.shape, F32),
        grid=(b, w // rb),
        in_specs=[
            pl.BlockSpec((None, rb, r, lanes), lambda bi, j: (bi, j, 0, 0)),
            pl.BlockSpec((None, t_new, r, lanes),
                         lambda bi, j: (bi, jnp.minimum((j + 1) * (rb // t_new), last_blk), 0, 0)),
            pl.BlockSpec((None, t_new, r, lanes), lambda bi, j: (bi, 0, 0, 0)),
        ],
        out_specs=pl.BlockSpec((None, rb, r, lanes), lambda bi, j: (bi, j, 0, 0)),
        compiler_params=_params(est, 2),
        name="kv_shift_append",
    )(buf, buf, new)


def _hgrn_kernel(*refs, heads, chunk, layer, has_state):
    if has_state:
        zq_ref, zf_ref, zv_ref, zg_ref, lb_ref, ng_ref, s0_ref = refs[:7]
        rest = refs[7:]
    else:
        zq_ref, zf_ref, zv_ref, zg_ref, lb_ref, ng_ref = refs[:6]
        s0_ref = None
        rest = refs[6:]
    y_ref, st_out_ref, st_ref = rest[:3]
    bc_ref = rest[3] if len(rest) > 3 else None
    tstep = pl.program_id(2)
    nsteps = pl.num_programs(2)
    tb = zq_ref.shape[0]

    @pl.when(tstep == 0)
    def _():
        for h in range(heads):
            if has_state:
                st_ref[h] = s0_ref[h].T
            else:
                st_ref[h] = jnp.zeros((HGRN_DK, HGRN_DK), F32)

    logits = lb_ref[...]
    ex = jnp.exp(logits - jnp.max(logits, axis=0, keepdims=True))
    sm = ex / jnp.sum(ex, axis=0, keepdims=True)
    lb = jnp.zeros_like(sm[0:1])
    for l in range(1, layer + 1):
        lb = lb + sm[l:l + 1]
    ng = ng_ref[...]

    sub = V7X_SUBLANES
    rblk = sub if chunk % sub == 0 else chunk
    nblk = chunk // rblk
    rowi = lax.broadcasted_iota(jnp.int32, (rblk, HGRN_DK), 0)
    rowi_all = lax.broadcasted_iota(jnp.int32, (chunk, zq_ref.shape[1]), 0)
    nt = (((1,), (1,)), ((), ()))
    tn = (((0,), (0,)), ((), ()))

    def body(ci, carry):
        rows = pl.ds(pl.multiple_of(ci * chunk, chunk), chunk)
        zq = zq_ref[rows, :]
        zf = zf_ref[rows, :]
        zv = zv_ref[rows, :]
        zg = zg_ref[rows, :]
        q_all = _silu(zq)
        f_all = lb + (1.0 - lb) * jax.nn.sigmoid(zf)
        lf_all = jnp.log(f_all)
        k_all = 1.0 - f_all
        og_all = ng * _silu(zg)
        if chunk % sub == 0:
            a_all = lf_all
            shift = 1
            while shift < chunk:
                a_all = a_all + jnp.where(rowi_all >= shift, pltpu.roll(a_all, shift, 0), 0.0)
                shift *= 2
        else:
            a_all = jnp.zeros_like(lf_all)
            for s in range(chunk):
                a_all = a_all + jnp.where(rowi_all >= s, lf_all[s:s + 1, :], 0.0)
        a_all = a_all * LOG2_E
        qe_all = q_all * jnp.exp2(a_all)
        a_last_all = a_all[chunk - 1:chunk, :]
        kd_all = k_all * jnp.exp2(a_last_all - a_all)
        dec_all = jnp.exp2(a_last_all)
        if bc_ref is not None:
            for h in range(heads):
                lanes = slice(h * HGRN_DK, (h + 1) * HGRN_DK)
                bc_ref[0, h] = a_all[:, lanes]
                bc_ref[1, h] = k_all[:, lanes]
                bc_ref[2, h] = zv[:, lanes]
        o_inter, scores, states_in = [], [], []
        for h in range(heads):
            lanes = slice(h * HGRN_DK, (h + 1) * HGRN_DK)
            q, k, a = q_all[:, lanes], k_all[:, lanes], a_all[:, lanes]
            st = st_ref[h]
            states_in.append(st)
            o_inter.append(lax.dot_general(qe_all[:, lanes].astype(BF16), st.astype(BF16), nt,
                                           preferred_element_type=F32))
            sc = None
            for j in range(nblk - 1):
                r0, r1 = j * rblk, (j + 1) * rblk
                b_j = a[r1 - 1:r1, :]
                qt = jnp.concatenate([jnp.zeros((r1, HGRN_DK), F32), q[r1:] * jnp.exp2(a[r1:] - b_j)], axis=0)
                kh = [k[r0:r1] * jnp.exp2(b_j - a[r0:r1])]
                if r0 > 0:
                    kh.insert(0, jnp.zeros((r0, HGRN_DK), F32))
                kh.append(jnp.zeros((chunk - r1, HGRN_DK), F32))
                sc_j = lax.dot_general(qt.astype(BF16), jnp.concatenate(kh, axis=0).astype(BF16), nt,
                                       preferred_element_type=F32)
                sc = sc_j if sc is None else sc + sc_j
            scores.append(sc)
        for h in range(heads):
            lanes = slice(h * HGRN_DK, (h + 1) * HGRN_DK)
            q, k, v, a = q_all[:, lanes], k_all[:, lanes], zv[:, lanes], a_all[:, lanes]
            st = states_in[h]
            o = o_inter[h]
            if scores[h] is not None:
                o = o + jnp.dot(scores[h].astype(BF16), v.astype(BF16), preferred_element_type=F32)
            ob = [None] * nblk
            for s in range(chunk):
                bi = s // rblk
                r0 = bi * rblk
                if bc_ref is not None:
                    a_s, k_s, v_s = (jnp.broadcast_to(bc_ref[c, h, s:s + 1, :], (rblk, HGRN_DK))
                                     for c in range(3))
                else:
                    a_s, k_s, v_s = a[s:s + 1, :], k[s:s + 1, :], v[s:s + 1, :]
                e = jnp.where(rowi >= s - r0, jnp.exp2(a[r0:r0 + rblk] - a_s), 0.0)
                wgt = (q[r0:r0 + rblk] * k_s) * e
                term = jnp.sum(wgt, axis=-1, keepdims=True) * v_s
                ob[bi] = term if ob[bi] is None else ob[bi] + term
            o = o + (ob[0] if nblk == 1 else jnp.concatenate(ob, axis=0))
            upd = lax.dot_general(v.astype(BF16), kd_all[:, lanes].astype(BF16), tn,
                                  preferred_element_type=F32)
            st_ref[h] = st * dec_all[:, lanes] + upd
            on = o * lax.rsqrt(jnp.mean(o * o, axis=-1, keepdims=True) + EPS)
            y_ref[rows, lanes] = (on * og_all[:, lanes]).astype(y_ref.dtype)
        return carry

    lax.fori_loop(0, tb // chunk, body, 0)

    @pl.when(tstep == nsteps - 1)
    def _():
        for h in range(heads):
            st_out_ref[h] = st_ref[h].T


def _hgrn(z, lb_logits, norm_g, s0, *, layer, heads, tb, out_dtype):
    b, t, d4 = z.shape
    d = d4 // 4
    nh = d // HGRN_DK
    nhg = nh // heads
    hb = heads * HGRN_DK
    chunk = math.gcd(t, HGRN_CHUNK)
    depth = lb_logits.shape[0]
    assert t % tb == 0 and tb % chunk == 0 and nh % heads == 0

    def zspec(part):
        return pl.BlockSpec((None, tb, hb), lambda bi, hg, i: (bi, i, part * nhg + hg))

    in_specs = [zspec(0), zspec(1), zspec(2), zspec(3),
                pl.BlockSpec((depth, hb), lambda bi, hg, i: (0, hg)),
                pl.BlockSpec((1, hb), lambda bi, hg, i: (0, hg))]
    args = [z, z, z, z, lb_logits, norm_g.reshape(1, d)]
    sspec = pl.BlockSpec((None, heads, HGRN_DK, HGRN_DK), lambda bi, hg, i: (bi, hg, 0, 0))
    if s0 is not None:
        in_specs.append(sspec)
        args.append(s0)
    est = (2 * (4 * _nbytes((tb, hb), F32) + _nbytes((tb, hb), out_dtype))
           + 5 * _nbytes((heads, HGRN_DK, HGRN_DK), F32))
    return pl.pallas_call(
        functools.partial(_hgrn_kernel, heads=heads, chunk=chunk, layer=layer, has_state=s0 is not None),
        out_shape=(jax.ShapeDtypeStruct((b, t, d), out_dtype),
                   jax.ShapeDtypeStruct((b, nh, HGRN_DK, HGRN_DK), F32)),
        grid=(b, nhg, t // tb),
        in_specs=in_specs,
        out_specs=(pl.BlockSpec((None, tb, hb), lambda bi, hg, i: (bi, i, hg)), sspec),
        scratch_shapes=([pltpu.VMEM((heads, HGRN_DK, HGRN_DK), F32)]
                        + ([pltpu.VMEM((3, heads, chunk, HGRN_DK), F32)] if chunk % V7X_SUBLANES == 0 else [])),
        compiler_params=_params(est, 3),
        name="hgrn2",
    )(*args)


def _forward(xp, xs, c_all, ada_w, ada_b, a_bufs, b_states, weights, *, nb, t_seq):
    (norm_mix_g, norm_ffn_g, a_w_qkv, a_w_o, b_w_in, b_lb_logits, b_norm_g, b_w_o,
     ffn_w_in, ffn_w_out, final_g) = weights
    bp, t, d = xp.shape
    ms = xs.shape[1]
    depth = norm_mix_g.shape[0]
    d_ff = ffn_w_out.shape[1]
    ngroups = len(DILATIONS)
    hw = a_w_o.shape[1]
    nheads = hw // HEAD_DIM
    bm = _pick(t, (1024, 512, 256, 128, 64, 32, 16, 8))
    bm_e = _pick(t, (256, 128, 64, 32, 16, 8))
    bn = 512
    ident = lambda j: j
    kv_p, kv_s, st_p, st_s = [], [], [], []
    w_dn_all = None
    bnf = 256
    nff = d_ff // bnf
    ffn_steps = nff * bp * (t // bm)
    ada_b3 = ada_b.reshape(depth, 1, 6 * d)
    ride = depth > 1 and _ada_job(c_all, ada_w, ada_b3, 1, ffn_steps) is not None
    mod0 = _ada_modulation(c_all, ada_w, ada_b3, nlayers=1 if ride else depth, bn=512)
    mod_rows = {l: mod0[l] for l in range(mod0.shape[0])}
    pending_shifts = []

    for layer in range(depth):
        i = layer // N_MIXERS
        mod_l = mod_rows[layer]
        mp = mod_l[:bp].reshape(bp, 1, 6 * d)
        msd = jnp.repeat(mod_l[bp:bp + nb], t_seq, axis=0).reshape(1, ms, 6 * d)
        hp = _norm_mod(xp, norm_mix_g[layer], mp, 1, 0, bm=bm_e)
        hs = _norm_mod(xs, norm_mix_g[layer], msd, 1, 0, bm=ms)
        if layer % N_MIXERS == 0:
            qp, qs = _linear(hp, a_w_qkv, layer=i, n=ngroups * hw, col_map=ident, out_dtype=F32,
                             bm=bm, bn=bn, xs=hs, name="a_q")
            half = hw // bn
            q5 = qs.reshape(nb, t_seq, ngroups, nheads, HEAD_DIM)
            flat = lambda a: a.reshape(nb, t_seq * nheads, HEAD_DIM)
            kvs = []
            carry = None
            for g in range(ngroups):
                def kv_cols(j, g=g):
                    return jnp.where(j < half, (ngroups + g) * half + j, (2 * ngroups + g) * half + (j - half))
                kvp, kvn = _linear(hp, a_w_qkv, layer=i, n=2 * hw, col_map=kv_cols, out_dtype=F32,
                                   bm=bm, bn=bn, xs=hs, name=f"a_kv{g}")
                kvs.append(kvp)
                keep = min(DILATIONS[g] * N_BACK, t)
                kv_keep = kvp if keep == t else kvp[:, t - keep:]
                kv_p.append(kv_keep.reshape(1, bp, keep, 2, nheads, HEAD_DIM))
                buf = a_bufs[g][i]
                w = buf.shape[1]
                kvn = kvn.reshape(nb, t_seq, 2, nheads, HEAD_DIM)
                carry = _attn_cached(flat(q5[:, :, g]), flat(kvn[:, :, 0]), flat(kvn[:, :, 1]), buf, g,
                                     carry, final=g == ngroups - 1, wc=min(w, 256))
                pending_shifts.append((buf.reshape(nb, w, 2 * nheads, HEAD_DIM),
                                       kvn.reshape(nb, t_seq, 2 * nheads, HEAD_DIM), len(kv_s)))
                kv_s.append(None)
            yp = _attn_prompt(qp, kvs)
            ys = carry.reshape(1, ms, hw).astype(BF16)
            w_o = a_w_o
        else:
            zp, zs = _linear(hp, b_w_in, layer=i, n=4 * d, col_map=ident, out_dtype=F32, bm=bm, bn=bn,
                             xs=hs, name="b_in")
            hg_heads = _pick(d // HGRN_DK, (8, 4, 2, 1))
            yp, s_p = _hgrn(zp, b_lb_logits, b_norm_g[i], None, layer=layer, heads=hg_heads,
                            tb=min(t, 512), out_dtype=BF16)
            ysb, s_s = _hgrn(zs.reshape(nb, t_seq, 4 * d), b_lb_logits, b_norm_g[i], b_states[i],
                             layer=layer, heads=hg_heads, tb=t_seq, out_dtype=F32)
            ys = ysb.reshape(1, ms, d).astype(BF16)
            st_p.append(s_p)
            st_s.append(s_s)
            w_o = b_w_o
        xp, xs = _linear(yp, w_o, layer=i, n=d, col_map=ident, out_dtype=F32, bm=bm, bn=bn,
                         mode="resid", res=xp, gate=mp, gate_chunk=2, xs=ys, res_s=xs, gate_s=msd,
                         name="mix_o")
        hp = _norm_mod(xp, norm_ffn_g[layer], mp, 4, 3, bm=bm_e)
        hs = _norm_mod(xs, norm_ffn_g[layer], msd, 4, 3, bm=ms)
        jobs, roles = [], []
        cast = _cast_job(ffn_w_out, layer, ffn_steps)
        if cast is not None:
            jobs.append(cast)
            roles.append(("w_dn", None))
        if ride and layer + 1 < depth:
            jobs.append(_ada_job(c_all, ada_w, ada_b3, layer + 1, ffn_steps))
            roles.append(("mod", layer + 1))
        else:
            for buf4, new4, slot in pending_shifts:
                job = next((jb for jb in (_shift_job(buf4, new4, ffn_steps, rb=rb)
                                          for rb in (64, 128, 256, 512, 1024) if rb <= buf4.shape[1])
                            if jb is not None), None)
                if job is not None:
                    jobs.append(job)
                    roles.append(("shift", slot))
            pending_shifts = [p for p in pending_shifts if ("shift", p[2]) not in roles]
        outs = _linear(hp, ffn_w_in, layer=layer, n=d_ff, col_map=ident, col_map2=lambda j: nff + j,
                       out_dtype=BF16, bm=bm, bn=bnf, mode="swiglu", xs=hs, side_jobs=jobs, name="ffn_in")
        actp, acts = outs[0], outs[1]
        w_dn = None
        for (role, key), val in zip(roles, outs[2:]):
            if role == "w_dn":
                w_dn, dn_layer = val[None], 0
            elif role == "mod":
                mod_rows[key] = val
            else:
                kv_s[key] = val.reshape(1, nb, val.shape[1], 2, nheads, HEAD_DIM)
        if w_dn is None:
            if w_dn_all is None:
                w_dn_all = _to_bf16(ffn_w_out, kb=_pick(d_ff, (d_ff // 16, d_ff // 8, d_ff // 4, d_ff // 2)))
            w_dn, dn_layer = w_dn_all, layer
        half_ff = d_ff // 2
        bk = half_ff if (d_ff % 2 == 0 and half_ff % V7X_LANES == 0) else d_ff
        xp, xs = _linear(actp, w_dn, layer=dn_layer, n=d, col_map=ident, out_dtype=F32,
                         bm=bm, bn=bn, bk=bk, mode="resid", res=xp, gate=mp, gate_chunk=5,
                         xs=acts, res_s=xs, gate_s=msd, name="ffn_out")
    for buf4, new4, slot in pending_shifts:
        w = buf4.shape[1]
        kv_s[slot] = _shift_append(buf4, new4, rb=min(w, 256)).reshape(1, nb, w, 2, nheads, HEAD_DIM)
    yp = _final_norm(xp, final_g, bm=bm_e)
    ys = _final_norm(xs, final_g, bm=ms)
    return yp, ys, kv_p, kv_s, st_p, st_s


def kernel(x_prompt, x_sample, state_a_kv_w128, state_a_kv_w512, state_a_kv_w2048, state_b_rec,
           c_prompt, c_sample, ada_w, ada_b, norm_mix_g, norm_ffn_g, a_w_qkv, a_w_o,
           b_w_in, b_lb_logits, b_norm_g, b_w_o, ffn_w_in, ffn_w_out, final_g):
    bp, sp, d = x_prompt.shape
    bs, ts, _ = x_sample.shape
    depth = ada_w.shape[0]
    weights = (norm_mix_g, norm_ffn_g, a_w_qkv, a_w_o, b_w_in, b_lb_logits, b_norm_g, b_w_o,
               ffn_w_in, ffn_w_out, final_g)

    rpad = -(bp + bs) % V7X_SUBLANES
    c_all = jnp.pad(jnp.concatenate([c_prompt, c_sample], axis=0), ((0, rpad), (0, 0)))
    y_p, y_s, kv_p, kv_s, st_p, st_s = _forward(
        x_prompt, x_sample.reshape(1, bs * ts, d), c_all, ada_w, ada_b,
        (state_a_kv_w128, state_a_kv_w512, state_a_kv_w2048), state_b_rec, weights, nb=bs, t_seq=ts)
    return (y_p, y_s.reshape(bs, ts, d), kv_p[0], kv_s[0], kv_p[1], kv_s[1], kv_p[2], kv_s[2],
            jnp.stack(st_p, axis=0), jnp.stack(st_s, axis=0))
```

```python
import functools
import math
from typing import Callable, NamedTuple

import jax
import jax.numpy as jnp
from jax import lax
from jax.experimental import pallas as pl
from jax.experimental.pallas import tpu as pltpu

F32 = jnp.float32
BF16 = jnp.bfloat16

EPS = 1e-6
DILATIONS = (1, 4, 16)
N_BACK = 128
HEAD_DIM = 128
HGRN_DK = 128
HGRN_CHUNK = 16
LOG2_E = 1.4426950408889634
N_MIXERS = 2
V7X_LANES = 128
V7X_SUBLANES = 8
V7X_SCOPED_VMEM_BYTES = 60000 * 1024


def _params(nbytes, n_axes):
    limit = int(min(V7X_SCOPED_VMEM_BYTES, max(2 * nbytes, 16 * 1024 * 1024)))
    return pltpu.CompilerParams(dimension_semantics=("arbitrary",) * n_axes, vmem_limit_bytes=limit)


def _nbytes(shape, dtype):
    return math.prod(shape) * jnp.dtype(dtype).itemsize


def _silu(x):
    return x * jax.nn.sigmoid(x)


def _pick(total, prefs):
    for p in prefs:
        if total % p == 0:
            return p
    return total


class _SideJob(NamedTuple):
    args: tuple
    in_blocks: tuple
    out_shapes: tuple
    out_blocks: tuple
    body: Callable
    nbytes: int

    @property
    def n_in(self):
        return len(self.in_blocks)

    @property
    def n_out(self):
        return len(self.out_blocks)


def _ada_kernel(c_ref, w_ref, b_ref, o_ref):
    s = _silu(c_ref[...])
    acc = jnp.dot(s.astype(BF16), w_ref[...].astype(BF16), preferred_element_type=F32)
    o_ref[...] = acc + b_ref[...]


def _ada_job(c_all, ada_w, ada_b3, layer, steps):
    _, d, n = ada_w.shape
    r = c_all.shape[0]
    cw = next((c for c in range(V7X_LANES, 4 * V7X_LANES + 1, V7X_LANES) if n % c == 0 and n // c <= steps), None)
    if cw is None:
        return None
    nact = n // cw
    col = lambda s: jnp.minimum(s, nact - 1)

    def body(step, ins, outs):
        @pl.when(step < nact)
        def _():
            _ada_kernel(*ins, outs[0])

    return _SideJob(
        args=(c_all, ada_w, ada_b3),
        in_blocks=(((r, d), lambda s: (0, 0)),
                   ((None, d, cw), lambda s: (layer, 0, col(s))),
                   ((None, 1, cw), lambda s: (layer, 0, col(s)))),
        out_shapes=(jax.ShapeDtypeStruct((r, n), F32),),
        out_blocks=(((r, cw), lambda s: (0, col(s))),),
        body=body,
        nbytes=2 * (_nbytes((d, cw), F32) + _nbytes((r, cw), F32)) + _nbytes((d, cw), BF16) + _nbytes((r, d), F32))


def _ada_modulation(c_all, ada_w, ada_b3, *, nlayers, bn):
    _, d, n = ada_w.shape
    depth = nlayers
    r = c_all.shape[0]
    est = 2 * (_nbytes((d, bn), F32) + _nbytes((r, bn), F32)) + _nbytes((d, bn), BF16) + _nbytes((r, d), F32)
    return pl.pallas_call(
        _ada_kernel,
        out_shape=jax.ShapeDtypeStruct((depth, r, n), F32),
        grid=(depth, n // bn),
        in_specs=[
            pl.BlockSpec((r, d), lambda l, j: (0, 0)),
            pl.BlockSpec((None, d, bn), lambda l, j: (l, 0, j)),
            pl.BlockSpec((None, 1, bn), lambda l, j: (l, 0, j)),
        ],
        out_specs=pl.BlockSpec((None, r, bn), lambda l, j: (l, 0, j)),
        compiler_params=_params(est, 2),
        name="ada_modulation",
    )(c_all, ada_w, ada_b3)


def _norm_mod_kernel(x_ref, g_ref, sc_ref, sh_ref, o_ref):
    x = x_ref[...]
    y = x * lax.rsqrt(jnp.mean(x * x, axis=-1, keepdims=True) + EPS)
    h = (y * g_ref[...]) * (1.0 + sc_ref[...]) + sh_ref[...]
    o_ref[...] = h.astype(o_ref.dtype)


def _norm_mod(x, g, mod, sc_chunk, sh_chunk, *, bm):
    bn_, t, d = x.shape
    tg = mod.shape[1]
    mg = 1 if tg == 1 else bm

    def mod_map(chunk):
        return lambda b, i: (b, 0 if tg == 1 else i, chunk)

    est = 2 * (_nbytes((bm, d), F32) + _nbytes((bm, d), BF16)) + 2 * _nbytes((bm, d), F32)
    return pl.pallas_call(
        _norm_mod_kernel,
        out_shape=jax.ShapeDtypeStruct((bn_, t, d), BF16),
        grid=(bn_, t // bm),
        in_specs=[
            pl.BlockSpec((None, bm, d), lambda b, i: (b, i, 0)),
            pl.BlockSpec((1, d), lambda b, i: (0, 0)),
            pl.BlockSpec((None, mg, d), mod_map(sc_chunk)),
            pl.BlockSpec((None, mg, d), mod_map(sh_chunk)),
        ],
        out_specs=pl.BlockSpec((None, bm, d), lambda b, i: (b, i, 0)),
        compiler_params=_params(est, 2),
        name="norm_mod",
    )(x, g.reshape(1, d), mod, mod)


def _final_norm_kernel(x_ref, g_ref, o_ref):
    x = x_ref[...]
    y = x * lax.rsqrt(jnp.mean(x * x, axis=-1, keepdims=True) + EPS)
    o_ref[...] = y * g_ref[...]


def _final_norm(x, g, *, bm):
    bn_, t, d = x.shape
    est = 6 * _nbytes((bm, d), F32)
    return pl.pallas_call(
        _final_norm_kernel,
        out_shape=jax.ShapeDtypeStruct((bn_, t, d), F32),
        grid=(bn_, t // bm),
        in_specs=[
            pl.BlockSpec((None, bm, d), lambda b, i: (b, i, 0)),
            pl.BlockSpec((1, d), lambda b, i: (0, 0)),
        ],
        out_specs=pl.BlockSpec((None, bm, d), lambda b, i: (b, i, 0)),
        compiler_params=_params(est, 2),
        name="final_norm",
    )(x, g.reshape(1, d))


def _to_bf16_kernel(w_ref, o_ref):
    o_ref[...] = w_ref[...].astype(BF16)


def _to_bf16(w, *, kb):
    l, k, n = w.shape
    assert k % kb == 0
    est = 2 * (_nbytes((kb, n), F32) + _nbytes((kb, n), BF16))
    spec = pl.BlockSpec((None, kb, n), lambda li, i: (li, i, 0))
    return pl.pallas_call(
        _to_bf16_kernel,
        out_shape=jax.ShapeDtypeStruct(w.shape, BF16),
        grid=(l, k // kb),
        in_specs=[spec],
        out_specs=spec,
        compiler_params=_params(est, 2),
        name="weight_to_bf16",
    )(w)


def _cast_job(w, layer, steps):
    kc, nc = w.shape[1:]
    if kc % steps or (kc // steps) % 16:
        return None
    slab = kc // steps

    def body(step, ins, outs):
        _to_bf16_kernel(ins[0], outs[0])

    return _SideJob(
        args=(w,),
        in_blocks=(((None, slab, nc), lambda s: (layer, s, 0)),),
        out_shapes=(jax.ShapeDtypeStruct((kc, nc), BF16),),
        out_blocks=(((slab, nc), lambda s: (s, 0)),),
        body=body,
        nbytes=2 * (_nbytes((slab, nc), F32) + _nbytes((slab, nc), BF16)))


def _linear_kernel(*refs, nk, mode, extra, side_jobs):
    refs = list(refs)
    x_ref, w_ref = refs[0], refs[1]
    pos = 2
    w2_ref = res_ref = gate_ref = xs_ref = ress_ref = gates_ref = os_ref = None
    if mode == "swiglu":
        w2_ref = refs[pos]
        pos += 1
    if mode == "resid":
        res_ref, gate_ref = refs[pos], refs[pos + 1]
        pos += 2
    if extra:
        xs_ref = refs[pos]
        pos += 1
        if mode == "resid":
            ress_ref, gates_ref = refs[pos], refs[pos + 1]
            pos += 2
    job_ins = []
    for job in side_jobs:
        job_ins.append(refs[pos:pos + job.n_in])
        pos += job.n_in
    o_ref = refs[pos]
    pos += 1
    if extra:
        os_ref = refs[pos]
        pos += 1
    step = (pl.program_id(0) * pl.num_programs(1) + pl.program_id(1)) * pl.num_programs(2) + pl.program_id(2)
    for job, ins in zip(side_jobs, job_ins):
        job.body(step, ins, refs[pos:pos + job.n_out])
        pos += job.n_out
    scratch = refs[pos:]

    def finalize(out_ref, r_ref, g_ref, acc, acc2=None):
        if mode == "swiglu":
            out_ref[...] = (_silu(acc) * acc2).astype(out_ref.dtype)
        elif mode == "resid":
            out_ref[...] = (r_ref[...] + g_ref[...] * acc).astype(out_ref.dtype)
        else:
            out_ref[...] = acc.astype(out_ref.dtype)

    cast = w_ref.dtype != BF16
    first = jnp.logical_and(pl.program_id(1) == 0, pl.program_id(2) == 0)
    if nk == 1:
        wb_ref, wb2_ref = w_ref, w2_ref
        if cast:
            wb_ref = scratch[0]
            wb2_ref = scratch[1] if mode == "swiglu" else None

        def first_step():
            if cast:
                wb_ref[...] = w_ref[...].astype(BF16)
                if mode == "swiglu":
                    wb2_ref[...] = w2_ref[...].astype(BF16)
            if extra:
                xs = xs_ref[...]
                accs = jnp.dot(xs, wb_ref[...], preferred_element_type=F32)
                accs2 = None
                if mode == "swiglu":
                    accs2 = jnp.dot(xs, wb2_ref[...], preferred_element_type=F32)
                finalize(os_ref, ress_ref, gates_ref, accs, accs2)

        if cast or extra:
            pl.when(first)(first_step)

        x = x_ref[...]
        acc = jnp.dot(x, wb_ref[...], preferred_element_type=F32)
        acc2 = None
        if mode == "swiglu":
            acc2 = jnp.dot(x, wb2_ref[...], preferred_element_type=F32)
        finalize(o_ref, res_ref, gate_ref, acc, acc2)
    else:
        acc_ref = scratch[0]
        k = pl.program_id(3)
        w = w_ref[...].astype(BF16) if cast else w_ref[...]
        part = jnp.dot(x_ref[...], w, preferred_element_type=F32)

        @pl.when(k == 0)
        def _():
            acc_ref[...] = part

        @pl.when(k > 0)
        def _():
            acc_ref[...] += part

        @pl.when(k == nk - 1)
        def _():
            finalize(o_ref, res_ref, gate_ref, acc_ref[...])

        if extra:
            accs_ref = scratch[1]

            @pl.when(first)
            def _():
                parts = jnp.dot(xs_ref[...], w, preferred_element_type=F32)

                @pl.when(k == 0)
                def _():
                    accs_ref[...] = parts

                @pl.when(k > 0)
                def _():
                    accs_ref[...] += parts

                @pl.when(k == nk - 1)
                def _():
                    finalize(os_ref, ress_ref, gates_ref, accs_ref[...])


def _linear(x, w, *, layer, n, col_map, out_dtype, bm, bn, bk=None, mode="plain",
            col_map2=None, res=None, gate=None, gate_chunk=0, xs=None, res_s=None, gate_s=None,
            side_jobs=(), name="linear"):
    bn_, t, kdim = x.shape
    bk = kdim if bk is None else bk
    nk = kdim // bk
    assert kdim % bk == 0 and t % bm == 0 and n % bn == 0
    assert mode != "swiglu" or nk == 1
    grid = (n // bn, bn_, t // bm, nk)

    in_specs = [
        pl.BlockSpec((None, bm, bk), lambda j, b, i, k: (b, i, k)),
        pl.BlockSpec((None, bk, bn), lambda j, b, i, k: (layer, k, col_map(j))),
    ]
    args = [x, w]
    est = 2 * (_nbytes((bm, bk), BF16) + _nbytes((bk, bn), w.dtype) + _nbytes((bm, bn), out_dtype))
    est += _nbytes((bk, bn), BF16) + 2 * _nbytes((bm, bn), F32)
    if mode == "swiglu":
        in_specs.append(pl.BlockSpec((None, bk, bn), lambda j, b, i, k: (layer, k, col_map2(j))))
        args.append(w)
        est += 2 * _nbytes((bk, bn), F32) + _nbytes((bk, bn), BF16) + _nbytes((bm, bn), F32)
    if mode == "resid":
        tg = gate.shape[1]
        mg = 1 if tg == 1 else bm
        nb = n // bn
        in_specs.append(pl.BlockSpec((None, bm, bn), lambda j, b, i, k: (b, i, j)))
        in_specs.append(pl.BlockSpec(
            (None, mg, bn), lambda j, b, i, k: (b, 0 if tg == 1 else i, gate_chunk * nb + j)))
        args += [res, gate]
        est += 4 * _nbytes((bm, bn), F32)

    out_shape = jax.ShapeDtypeStruct((bn_, t, n), out_dtype)
    out_specs = pl.BlockSpec((None, bm, bn), lambda j, b, i, k: (b, i, j))
    ms = 0
    if xs is not None:
        ms = xs.shape[1]
        in_specs.append(pl.BlockSpec((None, ms, bk), lambda j, b, i, k: (0, 0, k)))
        args.append(xs)
        if mode == "resid":
            nb = n // bn
            in_specs.append(pl.BlockSpec((None, ms, bn), lambda j, b, i, k: (0, 0, j)))
            in_specs.append(pl.BlockSpec((None, ms, bn), lambda j, b, i, k: (0, 0, gate_chunk * nb + j)))
            args += [res_s, gate_s]
        out_shape = (out_shape, jax.ShapeDtypeStruct((1, ms, n), out_dtype))
        out_specs = (out_specs, pl.BlockSpec((None, ms, bn), lambda j, b, i, k: (0, 0, j)))
        est += 2 * _nbytes((ms, bk), BF16) + 8 * _nbytes((ms, bn), F32)

    if side_jobs:
        assert nk == 1
        nt_ = t // bm
        out_shape = tuple(out_shape) if isinstance(out_shape, tuple) else (out_shape,)
        out_specs = tuple(out_specs) if isinstance(out_specs, tuple) else (out_specs,)

        def at_step(index_of_step):
            return lambda j, b, i, k: index_of_step((j * bn_ + b) * nt_ + i)

        for job in side_jobs:
            in_specs += [pl.BlockSpec(blk, at_step(f)) for blk, f in job.in_blocks]
            args += list(job.args)
            out_shape += tuple(job.out_shapes)
            out_specs += tuple(pl.BlockSpec(blk, at_step(f)) for blk, f in job.out_blocks)
            est += job.nbytes

    if nk > 1:
        scratch = [pltpu.VMEM((bm, bn), F32)] + ([pltpu.VMEM((ms, bn), F32)] if ms else [])
    elif w.dtype != BF16:
        scratch = [pltpu.VMEM((bk, bn), BF16)] * (2 if mode == "swiglu" else 1)
    else:
        scratch = []

    return pl.pallas_call(
        functools.partial(_linear_kernel, nk=nk, mode=mode, extra=xs is not None,
                          side_jobs=tuple(side_jobs)),
        out_shape=out_shape,
        grid=grid,
        in_specs=in_specs,
        out_specs=out_specs,
        scratch_shapes=scratch,
        compiler_params=_params(est, 4),
        name=name,
    )(*args)


def _attn_prompt_kernel(*refs, ngroups):
    q_refs, k_refs, v_refs = refs[:ngroups], refs[ngroups:2 * ngroups], refs[2 * ngroups:3 * ngroups]
    o_ref, og_ref, lg_ref = refs[3 * ngroups:]
    seq = o_ref.shape[0]
    scale = HEAD_DIM ** -0.5
    nt = (((1,), (1,)), ((), ()))
    neg = -jnp.inf
    row1 = lax.broadcasted_iota(jnp.int32, (N_BACK, N_BACK), 0)
    col1 = lax.broadcasted_iota(jnp.int32, (N_BACK, N_BACK), 1)
    row2 = lax.broadcasted_iota(jnp.int32, (N_BACK, 2 * N_BACK), 0)
    col2 = lax.broadcasted_iota(jnp.int32, (N_BACK, 2 * N_BACK), 1)
    ok_first = col1 <= row1
    ok_band = jnp.logical_or(jnp.logical_and(col2 < N_BACK, col2 >= row2),
                             jnp.logical_and(col2 >= N_BACK, col2 - N_BACK <= row2))
    ones = jnp.ones((N_BACK, HEAD_DIM), BF16)

    for g in range(ngroups):
        dil = DILATIONS[g]
        length = seq // dil
        nblk = length // N_BACK
        for r in range(dil):
            sub_rows = slice(0, length) if dil == 1 else pl.ds(r, length, stride=dil)
            q = q_refs[g][sub_rows, :].astype(BF16)
            k = k_refs[g][sub_rows, :].astype(BF16)
            v = v_refs[g][sub_rows, :].astype(BF16)
            scores = []
            for i in range(nblk):
                qi = q[i * N_BACK:(i + 1) * N_BACK]
                keys = k[max(i - 1, 0) * N_BACK:(i + 1) * N_BACK]
                scores.append(lax.dot_general(qi, keys, nt, preferred_element_type=F32) * scale)
            probs, maxes = [], []
            for i in range(nblk):
                s = jnp.where(ok_first if i == 0 else ok_band, scores[i], neg)
                m = jnp.max(s, axis=-1, keepdims=True)
                probs.append(jnp.exp(s - m).astype(BF16))
                maxes.append(m)
            for i in range(nblk):
                vals = v[max(i - 1, 0) * N_BACK:(i + 1) * N_BACK]
                vext = jnp.concatenate([vals, jnp.concatenate([ones] * (vals.shape[0] // N_BACK), axis=0)], axis=1)
                acc = jnp.dot(probs[i], vext, preferred_element_type=F32)
                den = acc[:, HEAD_DIM:]
                start = r + i * N_BACK * dil
                out_rows = slice(start, start + N_BACK) if dil == 1 else pl.ds(start, N_BACK, stride=dil)
                og_ref.at[g][out_rows, :] = acc[:, :HEAD_DIM] / den
                lg_ref.at[g][out_rows, :] = maxes[i] + jnp.log(den)

    rb = 256
    for c in range(seq // rb):
        rows = slice(c * rb, (c + 1) * rb)
        ls = [lg_ref[g, rows, :] for g in range(ngroups)]
        m = functools.reduce(jnp.maximum, ls)
        es = [jnp.exp(l - m) for l in ls]
        tot = functools.reduce(lambda a, b: a + b, es)
        num = functools.reduce(lambda a, b: a + b, [es[g] * og_ref[g, rows, :] for g in range(ngroups)])
        o_ref[rows, :] = (num / tot).astype(o_ref.dtype)


def _attn_prompt(q, kvs):
    ngroups = len(DILATIONS)
    b, s, qc = q.shape
    hw = qc // ngroups
    nh = hw // HEAD_DIM
    assert all(s % (d * N_BACK) == 0 for d in DILATIONS) and s % 256 == 0
    blk = (None, s, HEAD_DIM)
    in_specs = ([pl.BlockSpec(blk, (lambda bi, h, g=g: (bi, 0, g * nh + h))) for g in range(ngroups)]
                + [pl.BlockSpec(blk, lambda bi, h: (bi, 0, h))] * ngroups
                + [pl.BlockSpec(blk, lambda bi, h: (bi, 0, nh + h))] * ngroups)
    est = (2 * (3 * ngroups * _nbytes((s, HEAD_DIM), F32) + _nbytes((s, HEAD_DIM), BF16))
           + 2 * ngroups * _nbytes((s, HEAD_DIM), F32) + 6 * _nbytes((s, HEAD_DIM), F32))
    return pl.pallas_call(
        functools.partial(_attn_prompt_kernel, ngroups=ngroups),
        out_shape=jax.ShapeDtypeStruct((b, s, hw), BF16),
        grid=(b, nh),
        in_specs=in_specs,
        out_specs=pl.BlockSpec(blk, lambda bi, h: (bi, 0, h)),
        scratch_shapes=[pltpu.VMEM((ngroups, s, HEAD_DIM), F32), pltpu.VMEM((ngroups, s, HEAD_DIM), F32)],
        compiler_params=_params(est, 2),
        name="attn_prompt",
    )(*([q] * ngroups), *kvs, *kvs)


def _attn_cached_kernel(*refs, dil, nheads, has_carry, final):
    if has_carry:
        q_ref, kn_ref, vn_ref, kb_ref, vb_ref, m_in, l_in, a_in = refs[:8]
        outs = refs[8:]
    else:
        q_ref, kn_ref, vn_ref, kb_ref, vb_ref = refs[:5]
        outs = refs[5:]
    if final:
        o_ref, m_sc, l_sc, a_sc = outs
    else:
        m_out, l_out, a_out, m_sc, l_sc, a_sc = outs
    c = pl.program_id(1)
    nc = pl.num_programs(1)
    restricted = len(kb_ref.shape) == 4
    res = kb_ref.shape[1] if restricted else 1
    period = dil if restricted else 1
    wc = kb_ref.shape[0] * res
    w_total = kb_ref.shape[0] * period * nc
    nq = q_ref.shape[0]
    hshift = nheads.bit_length() - 1
    rshift = res.bit_length() - 1
    scale = HEAD_DIM ** -0.5
    nt = (((1,), (1,)), ((), ()))
    neg = -jnp.inf
    q = q_ref[...].astype(BF16)

    def attend(k, v, ok):
        s = lax.dot_general(q, k.astype(BF16), nt, preferred_element_type=F32) * scale
        s = jnp.where(ok, s, neg)
        m_old = m_sc[...]
        m_new = jnp.maximum(m_old, jnp.max(s, axis=-1, keepdims=True))
        alpha = jnp.exp(m_old - m_new)
        p = jnp.exp(s - m_new)
        l_sc[...] = alpha * l_sc[...] + jnp.sum(p, axis=-1, keepdims=True)
        a_sc[...] = alpha * a_sc[...] + jnp.dot(p.astype(BF16), v.astype(BF16), preferred_element_type=F32)
        m_sc[...] = m_new

    @pl.when(c == 0)
    def _():
        if has_carry:
            m_sc[...] = m_in[...]
            l_sc[...] = l_in[...]
            a_sc[...] = a_in[...]
        else:
            m_sc[...] = jnp.full(m_sc.shape, neg, F32)
            l_sc[...] = jnp.zeros(l_sc.shape, F32)
            a_sc[...] = jnp.zeros(a_sc.shape, F32)
        qr = lax.broadcasted_iota(jnp.int32, (nq, nq), 0)
        kr = lax.broadcasted_iota(jnp.int32, (nq, nq), 1)
        dt = (qr >> hshift) - (kr >> hshift)
        ok = ((qr & (nheads - 1)) == (kr & (nheads - 1))) & (dt >= 0) & ((dt & (dil - 1)) == 0)
        attend(kn_ref[...], vn_ref[...], ok)

    qr = lax.broadcasted_iota(jnp.int32, (nq, wc * nheads), 0)
    kr = lax.broadcasted_iota(jnp.int32, (nq, wc * nheads), 1)
    held = kr >> hshift
    w_row = (c * (wc // res) + (held >> rshift)) * period + (held & (res - 1))
    dist = w_total + (qr >> hshift) - w_row
    ok = (((qr & (nheads - 1)) == (kr & (nheads - 1))) & ((dist & (dil - 1)) == 0)
          & (dist <= dil * N_BACK))
    attend(kb_ref[...].reshape(wc * nheads, HEAD_DIM), vb_ref[...].reshape(wc * nheads, HEAD_DIM), ok)

    @pl.when(c == nc - 1)
    def _():
        if final:
            o_ref[...] = a_sc[...] / l_sc[...]
        else:
            m_out[...] = m_sc[...]
            l_out[...] = l_sc[...]
            a_out[...] = a_sc[...]


def _attn_cached(qf, knf, vnf, buf, g, carry, *, final, wc):
    dil = DILATIONS[g]
    b, nq, hd = qf.shape
    w, nheads = buf.shape[1], buf.shape[3]
    t_new = nq // nheads
    assert w % wc == 0 and nheads & (nheads - 1) == 0 and dil & (dil - 1) == 0
    row = pl.BlockSpec((None, nq, hd), lambda bi, c: (bi, 0, 0))
    stat = pl.BlockSpec((None, nq, 1), lambda bi, c: (bi, 0, 0))
    if dil > t_new and t_new & (t_new - 1) == 0 and w % dil == 0 and wc % dil == 0:
        wc = min(w, wc * dil // t_new)
        buf = buf.reshape(b, w // dil, dil, 2, nheads, hd)
        kv_blk = (None, wc // dil, t_new, None, nheads, hd)
        kv_specs = [pl.BlockSpec(kv_blk, lambda bi, c: (bi, c, 0, 0, 0, 0)),
                    pl.BlockSpec(kv_blk, lambda bi, c: (bi, c, 0, 1, 0, 0))]
        held = wc // dil * t_new
    else:
        kv_blk = (None, wc, None, nheads, hd)
        kv_specs = [pl.BlockSpec(kv_blk, lambda bi, c: (bi, c, 0, 0, 0)),
                    pl.BlockSpec(kv_blk, lambda bi, c: (bi, c, 1, 0, 0))]
        held = wc
    in_specs = [row, row, row] + kv_specs
    args = [qf, knf, vnf, buf, buf]
    if carry is not None:
        in_specs += [stat, stat, row]
        args += list(carry)
    if final:
        out_shape = jax.ShapeDtypeStruct((b, nq, hd), F32)
        out_specs = row
    else:
        out_shape = (jax.ShapeDtypeStruct((b, nq, 1), F32), jax.ShapeDtypeStruct((b, nq, 1), F32),
                     jax.ShapeDtypeStruct((b, nq, hd), F32))
        out_specs = (stat, stat, row)
    est = (4 * _nbytes((held, nheads, hd), F32) + 4 * _nbytes((nq, held * nheads), F32)
           + 16 * _nbytes((nq, hd), F32))
    return pl.pallas_call(
        functools.partial(_attn_cached_kernel, dil=dil, nheads=nheads, has_carry=carry is not None, final=final),
        out_shape=out_shape,
        grid=(b, w // wc),
        in_specs=in_specs,
        out_specs=out_specs,
        scratch_shapes=[pltpu.VMEM((nq, 1), F32), pltpu.VMEM((nq, 1), F32), pltpu.VMEM((nq, hd), F32)],
        compiler_params=_params(est, 2),
        name=f"attn_cached_g{g}",
    )(*args)


def _shift_block(cur_ref, nxt_ref, new_ref, o_ref, last):
    rb, t_new = cur_ref.shape[0], new_ref.shape[0]
    o_ref[0:rb - t_new] = cur_ref[t_new:rb]
    o_ref[rb - t_new:rb] = jnp.where(last, new_ref[...], nxt_ref[...])


def _shift_kernel(cur_ref, nxt_ref, new_ref, o_ref):
    _shift_block(cur_ref, nxt_ref, new_ref, o_ref, pl.program_id(1) == pl.num_programs(1) - 1)


def _shift_job(buf, new, steps, *, rb):
    b, w, r, lanes = buf.shape
    t_new = new.shape[1]
    if w % rb or rb % t_new or rb <= t_new or b * (w // rb) > steps:
        return None
    nblk = w // rb
    nact = b * nblk
    last_blk = w // t_new - 1

    def where(s):
        s = jnp.minimum(s, nact - 1)
        return lax.div(s, nblk), lax.rem(s, nblk)

    def body(step, ins, outs):
        @pl.when(step < nact)
        def _():
            _shift_block(*ins, outs[0], lax.rem(step, nblk) == nblk - 1)

    blk = (None, rb, r, lanes)
    small = (None, t_new, r, lanes)
    return _SideJob(
        args=(buf, buf, new),
        in_blocks=((blk, lambda s: (*where(s), 0, 0)),
                   (small, lambda s: (where(s)[0],
                                      jnp.minimum((where(s)[1] + 1) * (rb // t_new), last_blk), 0, 0)),
                   (small, lambda s: (where(s)[0], 0, 0, 0))),
        out_shapes=(jax.ShapeDtypeStruct(buf.shape, F32),),
        out_blocks=((blk, lambda s: (*where(s), 0, 0)),),
        body=body,
        nbytes=4 * _nbytes((rb, r, lanes), F32) + 4 * _nbytes((t_new, r, lanes), F32))


def _shift_append(buf, new, *, rb):
    b, w, r, lanes = buf.shape
    t_new = new.shape[1]
    assert w % rb == 0 and rb % t_new == 0 and rb > t_new
    last_blk = w // t_new - 1
    est = 4 * _nbytes((rb, r, lanes), F32) + 6 * _nbytes((t_new, r, lanes), F32)
    return pl.pallas_call(
        _shift_kernel,
        out_shape=jax.ShapeDtypeStruct(buf.shape, F32),
        grid=(b, w // rb),
        in_specs=[
            pl.BlockSpec((None, rb, r, lanes), lambda bi, j: (bi, j, 0, 0)),
            pl.BlockSpec((None, t_new, r, lanes),
                         lambda bi, j: (bi, jnp.minimum((j + 1) * (rb // t_new), last_blk), 0, 0)),
            pl.BlockSpec((None, t_new, r, lanes), lambda bi, j: (bi, 0, 0, 0)),
        ],
        out_specs=pl.BlockSpec((None, rb, r, lanes), lambda bi, j: (bi, j, 0, 0)),
        compiler_params=_params(est, 2),
        name="kv_shift_append",
    )(buf, buf, new)


def _hgrn_kernel(*refs, heads, chunk, layer, has_state):
    if has_state:
        zq_ref, zf_ref, zv_ref, zg_ref, lb_ref, ng_ref, s0_ref = refs[:7]
        rest = refs[7:]
    else:
        zq_ref, zf_ref, zv_ref, zg_ref, lb_ref, ng_ref = refs[:6]
        s0_ref = None
        rest = refs[6:]
    y_ref, st_out_ref, st_ref = rest[:3]
    bc_ref = rest[3] if len(rest) > 3 else None
    tstep = pl.program_id(2)
    nsteps = pl.num_programs(2)
    tb = zq_ref.shape[0]

    @pl.when(tstep == 0)
    def _():
        for h in range(heads):
            if has_state:
                st_ref[h] = s0_ref[h].T
            else:
                st_ref[h] = jnp.zeros((HGRN_DK, HGRN_DK), F32)

    logits = lb_ref[...]
    ex = jnp.exp(logits - jnp.max(logits, axis=0, keepdims=True))
    sm = ex / jnp.sum(ex, axis=0, keepdims=True)
    lb = jnp.zeros_like(sm[0:1])
    for l in range(1, layer + 1):
        lb = lb + sm[l:l + 1]
    ng = ng_ref[...]

    sub = V7X_SUBLANES
    rblk = sub if chunk % sub == 0 else chunk
    nblk = chunk // rblk
    rowi = lax.broadcasted_iota(jnp.int32, (rblk, HGRN_DK), 0)
    rowi_all = lax.broadcasted_iota(jnp.int32, (chunk, zq_ref.shape[1]), 0)
    nt = (((1,), (1,)), ((), ()))
    tn = (((0,), (0,)), ((), ()))

    def body(ci, carry):
        rows = pl.ds(pl.multiple_of(ci * chunk, chunk), chunk)
        zq = zq_ref[rows, :]
        zf = zf_ref[rows, :]
        zv = zv_ref[rows, :]
        zg = zg_ref[rows, :]
        q_all = _silu(zq)
        f_all = lb + (1.0 - lb) * jax.nn.sigmoid(zf)
        lf_all = jnp.log(f_all)
        k_all = 1.0 - f_all
        og_all = ng * _silu(zg)
        if chunk % sub == 0:
            a_all = lf_all
            shift = 1
            while shift < chunk:
                a_all = a_all + jnp.where(rowi_all >= shift, pltpu.roll(a_all, shift, 0), 0.0)
                shift *= 2
        else:
            a_all = jnp.zeros_like(lf_all)
            for s in range(chunk):
                a_all = a_all + jnp.where(rowi_all >= s, lf_all[s:s + 1, :], 0.0)
        a_all = a_all * LOG2_E
        qe_all = q_all * jnp.exp2(a_all)
        a_last_all = a_all[chunk - 1:chunk, :]
        kd_all = k_all * jnp.exp2(a_last_all - a_all)
        dec_all = jnp.exp2(a_last_all)
        if bc_ref is not None:
            for h in range(heads):
                lanes = slice(h * HGRN_DK, (h + 1) * HGRN_DK)
                bc_ref[0, h] = a_all[:, lanes]
                bc_ref[1, h] = k_all[:, lanes]
                bc_ref[2, h] = zv[:, lanes]
        o_inter, scores, states_in = [], [], []
        for h in range(heads):
            lanes = slice(h * HGRN_DK, (h + 1) * HGRN_DK)
            q, k, a = q_all[:, lanes], k_all[:, lanes], a_all[:, lanes]
            st = st_ref[h]
            states_in.append(st)
            o_inter.append(lax.dot_general(qe_all[:, lanes].astype(BF16), st.astype(BF16), nt,
                                           preferred_element_type=F32))
            sc = None
            for j in range(nblk - 1):
                r0, r1 = j * rblk, (j + 1) * rblk
                b_j = a[r1 - 1:r1, :]
                qt = jnp.concatenate([jnp.zeros((r1, HGRN_DK), F32), q[r1:] * jnp.exp2(a[r1:] - b_j)], axis=0)
                kh = [k[r0:r1] * jnp.exp2(b_j - a[r0:r1])]
                if r0 > 0:
                    kh.insert(0, jnp.zeros((r0, HGRN_DK), F32))
                kh.append(jnp.zeros((chunk - r1, HGRN_DK), F32))
                sc_j = lax.dot_general(qt.astype(BF16), jnp.concatenate(kh, axis=0).astype(BF16), nt,
                                       preferred_element_type=F32)
                sc = sc_j if sc is None else sc + sc_j
            scores.append(sc)
        for h in range(heads):
            lanes = slice(h * HGRN_DK, (h + 1) * HGRN_DK)
            q, k, v, a = q_all[:, lanes], k_all[:, lanes], zv[:, lanes], a_all[:, lanes]
            st = states_in[h]
            o = o_inter[h]
            if scores[h] is not None:
                o = o + jnp.dot(scores[h].astype(BF16), v.astype(BF16), preferred_element_type=F32)
            ob = [None] * nblk
            for s in range(chunk):
                bi = s // rblk
                r0 = bi * rblk
                if bc_ref is not None:
                    a_s, k_s, v_s = (jnp.broadcast_to(bc_ref[c, h, s:s + 1, :], (rblk, HGRN_DK))
                                     for c in range(3))
                else:
                    a_s, k_s, v_s = a[s:s + 1, :], k[s:s + 1, :], v[s:s + 1, :]
                e = jnp.where(rowi >= s - r0, jnp.exp2(a[r0:r0 + rblk] - a_s), 0.0)
                wgt = (q[r0:r0 + rblk] * k_s) * e
                term = jnp.sum(wgt, axis=-1, keepdims=True) * v_s
                ob[bi] = term if ob[bi] is None else ob[bi] + term
            o = o + (ob[0] if nblk == 1 else jnp.concatenate(ob, axis=0))
            upd = lax.dot_general(v.astype(BF16), kd_all[:, lanes].astype(BF16), tn,
                                  preferred_element_type=F32)
            st_ref[h] = st * dec_all[:, lanes] + upd
            on = o * lax.rsqrt(jnp.mean(o * o, axis=-1, keepdims=True) + EPS)
            y_ref[rows, lanes] = (on * og_all[:, lanes]).astype(y_ref.dtype)
        return carry

    lax.fori_loop(0, tb // chunk, body, 0)

    @pl.when(tstep == nsteps - 1)
    def _():
        for h in range(heads):
            st_out_ref[h] = st_ref[h].T


def _hgrn(z, lb_logits, norm_g, s0, *, layer, heads, tb, out_dtype):
    b, t, d4 = z.shape
    d = d4 // 4
    nh = d // HGRN_DK
    nhg = nh // heads
    hb = heads * HGRN_DK
    chunk = math.gcd(t, HGRN_CHUNK)
    depth = lb_logits.shape[0]
    assert t % tb == 0 and tb % chunk == 0 and nh % heads == 0

    def zspec(part):
        return pl.BlockSpec((None, tb, hb), lambda bi, hg, i: (bi, i, part * nhg + hg))

    in_specs = [zspec(0), zspec(1), zspec(2), zspec(3),
                pl.BlockSpec((depth, hb), lambda bi, hg, i: (0, hg)),
                pl.BlockSpec((1, hb), lambda bi, hg, i: (0, hg))]
    args = [z, z, z, z, lb_logits, norm_g.reshape(1, d)]
    sspec = pl.BlockSpec((None, heads, HGRN_DK, HGRN_DK), lambda bi, hg, i: (bi, hg, 0, 0))
    if s0 is not None:
        in_specs.append(sspec)
        args.append(s0)
    est = (2 * (4 * _nbytes((tb, hb), F32) + _nbytes((tb, hb), out_dtype))
           + 5 * _nbytes((heads, HGRN_DK, HGRN_DK), F32))
    return pl.pallas_call(
        functools.partial(_hgrn_kernel, heads=heads, chunk=chunk, layer=layer, has_state=s0 is not None),
        out_shape=(jax.ShapeDtypeStruct((b, t, d), out_dtype),
                   jax.ShapeDtypeStruct((b, nh, HGRN_DK, HGRN_DK), F32)),
        grid=(b, nhg, t // tb),
        in_specs=in_specs,
        out_specs=(pl.BlockSpec((None, tb, hb), lambda bi, hg, i: (bi, i, hg)), sspec),
        scratch_shapes=([pltpu.VMEM((heads, HGRN_DK, HGRN_DK), F32)]
                        + ([pltpu.VMEM((3, heads, chunk, HGRN_DK), F32)] if chunk % V7X_SUBLANES == 0 else [])),
        compiler_params=_params(est, 3),
        name="hgrn2",
    )(*args)


def _forward(xp, xs, c_all, ada_w, ada_b, a_bufs, b_states, weights, *, nb, t_seq):
    (norm_mix_g, norm_ffn_g, a_w_qkv, a_w_o, b_w_in, b_lb_logits, b_norm_g, b_w_o,
     ffn_w_in, ffn_w_out, final_g) = weights
    bp, t, d = xp.shape
    ms = xs.shape[1]
    depth = norm_mix_g.shape[0]
    d_ff = ffn_w_out.shape[1]
    ngroups = len(DILATIONS)
    hw = a_w_o.shape[1]
    nheads = hw // HEAD_DIM
    bm = _pick(t, (1024, 512, 256, 128, 64, 32, 16, 8))
    bm_e = _pick(t, (256, 128, 64, 32, 16, 8))
    bn = 512
    ident = lambda j: j
    kv_p, kv_s, st_p, st_s = [], [], [], []
    w_dn_all = None
    bnf = 256
    nff = d_ff // bnf
    ffn_steps = nff * bp * (t // bm)
    ada_b3 = ada_b.reshape(depth, 1, 6 * d)
    ride = depth > 1 and _ada_job(c_all, ada_w, ada_b3, 1, ffn_steps) is not None
    mod0 = _ada_modulation(c_all, ada_w, ada_b3, nlayers=1 if ride else depth, bn=512)
    mod_rows = {l: mod0[l] for l in range(mod0.shape[0])}
    pending_shifts = []

    for layer in range(depth):
        i = layer // N_MIXERS
        mod_l = mod_rows[layer]
        mp = mod_l[:bp].reshape(bp, 1, 6 * d)
        msd = jnp.repeat(mod_l[bp:bp + nb], t_seq, axis=0).reshape(1, ms, 6 * d)
        hp = _norm_mod(xp, norm_mix_g[layer], mp, 1, 0, bm=bm_e)
        hs = _norm_mod(xs, norm_mix_g[layer], msd, 1, 0, bm=ms)
        if layer % N_MIXERS == 0:
            qp, qs = _linear(hp, a_w_qkv, layer=i, n=ngroups * hw, col_map=ident, out_dtype=F32,
                             bm=bm, bn=bn, xs=hs, name="a_q")
            half = hw // bn
            q5 = qs.reshape(nb, t_seq, ngroups, nheads, HEAD_DIM)
            flat = lambda a: a.reshape(nb, t_seq * nheads, HEAD_DIM)
            kvs = []
            carry = None
            for g in range(ngroups):
                def kv_cols(j, g=g):
                    return jnp.where(j < half, (ngroups + g) * half + j, (2 * ngroups + g) * half + (j - half))
                kvp, kvn = _linear(hp, a_w_qkv, layer=i, n=2 * hw, col_map=kv_cols, out_dtype=F32,
                                   bm=bm, bn=bn, xs=hs, name=f"a_kv{g}")
                kvs.append(kvp)
                keep = min(DILATIONS[g] * N_BACK, t)
                kv_keep = kvp if keep == t else kvp[:, t - keep:]
                kv_p.append(kv_keep.reshape(1, bp, keep, 2, nheads, HEAD_DIM))
                buf = a_bufs[g][i]
                w = buf.shape[1]
                kvn = kvn.reshape(nb, t_seq, 2, nheads, HEAD_DIM)
                carry = _attn_cached(flat(q5[:, :, g]), flat(kvn[:, :, 0]), flat(kvn[:, :, 1]), buf, g,
                                     carry, final=g == ngroups - 1, wc=min(w, 256))
                pending_shifts.append((buf.reshape(nb, w, 2 * nheads, HEAD_DIM),
                                       kvn.reshape(nb, t_seq, 2 * nheads, HEAD_DIM), len(kv_s)))
                kv_s.append(None)
            yp = _attn_prompt(qp, kvs)
            ys = carry.reshape(1, ms, hw).astype(BF16)
            w_o = a_w_o
        else:
            zp, zs = _linear(hp, b_w_in, layer=i, n=4 * d, col_map=ident, out_dtype=F32, bm=bm, bn=bn,
                             xs=hs, name="b_in")
            hg_heads = _pick(d // HGRN_DK, (8, 4, 2, 1))
            yp, s_p = _hgrn(zp, b_lb_logits, b_norm_g[i], None, layer=layer, heads=hg_heads,
                            tb=min(t, 512), out_dtype=BF16)
            ysb, s_s = _hgrn(zs.reshape(nb, t_seq, 4 * d), b_lb_logits, b_norm_g[i], b_states[i],
                             layer=layer, heads=hg_heads, tb=t_seq, out_dtype=F32)
            ys = ysb.reshape(1, ms, d).astype(BF16)
            st_p.append(s_p)
            st_s.append(s_s)
            w_o = b_w_o
        xp, xs = _linear(yp, w_o, layer=i, n=d, col_map=ident, out_dtype=F32, bm=bm, bn=bn,
                         mode="resid", res=xp, gate=mp, gate_chunk=2, xs=ys, res_s=xs, gate_s=msd,
                         name="mix_o")
        hp = _norm_mod(xp, norm_ffn_g[layer], mp, 4, 3, bm=bm_e)
        hs = _norm_mod(xs, norm_ffn_g[layer], msd, 4, 3, bm=ms)
        jobs, roles = [], []
        cast = _cast_job(ffn_w_out, layer, ffn_steps)
        if cast is not None:
            jobs.append(cast)
            roles.append(("w_dn", None))
        if ride and layer + 1 < depth:
            jobs.append(_ada_job(c_all, ada_w, ada_b3, layer + 1, ffn_steps))
            roles.append(("mod", layer + 1))
        else:
            for buf4, new4, slot in pending_shifts:
                job = next((jb for jb in (_shift_job(buf4, new4, ffn_steps, rb=rb)
                                          for rb in (64, 128, 256, 512, 1024) if rb <= buf4.shape[1])
                            if jb is not None), None)
                if job is not None:
                    jobs.append(job)
                    roles.append(("shift", slot))
            pending_shifts = [p for p in pending_shifts if ("shift", p[2]) not in roles]
        outs = _linear(hp, ffn_w_in, layer=layer, n=d_ff, col_map=ident, col_map2=lambda j: nff + j,
                       out_dtype=BF16, bm=bm, bn=bnf, mode="swiglu", xs=hs, side_jobs=jobs, name="ffn_in")
        actp, acts = outs[0], outs[1]
        w_dn = None
        for (role, key), val in zip(roles, outs[2:]):
            if role == "w_dn":
                w_dn, dn_layer = val[None], 0
            elif role == "mod":
                mod_rows[key] = val
            else:
                kv_s[key] = val.reshape(1, nb, val.shape[1], 2, nheads, HEAD_DIM)
        if w_dn is None:
            if w_dn_all is None:
                w_dn_all = _to_bf16(ffn_w_out, kb=_pick(d_ff, (d_ff // 16, d_ff // 8, d_ff // 4, d_ff // 2)))
            w_dn, dn_layer = w_dn_all, layer
        half_ff = d_ff // 2
        bk = half_ff if (d_ff % 2 == 0 and half_ff % V7X_LANES == 0) else d_ff
        xp, xs = _linear(actp, w_dn, layer=dn_layer, n=d, col_map=ident, out_dtype=F32,
                         bm=bm, bn=bn, bk=bk, mode="resid", res=xp, gate=mp, gate_chunk=5,
                         xs=acts, res_s=xs, gate_s=msd, name="ffn_out")
    for buf4, new4, slot in pending_shifts:
        w = buf4.shape[1]
        kv_s[slot] = _shift_append(buf4, new4, rb=min(w, 256)).reshape(1, nb, w, 2, nheads, HEAD_DIM)
    yp = _final_norm(xp, final_g, bm=bm_e)
    ys = _final_norm(xs, final_g, bm=ms)
    return yp, ys, kv_p, kv_s, st_p, st_s


def kernel(x_prompt, x_sample, state_a_kv_w128, state_a_kv_w512, state_a_kv_w2048, state_b_rec,
           c_prompt, c_sample, ada_w, ada_b, norm_mix_g, norm_ffn_g, a_w_qkv, a_w_o,
           b_w_in, b_lb_logits, b_norm_g, b_w_o, ffn_w_in, ffn_w_out, final_g):
    bp, sp, d = x_prompt.shape
    bs, ts, _ = x_sample.shape
    depth = ada_w.shape[0]
    weights = (norm_mix_g, norm_ffn_g, a_w_qkv, a_w_o, b_w_in, b_lb_logits, b_norm_g, b_w_o,
               ffn_w_in, ffn_w_out, final_g)

    rpad = -(bp + bs) % V7X_SUBLANES
    c_all = jnp.pad(jnp.concatenate([c_prompt, c_sample], axis=0), ((0, rpad), (0, 0)))
    y_p, y_s, kv_p, kv_s, st_p, st_s = _forward(
        x_prompt, x_sample.reshape(1, bs * ts, d), c_all, ada_w, ada_b,
        (state_a_kv_w128, state_a_kv_w512, state_a_kv_w2048), state_b_rec, weights, nb=bs, t_seq=ts)
    return (y_p, y_s.reshape(bs, ts, d), kv_p[0], kv_s[0], kv_p[1], kv_s[1], kv_p[2], kv_s[2],
            jnp.stack(st_p, axis=0), jnp.stack(st_s, axis=0))
```

```python
import functools
import math
from typing import Callable, NamedTuple

import jax
import jax.numpy as jnp
from jax import lax
from jax.experimental import pallas as pl
from jax.experimental.pallas import tpu as pltpu

F32 = jnp.float32
BF16 = jnp.bfloat16

EPS = 1e-6
DILATIONS = (1, 4, 16)
N_BACK = 128
HEAD_DIM = 128
HGRN_DK = 128
HGRN_CHUNK = 16
LOG2_E = 1.4426950408889634
N_MIXERS = 2
V7X_LANES = 128
V7X_SUBLANES = 8
V7X_SCOPED_VMEM_BYTES = 60000 * 1024
V7X_MIN_SCOPED_VMEM_BYTES = 16 * 1024 * 1024

ROW_TILES = (1024, 512, 256, 128, 64, 32, 16, 8)
NORM_ROW_TILES = (256, 128, 64, 32, 16, 8)
N_TILE = 512
N_TILE_SWIGLU = 256
ADA_N_TILE = 512
HGRN_HEADS_PER_STEP = (16, 8, 4, 2, 1)
HGRN_ROW_TILE = 512
WINDOW_CHUNK = 256
SIDE_SHIFT_ROWS = (64, 128, 256, 512, 1024)
MERGE_ROW_TILE = 256


def _params(nbytes, n_axes):
    limit = int(min(V7X_SCOPED_VMEM_BYTES, max(2 * nbytes, V7X_MIN_SCOPED_VMEM_BYTES)))
    return pltpu.CompilerParams(dimension_semantics=("arbitrary",) * n_axes, vmem_limit_bytes=limit)


def _nbytes(shape, dtype):
    return math.prod(shape) * jnp.dtype(dtype).itemsize


def _silu(x):
    return x * jax.nn.sigmoid(x)


def _pick(total, prefs):
    for p in prefs:
        if total % p == 0:
            return p
    return total


class _SideJob(NamedTuple):
    args: tuple
    in_blocks: tuple
    out_shapes: tuple
    out_blocks: tuple
    body: Callable
    nbytes: int

    @property
    def n_in(self):
        return len(self.in_blocks)

    @property
    def n_out(self):
        return len(self.out_blocks)


def _ada_kernel(c_ref, w_ref, b_ref, o_ref):
    s = _silu(c_ref[...])
    acc = jnp.dot(s.astype(BF16), w_ref[...].astype(BF16), preferred_element_type=F32)
    o_ref[...] = acc + b_ref[...]


def _ada_job(c_all, ada_w, ada_b3, layer, steps):
    _, d, n = ada_w.shape
    r = c_all.shape[0]
    cw = next((c for c in range(V7X_LANES, 4 * V7X_LANES + 1, V7X_LANES) if n % c == 0 and n // c <= steps), None)
    if cw is None:
        return None
    nact = n // cw
    col = lambda s: jnp.minimum(s, nact - 1)

    def body(step, ins, outs):
        @pl.when(step < nact)
        def _():
            _ada_kernel(*ins, outs[0])

    return _SideJob(
        args=(c_all, ada_w, ada_b3),
        in_blocks=(((r, d), lambda s: (0, 0)),
                   ((None, d, cw), lambda s: (layer, 0, col(s))),
                   ((None, 1, cw), lambda s: (layer, 0, col(s)))),
        out_shapes=(jax.ShapeDtypeStruct((r, n), F32),),
        out_blocks=(((r, cw), lambda s: (0, col(s))),),
        body=body,
        nbytes=2 * (_nbytes((d, cw), F32) + _nbytes((r, cw), F32)) + _nbytes((d, cw), BF16) + _nbytes((r, d), F32))


def _ada_modulation(c_all, ada_w, ada_b3, *, nlayers, bn):
    _, d, n = ada_w.shape
    depth = nlayers
    r = c_all.shape[0]
    est = 2 * (_nbytes((d, bn), F32) + _nbytes((r, bn), F32)) + _nbytes((d, bn), BF16) + _nbytes((r, d), F32)
    return pl.pallas_call(
        _ada_kernel,
        out_shape=jax.ShapeDtypeStruct((depth, r, n), F32),
        grid=(depth, n // bn),
        in_specs=[
            pl.BlockSpec((r, d), lambda l, j: (0, 0)),
            pl.BlockSpec((None, d, bn), lambda l, j: (l, 0, j)),
            pl.BlockSpec((None, 1, bn), lambda l, j: (l, 0, j)),
        ],
        out_specs=pl.BlockSpec((None, r, bn), lambda l, j: (l, 0, j)),
        compiler_params=_params(est, 2),
        name="ada_modulation",
    )(c_all, ada_w, ada_b3)


def _norm_mod_kernel(x_ref, g_ref, sc_ref, sh_ref, o_ref):
    x = x_ref[...]
    y = x * lax.rsqrt(jnp.mean(x * x, axis=-1, keepdims=True) + EPS)
    h = (y * g_ref[...]) * (1.0 + sc_ref[...]) + sh_ref[...]
    o_ref[...] = h.astype(o_ref.dtype)


def _norm_mod(x, g, mod, sc_chunk, sh_chunk, *, bm):
    bn_, t, d = x.shape
    tg = mod.shape[1]
    mg = 1 if tg == 1 else bm

    def mod_map(chunk):
        return lambda b, i: (b, 0 if tg == 1 else i, chunk)

    est = 2 * (_nbytes((bm, d), F32) + _nbytes((bm, d), BF16)) + 2 * _nbytes((bm, d), F32)
    return pl.pallas_call(
        _norm_mod_kernel,
        out_shape=jax.ShapeDtypeStruct((bn_, t, d), BF16),
        grid=(bn_, t // bm),
        in_specs=[
            pl.BlockSpec((None, bm, d), lambda b, i: (b, i, 0)),
            pl.BlockSpec((1, d), lambda b, i: (0, 0)),
            pl.BlockSpec((None, mg, d), mod_map(sc_chunk)),
            pl.BlockSpec((None, mg, d), mod_map(sh_chunk)),
        ],
        out_specs=pl.BlockSpec((None, bm, d), lambda b, i: (b, i, 0)),
        compiler_params=_params(est, 2),
        name="norm_mod",
    )(x, g.reshape(1, d), mod, mod)


def _final_norm_kernel(x_ref, g_ref, o_ref):
    x = x_ref[...]
    y = x * lax.rsqrt(jnp.mean(x * x, axis=-1, keepdims=True) + EPS)
    o_ref[...] = y * g_ref[...]


def _final_norm(x, g, *, bm):
    bn_, t, d = x.shape
    est = 6 * _nbytes((bm, d), F32)
    return pl.pallas_call(
        _final_norm_kernel,
        out_shape=jax.ShapeDtypeStruct((bn_, t, d), F32),
        grid=(bn_, t // bm),
        in_specs=[
            pl.BlockSpec((None, bm, d), lambda b, i: (b, i, 0)),
            pl.BlockSpec((1, d), lambda b, i: (0, 0)),
        ],
        out_specs=pl.BlockSpec((None, bm, d), lambda b, i: (b, i, 0)),
        compiler_params=_params(est, 2),
        name="final_norm",
    )(x, g.reshape(1, d))


def _to_bf16_kernel(w_ref, o_ref):
    o_ref[...] = w_ref[...].astype(BF16)


def _to_bf16(w, *, kb):
    l, k, n = w.shape
    assert k % kb == 0
    est = 2 * (_nbytes((kb, n), F32) + _nbytes((kb, n), BF16))
    spec = pl.BlockSpec((None, kb, n), lambda li, i: (li, i, 0))
    return pl.pallas_call(
        _to_bf16_kernel,
        out_shape=jax.ShapeDtypeStruct(w.shape, BF16),
        grid=(l, k // kb),
        in_specs=[spec],
        out_specs=spec,
        compiler_params=_params(est, 2),
        name="weight_to_bf16",
    )(w)


def _cast_job(w, layer, steps):
    kc, nc = w.shape[1:]
    if kc % steps or (kc // steps) % 16:
        return None
    slab = kc // steps

    def body(step, ins, outs):
        _to_bf16_kernel(ins[0], outs[0])

    return _SideJob(
        args=(w,),
        in_blocks=(((None, slab, nc), lambda s: (layer, s, 0)),),
        out_shapes=(jax.ShapeDtypeStruct((kc, nc), BF16),),
        out_blocks=(((slab, nc), lambda s: (s, 0)),),
        body=body,
        nbytes=2 * (_nbytes((slab, nc), F32) + _nbytes((slab, nc), BF16)))


def _linear_kernel(*refs, nk, mode, extra, side_jobs):
    refs = list(refs)
    x_ref, w_ref = refs[0], refs[1]
    pos = 2
    w2_ref = res_ref = gate_ref = xs_ref = ress_ref = gates_ref = os_ref = None
    if mode == "swiglu":
        w2_ref = refs[pos]
        pos += 1
    if mode == "resid":
        res_ref, gate_ref = refs[pos], refs[pos + 1]
        pos += 2
    if extra:
        xs_ref = refs[pos]
        pos += 1
        if mode == "resid":
            ress_ref, gates_ref = refs[pos], refs[pos + 1]
            pos += 2
    job_ins = []
    for job in side_jobs:
        job_ins.append(refs[pos:pos + job.n_in])
        pos += job.n_in
    o_ref = refs[pos]
    pos += 1
    if extra:
        os_ref = refs[pos]
        pos += 1
    step = (pl.program_id(0) * pl.num_programs(1) + pl.program_id(1)) * pl.num_programs(2) + pl.program_id(2)
    for job, ins in zip(side_jobs, job_ins):
        job.body(step, ins, refs[pos:pos + job.n_out])
        pos += job.n_out
    scratch = refs[pos:]

    def finalize(out_ref, r_ref, g_ref, acc, acc2=None):
        if mode == "swiglu":
            out_ref[...] = (_silu(acc) * acc2).astype(out_ref.dtype)
        elif mode == "resid":
            out_ref[...] = (r_ref[...] + g_ref[...] * acc).astype(out_ref.dtype)
        else:
            out_ref[...] = acc.astype(out_ref.dtype)

    cast = w_ref.dtype != BF16
    first = jnp.logical_and(pl.program_id(1) == 0, pl.program_id(2) == 0)
    if nk == 1:
        wb_ref, wb2_ref = w_ref, w2_ref
        if cast:
            wb_ref = scratch[0]
            wb2_ref = scratch[1] if mode == "swiglu" else None

        def first_step():
            if cast:
                wb_ref[...] = w_ref[...].astype(BF16)
                if mode == "swiglu":
                    wb2_ref[...] = w2_ref[...].astype(BF16)
            if extra:
                xs = xs_ref[...]
                accs = jnp.dot(xs, wb_ref[...], preferred_element_type=F32)
                accs2 = None
                if mode == "swiglu":
                    accs2 = jnp.dot(xs, wb2_ref[...], preferred_element_type=F32)
                finalize(os_ref, ress_ref, gates_ref, accs, accs2)

        if cast or extra:
            pl.when(first)(first_step)

        x = x_ref[...]
        acc = jnp.dot(x, wb_ref[...], preferred_element_type=F32)
        acc2 = None
        if mode == "swiglu":
            acc2 = jnp.dot(x, wb2_ref[...], preferred_element_type=F32)
        finalize(o_ref, res_ref, gate_ref, acc, acc2)
    else:
        acc_ref = scratch[0]
        k = pl.program_id(3)
        w = w_ref[...].astype(BF16) if cast else w_ref[...]
        part = jnp.dot(x_ref[...], w, preferred_element_type=F32)

        @pl.when(k == 0)
        def _():
            acc_ref[...] = part

        @pl.when(k > 0)
        def _():
            acc_ref[...] += part

        @pl.when(k == nk - 1)
        def _():
            finalize(o_ref, res_ref, gate_ref, acc_ref[...])

        if extra:
            accs_ref = scratch[1]

            @pl.when(first)
            def _():
                parts = jnp.dot(xs_ref[...], w, preferred_element_type=F32)

                @pl.when(k == 0)
                def _():
                    accs_ref[...] = parts

                @pl.when(k > 0)
                def _():
                    accs_ref[...] += parts

                @pl.when(k == nk - 1)
                def _():
                    finalize(os_ref, ress_ref, gates_ref, accs_ref[...])


def _linear(x, w, *, layer, n, col_map, out_dtype, bm, bn, bk=None, mode="plain",
            col_map2=None, res=None, gate=None, gate_chunk=0, xs=None, res_s=None, gate_s=None,
            side_jobs=(), name="linear"):
    bn_, t, kdim = x.shape
    bk = kdim if bk is None else bk
    nk = kdim // bk
    assert kdim % bk == 0 and t % bm == 0 and n % bn == 0
    assert mode != "swiglu" or nk == 1
    grid = (n // bn, bn_, t // bm, nk)

    in_specs = [
        pl.BlockSpec((None, bm, bk), lambda j, b, i, k: (b, i, k)),
        pl.BlockSpec((None, bk, bn), lambda j, b, i, k: (layer, k, col_map(j))),
    ]
    args = [x, w]
    est = 2 * (_nbytes((bm, bk), BF16) + _nbytes((bk, bn), w.dtype) + _nbytes((bm, bn), out_dtype))
    est += _nbytes((bk, bn), BF16) + 2 * _nbytes((bm, bn), F32)
    if mode == "swiglu":
        in_specs.append(pl.BlockSpec((None, bk, bn), lambda j, b, i, k: (layer, k, col_map2(j))))
        args.append(w)
        est += 2 * _nbytes((bk, bn), F32) + _nbytes((bk, bn), BF16) + _nbytes((bm, bn), F32)
    if mode == "resid":
        tg = gate.shape[1]
        mg = 1 if tg == 1 else bm
        nb = n // bn
        in_specs.append(pl.BlockSpec((None, bm, bn), lambda j, b, i, k: (b, i, j)))
        in_specs.append(pl.BlockSpec(
            (None, mg, bn), lambda j, b, i, k: (b, 0 if tg == 1 else i, gate_chunk * nb + j)))
        args += [res, gate]
        est += 4 * _nbytes((bm, bn), F32)

    out_shape = jax.ShapeDtypeStruct((bn_, t, n), out_dtype)
    out_specs = pl.BlockSpec((None, bm, bn), lambda j, b, i, k: (b, i, j))
    ms = 0
    if xs is not None:
        ms = xs.shape[1]
        in_specs.append(pl.BlockSpec((None, ms, bk), lambda j, b, i, k: (0, 0, k)))
        args.append(xs)
        if mode == "resid":
            nb = n // bn
            in_specs.append(pl.BlockSpec((None, ms, bn), lambda j, b, i, k: (0, 0, j)))
            in_specs.append(pl.BlockSpec((None, ms, bn), lambda j, b, i, k: (0, 0, gate_chunk * nb + j)))
            args += [res_s, gate_s]
        out_shape = (out_shape, jax.ShapeDtypeStruct((1, ms, n), out_dtype))
        out_specs = (out_specs, pl.BlockSpec((None, ms, bn), lambda j, b, i, k: (0, 0, j)))
        est += 2 * _nbytes((ms, bk), BF16) + 8 * _nbytes((ms, bn), F32)

    if side_jobs:
        assert nk == 1
        nt_ = t // bm
        out_shape = tuple(out_shape) if isinstance(out_shape, tuple) else (out_shape,)
        out_specs = tuple(out_specs) if isinstance(out_specs, tuple) else (out_specs,)

        def at_step(index_of_step):
            return lambda j, b, i, k: index_of_step((j * bn_ + b) * nt_ + i)

        for job in side_jobs:
            in_specs += [pl.BlockSpec(blk, at_step(f)) for blk, f in job.in_blocks]
            args += list(job.args)
            out_shape += tuple(job.out_shapes)
            out_specs += tuple(pl.BlockSpec(blk, at_step(f)) for blk, f in job.out_blocks)
            est += job.nbytes

    if nk > 1:
        scratch = [pltpu.VMEM((bm, bn), F32)] + ([pltpu.VMEM((ms, bn), F32)] if ms else [])
    elif w.dtype != BF16:
        scratch = [pltpu.VMEM((bk, bn), BF16)] * (2 if mode == "swiglu" else 1)
    else:
        scratch = []

    return pl.pallas_call(
        functools.partial(_linear_kernel, nk=nk, mode=mode, extra=xs is not None,
                          side_jobs=tuple(side_jobs)),
        out_shape=out_shape,
        grid=grid,
        in_specs=in_specs,
        out_specs=out_specs,
        scratch_shapes=scratch,
        compiler_params=_params(est, 4),
        name=name,
    )(*args)


def _attn_prompt_kernel(*refs, ngroups):
    q_refs, k_refs, v_refs = refs[:ngroups], refs[ngroups:2 * ngroups], refs[2 * ngroups:3 * ngroups]
    o_ref, og_ref, lg_ref, bias_ref = refs[3 * ngroups:]
    seq = o_ref.shape[0]
    scale = HEAD_DIM ** -0.5
    nt = (((1,), (1,)), ((), ()))
    neg = -jnp.inf
    row1 = lax.broadcasted_iota(jnp.int32, (N_BACK, N_BACK), 0)
    col1 = lax.broadcasted_iota(jnp.int32, (N_BACK, N_BACK), 1)
    row2 = lax.broadcasted_iota(jnp.int32, (N_BACK, 2 * N_BACK), 0)
    col2 = lax.broadcasted_iota(jnp.int32, (N_BACK, 2 * N_BACK), 1)
    ok_first = col1 <= row1
    ok_band = jnp.logical_or(jnp.logical_and(col2 < N_BACK, col2 >= row2),
                             jnp.logical_and(col2 >= N_BACK, col2 - N_BACK <= row2))
    bias_ref[:, :N_BACK] = jnp.where(ok_first, 0.0, neg)
    bias_ref[:, N_BACK:] = jnp.where(ok_band, 0.0, neg)
    ones = jnp.ones((N_BACK, HEAD_DIM), BF16)

    for g in range(ngroups):
        dil = DILATIONS[g]
        length = seq // dil
        nblk = length // N_BACK
        for r in range(dil):
            sub_rows = slice(0, length) if dil == 1 else pl.ds(r, length, stride=dil)
            q = q_refs[g][sub_rows, :].astype(BF16)
            k = k_refs[g][sub_rows, :].astype(BF16)
            v = v_refs[g][sub_rows, :].astype(BF16)
            scores = []
            for i in range(nblk):
                qi = q[i * N_BACK:(i + 1) * N_BACK]
                keys = k[max(i - 1, 0) * N_BACK:(i + 1) * N_BACK]
                scores.append(lax.dot_general(qi, keys, nt, preferred_element_type=F32) * scale)
            probs, maxes = [], []
            for i in range(nblk):
                s = scores[i] + (bias_ref[:, :N_BACK] if i == 0 else bias_ref[:, N_BACK:])
                m = jnp.max(s, axis=-1, keepdims=True)
                probs.append(jnp.exp(s - m).astype(BF16))
                maxes.append(m)
            for i in range(nblk):
                vals = v[max(i - 1, 0) * N_BACK:(i + 1) * N_BACK]
                vext = jnp.concatenate([vals, jnp.concatenate([ones] * (vals.shape[0] // N_BACK), axis=0)], axis=1)
                acc = jnp.dot(probs[i], vext, preferred_element_type=F32)
                den = acc[:, HEAD_DIM:]
                start = r + i * N_BACK * dil
                out_rows = slice(start, start + N_BACK) if dil == 1 else pl.ds(start, N_BACK, stride=dil)
                og_ref.at[g][out_rows, :] = acc[:, :HEAD_DIM] / den
                lg_ref.at[g][out_rows, :] = maxes[i] + jnp.log(den)

    rb = MERGE_ROW_TILE
    for c in range(seq // rb):
        rows = slice(c * rb, (c + 1) * rb)
        ls = [lg_ref[g, rows, :] for g in range(ngroups)]
        m = functools.reduce(jnp.maximum, ls)
        es = [jnp.exp(l - m) for l in ls]
        tot = functools.reduce(lambda a, b: a + b, es)
        num = functools.reduce(lambda a, b: a + b, [es[g] * og_ref[g, rows, :] for g in range(ngroups)])
        o_ref[rows, :] = (num / tot).astype(o_ref.dtype)


def _attn_prompt(q, kvs):
    ngroups = len(DILATIONS)
    b, s, qc = q.shape
    hw = qc // ngroups
    nh = hw // HEAD_DIM
    assert all(s % (d * N_BACK) == 0 for d in DILATIONS) and s % MERGE_ROW_TILE == 0
    blk = (None, s, HEAD_DIM)
    in_specs = ([pl.BlockSpec(blk, (lambda bi, h, g=g: (bi, 0, g * nh + h))) for g in range(ngroups)]
                + [pl.BlockSpec(blk, lambda bi, h: (bi, 0, h))] * ngroups
                + [pl.BlockSpec(blk, lambda bi, h: (bi, 0, nh + h))] * ngroups)
    est = (2 * (3 * ngroups * _nbytes((s, HEAD_DIM), F32) + _nbytes((s, HEAD_DIM), BF16))
           + 2 * ngroups * _nbytes((s, HEAD_DIM), F32) + 6 * _nbytes((s, HEAD_DIM), F32))
    return pl.pallas_call(
        functools.partial(_attn_prompt_kernel, ngroups=ngroups),
        out_shape=jax.ShapeDtypeStruct((b, s, hw), BF16),
        grid=(b, nh),
        in_specs=in_specs,
        out_specs=pl.BlockSpec(blk, lambda bi, h: (bi, 0, h)),
        scratch_shapes=[pltpu.VMEM((ngroups, s, HEAD_DIM), F32), pltpu.VMEM((ngroups, s, HEAD_DIM), F32),
                        pltpu.VMEM((N_BACK, 3 * N_BACK), F32)],
        compiler_params=_params(est, 2),
        name="attn_prompt",
    )(*([q] * ngroups), *kvs, *kvs)


def _attn_cached_kernel(*refs, dil, nheads, has_carry, final):
    if has_carry:
        q_ref, kn_ref, vn_ref, kb_ref, vb_ref, m_in, l_in, a_in = refs[:8]
        outs = refs[8:]
    else:
        q_ref, kn_ref, vn_ref, kb_ref, vb_ref = refs[:5]
        outs = refs[5:]
    if final:
        o_ref, m_sc, l_sc, a_sc = outs
    else:
        m_out, l_out, a_out, m_sc, l_sc, a_sc = outs
    c = pl.program_id(1)
    nc = pl.num_programs(1)
    restricted = len(kb_ref.shape) == 4
    res = kb_ref.shape[1] if restricted else 1
    period = dil if restricted else 1
    wc = kb_ref.shape[0] * res
    w_total = kb_ref.shape[0] * period * nc
    nq = q_ref.shape[0]
    hshift = nheads.bit_length() - 1
    rshift = res.bit_length() - 1
    scale = HEAD_DIM ** -0.5
    nt = (((1,), (1,)), ((), ()))
    neg = -jnp.inf
    q = q_ref[...].astype(BF16)

    def attend(k, v, ok):
        s = lax.dot_general(q, k.astype(BF16), nt, preferred_element_type=F32) * scale
        s = jnp.where(ok, s, neg)
        m_old = m_sc[...]
        m_new = jnp.maximum(m_old, jnp.max(s, axis=-1, keepdims=True))
        alpha = jnp.exp(m_old - m_new)
        p = jnp.exp(s - m_new)
        l_sc[...] = alpha * l_sc[...] + jnp.sum(p, axis=-1, keepdims=True)
        a_sc[...] = alpha * a_sc[...] + jnp.dot(p.astype(BF16), v.astype(BF16), preferred_element_type=F32)
        m_sc[...] = m_new

    @pl.when(c == 0)
    def _():
        if has_carry:
            m_sc[...] = m_in[...]
            l_sc[...] = l_in[...]
            a_sc[...] = a_in[...]
        else:
            m_sc[...] = jnp.full(m_sc.shape, neg, F32)
            l_sc[...] = jnp.zeros(l_sc.shape, F32)
            a_sc[...] = jnp.zeros(a_sc.shape, F32)
        qr = lax.broadcasted_iota(jnp.int32, (nq, nq), 0)
        kr = lax.broadcasted_iota(jnp.int32, (nq, nq), 1)
        dt = (qr >> hshift) - (kr >> hshift)
        ok = ((qr & (nheads - 1)) == (kr & (nheads - 1))) & (dt >= 0) & ((dt & (dil - 1)) == 0)
        attend(kn_ref[...], vn_ref[...], ok)

    qr = lax.broadcasted_iota(jnp.int32, (nq, wc * nheads), 0)
    kr = lax.broadcasted_iota(jnp.int32, (nq, wc * nheads), 1)
    held = kr >> hshift
    w_row = (c * (wc // res) + (held >> rshift)) * period + (held & (res - 1))
    dist = w_total + (qr >> hshift) - w_row
    ok = (((qr & (nheads - 1)) == (kr & (nheads - 1))) & ((dist & (dil - 1)) == 0)
          & (dist <= dil * N_BACK))
    attend(kb_ref[...].reshape(wc * nheads, HEAD_DIM), vb_ref[...].reshape(wc * nheads, HEAD_DIM), ok)

    @pl.when(c == nc - 1)
    def _():
        if final:
            o_ref[...] = a_sc[...] / l_sc[...]
        else:
            m_out[...] = m_sc[...]
            l_out[...] = l_sc[...]
            a_out[...] = a_sc[...]


def _attn_cached(qf, knf, vnf, buf, g, carry, *, final, wc):
    dil = DILATIONS[g]
    b, nq, hd = qf.shape
    w, nheads = buf.shape[1], buf.shape[3]
    t_new = nq // nheads
    assert w % wc == 0 and nheads & (nheads - 1) == 0 and dil & (dil - 1) == 0
    row = pl.BlockSpec((None, nq, hd), lambda bi, c: (bi, 0, 0))
    stat = pl.BlockSpec((None, nq, 1), lambda bi, c: (bi, 0, 0))
    if dil > t_new and t_new & (t_new - 1) == 0 and w % dil == 0 and wc % dil == 0:
        wc = min(w, wc * dil // t_new)
        buf = buf.reshape(b, w // dil, dil, 2, nheads, hd)
        kv_blk = (None, wc // dil, t_new, None, nheads, hd)
        kv_specs = [pl.BlockSpec(kv_blk, lambda bi, c: (bi, c, 0, 0, 0, 0)),
                    pl.BlockSpec(kv_blk, lambda bi, c: (bi, c, 0, 1, 0, 0))]
        held = wc // dil * t_new
    else:
        kv_blk = (None, wc, None, nheads, hd)
        kv_specs = [pl.BlockSpec(kv_blk, lambda bi, c: (bi, c, 0, 0, 0)),
                    pl.BlockSpec(kv_blk, lambda bi, c: (bi, c, 1, 0, 0))]
        held = wc
    in_specs = [row, row, row] + kv_specs
    args = [qf, knf, vnf, buf, buf]
    if carry is not None:
        in_specs += [stat, stat, row]
        args += list(carry)
    if final:
        out_shape = jax.ShapeDtypeStruct((b, nq, hd), F32)
        out_specs = row
    else:
        out_shape = (jax.ShapeDtypeStruct((b, nq, 1), F32), jax.ShapeDtypeStruct((b, nq, 1), F32),
                     jax.ShapeDtypeStruct((b, nq, hd), F32))
        out_specs = (stat, stat, row)
    est = (4 * _nbytes((held, nheads, hd), F32) + 4 * _nbytes((nq, held * nheads), F32)
           + 16 * _nbytes((nq, hd), F32))
    return pl.pallas_call(
        functools.partial(_attn_cached_kernel, dil=dil, nheads=nheads, has_carry=carry is not None, final=final),
        out_shape=out_shape,
        grid=(b, w // wc),
        in_specs=in_specs,
        out_specs=out_specs,
        scratch_shapes=[pltpu.VMEM((nq, 1), F32), pltpu.VMEM((nq, 1), F32), pltpu.VMEM((nq, hd), F32)],
        compiler_params=_params(est, 2),
        name=f"attn_cached_g{g}",
    )(*args)


def _shift_block(cur_ref, nxt_ref, new_ref, o_ref, last):
    rb, t_new = cur_ref.shape[0], new_ref.shape[0]
    o_ref[0:rb - t_new] = cur_ref[t_new:rb]
    o_ref[rb - t_new:rb] = jnp.where(last, new_ref[...], nxt_ref[...])


def _shift_kernel(cur_ref, nxt_ref, new_ref, o_ref):
    _shift_block(cur_ref, nxt_ref, new_ref, o_ref, pl.program_id(1) == pl.num_programs(1) - 1)


def _shift_job(buf, new, steps, *, rb):
    b, w, r, lanes = buf.shape
    t_new = new.shape[1]
    if w % rb or rb % t_new or rb <= t_new or b * (w // rb) > steps:
        return None
    nblk = w // rb
    nact = b * nblk
    last_blk = w // t_new - 1

    def where(s):
        s = jnp.minimum(s, nact - 1)
        return lax.div(s, nblk), lax.rem(s, nblk)

    def body(step, ins, outs):
        @pl.when(step < nact)
        def _():
            _shift_block(*ins, outs[0], lax.rem(step, nblk) == nblk - 1)

    blk = (None, rb, r, lanes)
    small = (None, t_new, r, lanes)
    return _SideJob(
        args=(buf, buf, new),
        in_blocks=((blk, lambda s: (*where(s), 0, 0)),
                   (small, lambda s: (where(s)[0],
                                      jnp.minimum((where(s)[1] + 1) * (rb // t_new), last_blk), 0, 0)),
                   (small, lambda s: (where(s)[0], 0, 0, 0))),
        out_shapes=(jax.ShapeDtypeStruct(buf.shape, F32),),
        out_blocks=((blk, lambda s: (*where(s), 0, 0)),),
        body=body,
        nbytes=4 * _nbytes((rb, r, lanes), F32) + 4 * _nbytes((t_new, r, lanes), F32))


def _shift_append(buf, new, *, rb):
    b, w, r, lanes = buf.shape
    t_new = new.shape[1]
    assert w % rb == 0 and rb % t_new == 0 and rb > t_new
    last_blk = w // t_new - 1
    est = 4 * _nbytes((rb, r, lanes), F32) + 6 * _nbytes((t_new, r, lanes), F32)
    return pl.pallas_call(
        _shift_kernel,
        out_shape=jax.ShapeDtypeStruct(buf.shape, F32),
        grid=(b, w // rb),
        in_specs=[
            pl.BlockSpec((None, rb, r, lanes), lambda bi, j: (bi, j, 0, 0)),
            pl.BlockSpec((None, t_new, r, lanes),
                         lambda bi, j: (bi, jnp.minimum((j + 1) * (rb // t_new), last_blk), 0, 0)),
            pl.BlockSpec((None, t_new, r, lanes), lambda bi, j: (bi, 0, 0, 0)),
        ],
        out_specs=pl.BlockSpec((None, rb, r, lanes), lambda bi, j: (bi, j, 0, 0)),
        compiler_params=_params(est, 2),
        name="kv_shift_append",
    )(buf, buf, new)


def _hgrn_kernel(*refs, heads, chunk, layer, has_state):
    if has_state:
        zq_ref, zf_ref, zv_ref, zg_ref, lb_ref, ng_ref, s0_ref = refs[:7]
        rest = refs[7:]
    else:
        zq_ref, zf_ref, zv_ref, zg_ref, lb_ref, ng_ref = refs[:6]
        s0_ref = None
        rest = refs[6:]
    y_ref, st_out_ref, st_ref = rest[:3]
    bc_ref = rest[3] if len(rest) > 3 else None
    tstep = pl.program_id(2)
    nsteps = pl.num_programs(2)
    tb = zq_ref.shape[0]

    @pl.when(tstep == 0)
    def _():
        for h in range(heads):
            if has_state:
                st_ref[h] = s0_ref[h].T
            else:
                st_ref[h] = jnp.zeros((HGRN_DK, HGRN_DK), F32)

    logits = lb_ref[...]
    ex = jnp.exp(logits - jnp.max(logits, axis=0, keepdims=True))
    sm = ex / jnp.sum(ex, axis=0, keepdims=True)
    lb = jnp.zeros_like(sm[0:1])
    for l in range(1, layer + 1):
        lb = lb + sm[l:l + 1]
    ng = ng_ref[...]

    sub = V7X_SUBLANES
    rblk = sub if chunk % sub == 0 else chunk
    nblk = chunk // rblk
    rowi = lax.broadcasted_iota(jnp.int32, (rblk, HGRN_DK), 0)
    rowi_all = lax.broadcasted_iota(jnp.int32, (chunk, zq_ref.shape[1]), 0)
    nt = (((1,), (1,)), ((), ()))
    tn = (((0,), (0,)), ((), ()))

    def body(ci, carry):
        rows = pl.ds(pl.multiple_of(ci * chunk, chunk), chunk)
        zq = zq_ref[rows, :]
        zf = zf_ref[rows, :]
        zv = zv_ref[rows, :]
        zg = zg_ref[rows, :]
        q_all = _silu(zq)
        f_all = lb + (1.0 - lb) * jax.nn.sigmoid(zf)
        lf_all = jnp.log(f_all)
        k_all = 1.0 - f_all
        og_all = ng * _silu(zg)
        if chunk % sub == 0:
            a_all = lf_all
            shift = 1
            while shift < chunk:
                a_all = a_all + jnp.where(rowi_all >= shift, pltpu.roll(a_all, shift, 0), 0.0)
                shift *= 2
        else:
            a_all = jnp.zeros_like(lf_all)
            for s in range(chunk):
                a_all = a_all + jnp.where(rowi_all >= s, lf_all[s:s + 1, :], 0.0)
        a_all = a_all * LOG2_E
        qe_all = q_all * jnp.exp2(a_all)
        a_last_all = a_all[chunk - 1:chunk, :]
        kd_all = k_all * jnp.exp2(a_last_all - a_all)
        dec_all = jnp.exp2(a_last_all)
        if bc_ref is not None:
            for h in range(heads):
                lanes = slice(h * HGRN_DK, (h + 1) * HGRN_DK)
                bc_ref[0, h] = a_all[:, lanes]
                bc_ref[1, h] = k_all[:, lanes]
                bc_ref[2, h] = zv[:, lanes]
        o_inter, scores, states_in = [], [], []
        for h in range(heads):
            lanes = slice(h * HGRN_DK, (h + 1) * HGRN_DK)
            q, k, a = q_all[:, lanes], k_all[:, lanes], a_all[:, lanes]
            st = st_ref[h]
            states_in.append(st)
            o_inter.append(lax.dot_general(qe_all[:, lanes].astype(BF16), st.astype(BF16), nt,
                                           preferred_element_type=F32))
            sc = None
            for j in range(nblk - 1):
                r0, r1 = j * rblk, (j + 1) * rblk
                b_j = a[r1 - 1:r1, :]
                qt = jnp.concatenate([jnp.zeros((r1, HGRN_DK), F32), q[r1:] * jnp.exp2(a[r1:] - b_j)], axis=0)
                kh = [k[r0:r1] * jnp.exp2(b_j - a[r0:r1])]
                if r0 > 0:
                    kh.insert(0, jnp.zeros((r0, HGRN_DK), F32))
                kh.append(jnp.zeros((chunk - r1, HGRN_DK), F32))
                sc_j = lax.dot_general(qt.astype(BF16), jnp.concatenate(kh, axis=0).astype(BF16), nt,
                                       preferred_element_type=F32)
                sc = sc_j if sc is None else sc + sc_j
            scores.append(sc)
        for h in range(heads):
            lanes = slice(h * HGRN_DK, (h + 1) * HGRN_DK)
            q, k, v, a = q_all[:, lanes], k_all[:, lanes], zv[:, lanes], a_all[:, lanes]
            st = states_in[h]
            o = o_inter[h]
            if scores[h] is not None:
                o = o + jnp.dot(scores[h].astype(BF16), v.astype(BF16), preferred_element_type=F32)
            ob = [None] * nblk
            for s in range(chunk):
                bi = s // rblk
                r0 = bi * rblk
                if bc_ref is not None:
                    a_s, k_s, v_s = (jnp.broadcast_to(bc_ref[c, h, s:s + 1, :], (rblk, HGRN_DK))
                                     for c in range(3))
                else:
                    a_s, k_s, v_s = a[s:s + 1, :], k[s:s + 1, :], v[s:s + 1, :]
                e = jnp.where(rowi >= s - r0, jnp.exp2(a[r0:r0 + rblk] - a_s), 0.0)
                wgt = (q[r0:r0 + rblk] * k_s) * e
                term = jnp.sum(wgt, axis=-1, keepdims=True) * v_s
                ob[bi] = term if ob[bi] is None else ob[bi] + term
            o = o + (ob[0] if nblk == 1 else jnp.concatenate(ob, axis=0))
            upd = lax.dot_general(v.astype(BF16), kd_all[:, lanes].astype(BF16), tn,
                                  preferred_element_type=F32)
            st_ref[h] = st * dec_all[:, lanes] + upd
            on = o * lax.rsqrt(jnp.mean(o * o, axis=-1, keepdims=True) + EPS)
            y_ref[rows, lanes] = (on * og_all[:, lanes]).astype(y_ref.dtype)
        return carry

    lax.fori_loop(0, tb // chunk, body, 0)

    @pl.when(tstep == nsteps - 1)
    def _():
        for h in range(heads):
            st_out_ref[h] = st_ref[h].T


def _hgrn(z, lb_logits, norm_g, s0, *, layer, heads, tb, out_dtype):
    b, t, d4 = z.shape
    d = d4 // 4
    nh = d // HGRN_DK
    nhg = nh // heads
    hb = heads * HGRN_DK
    chunk = math.gcd(t, HGRN_CHUNK)
    depth = lb_logits.shape[0]
    assert t % tb == 0 and tb % chunk == 0 and nh % heads == 0

    def zspec(part):
        return pl.BlockSpec((None, tb, hb), lambda bi, hg, i: (bi, i, part * nhg + hg))

    in_specs = [zspec(0), zspec(1), zspec(2), zspec(3),
                pl.BlockSpec((depth, hb), lambda bi, hg, i: (0, hg)),
                pl.BlockSpec((1, hb), lambda bi, hg, i: (0, hg))]
    args = [z, z, z, z, lb_logits, norm_g.reshape(1, d)]
    sspec = pl.BlockSpec((None, heads, HGRN_DK, HGRN_DK), lambda bi, hg, i: (bi, hg, 0, 0))
    if s0 is not None:
        in_specs.append(sspec)
        args.append(s0)
    est = (2 * (4 * _nbytes((tb, hb), F32) + _nbytes((tb, hb), out_dtype))
           + 5 * _nbytes((heads, HGRN_DK, HGRN_DK), F32))
    return pl.pallas_call(
        functools.partial(_hgrn_kernel, heads=heads, chunk=chunk, layer=layer, has_state=s0 is not None),
        out_shape=(jax.ShapeDtypeStruct((b, t, d), out_dtype),
                   jax.ShapeDtypeStruct((b, nh, HGRN_DK, HGRN_DK), F32)),
        grid=(b, nhg, t // tb),
        in_specs=in_specs,
        out_specs=(pl.BlockSpec((None, tb, hb), lambda bi, hg, i: (bi, i, hg)), sspec),
        scratch_shapes=([pltpu.VMEM((heads, HGRN_DK, HGRN_DK), F32)]
                        + ([pltpu.VMEM((3, heads, chunk, HGRN_DK), F32)] if chunk % V7X_SUBLANES == 0 else [])),
        compiler_params=_params(est, 3),
        name="hgrn2",
    )(*args)


def _forward(xp, xs, c_all, ada_w, ada_b, a_bufs, b_states, weights, *, nb, t_seq):
    (norm_mix_g, norm_ffn_g, a_w_qkv, a_w_o, b_w_in, b_lb_logits, b_norm_g, b_w_o,
     ffn_w_in, ffn_w_out, final_g) = weights
    bp, t, d = xp.shape
    ms = xs.shape[1]
    depth = norm_mix_g.shape[0]
    d_ff = ffn_w_out.shape[1]
    ngroups = len(DILATIONS)
    hw = a_w_o.shape[1]
    nheads = hw // HEAD_DIM
    bm = _pick(t, ROW_TILES)
    bm_e = _pick(t, NORM_ROW_TILES)
    bn = N_TILE
    ident = lambda j: j
    kv_p, kv_s, st_p, st_s = [], [], [], []
    w_dn_all = None
    bnf = N_TILE_SWIGLU
    nff = d_ff // bnf
    ffn_steps = nff * bp * (t // bm)
    ada_b3 = ada_b.reshape(depth, 1, 6 * d)
    ride = depth > 1 and _ada_job(c_all, ada_w, ada_b3, 1, ffn_steps) is not None
    mod0 = _ada_modulation(c_all, ada_w, ada_b3, nlayers=1 if ride else depth, bn=ADA_N_TILE)
    mod_rows = {l: mod0[l] for l in range(mod0.shape[0])}
    pending_shifts = []

    for layer in range(depth):
        i = layer // N_MIXERS
        mod_l = mod_rows[layer]
        mp = mod_l[:bp].reshape(bp, 1, 6 * d)
        msd = jnp.repeat(mod_l[bp:bp + nb], t_seq, axis=0).reshape(1, ms, 6 * d)
        hp = _norm_mod(xp, norm_mix_g[layer], mp, 1, 0, bm=bm_e)
        hs = _norm_mod(xs, norm_mix_g[layer], msd, 1, 0, bm=ms)
        if layer % N_MIXERS == 0:
            qp, qs = _linear(hp, a_w_qkv, layer=i, n=ngroups * hw, col_map=ident, out_dtype=F32,
                             bm=bm, bn=bn, xs=hs, name="a_q")
            half = hw // bn
            q5 = qs.reshape(nb, t_seq, ngroups, nheads, HEAD_DIM)
            flat = lambda a: a.reshape(nb, t_seq * nheads, HEAD_DIM)
            kvs = []
            carry = None
            for g in range(ngroups):
                def kv_cols(j, g=g):
                    return jnp.where(j < half, (ngroups + g) * half + j, (2 * ngroups + g) * half + (j - half))
                kvp, kvn = _linear(hp, a_w_qkv, layer=i, n=2 * hw, col_map=kv_cols, out_dtype=F32,
                                   bm=bm, bn=bn, xs=hs, name=f"a_kv{g}")
                kvs.append(kvp)
                keep = min(DILATIONS[g] * N_BACK, t)
                kv_keep = kvp if keep == t else kvp[:, t - keep:]
                kv_p.append(kv_keep.reshape(1, bp, keep, 2, nheads, HEAD_DIM))
                buf = a_bufs[g][i]
                w = buf.shape[1]
                kvn = kvn.reshape(nb, t_seq, 2, nheads, HEAD_DIM)
                carry = _attn_cached(flat(q5[:, :, g]), flat(kvn[:, :, 0]), flat(kvn[:, :, 1]), buf, g,
                                     carry, final=g == ngroups - 1, wc=min(w, WINDOW_CHUNK))
                pending_shifts.append((buf.reshape(nb, w, 2 * nheads, HEAD_DIM),
                                       kvn.reshape(nb, t_seq, 2 * nheads, HEAD_DIM), len(kv_s)))
                kv_s.append(None)
            yp = _attn_prompt(qp, kvs)
            ys = carry.reshape(1, ms, hw).astype(BF16)
            w_o = a_w_o
        else:
            zp, zs = _linear(hp, b_w_in, layer=i, n=4 * d, col_map=ident, out_dtype=F32, bm=bm, bn=bn,
                             xs=hs, name="b_in")
            hg_heads = _pick(d // HGRN_DK, HGRN_HEADS_PER_STEP)
            yp, s_p = _hgrn(zp, b_lb_logits, b_norm_g[i], None, layer=layer, heads=hg_heads,
                            tb=min(t, HGRN_ROW_TILE), out_dtype=BF16)
            ysb, s_s = _hgrn(zs.reshape(nb, t_seq, 4 * d), b_lb_logits, b_norm_g[i], b_states[i],
                             layer=layer, heads=hg_heads, tb=t_seq, out_dtype=F32)
            ys = ysb.reshape(1, ms, d).astype(BF16)
            st_p.append(s_p)
            st_s.append(s_s)
            w_o = b_w_o
        xp, xs = _linear(yp, w_o, layer=i, n=d, col_map=ident, out_dtype=F32, bm=bm, bn=bn,
                         mode="resid", res=xp, gate=mp, gate_chunk=2, xs=ys, res_s=xs, gate_s=msd,
                         name="mix_o")
        hp = _norm_mod(xp, norm_ffn_g[layer], mp, 4, 3, bm=bm_e)
        hs = _norm_mod(xs, norm_ffn_g[layer], msd, 4, 3, bm=ms)
        jobs, roles = [], []
        cast = _cast_job(ffn_w_out, layer, ffn_steps)
        if cast is not None:
            jobs.append(cast)
            roles.append(("w_dn", None))
        if ride and layer + 1 < depth:
            jobs.append(_ada_job(c_all, ada_w, ada_b3, layer + 1, ffn_steps))
            roles.append(("mod", layer + 1))
        else:
            for buf4, new4, slot in pending_shifts:
                job = next((jb for jb in (_shift_job(buf4, new4, ffn_steps, rb=rb)
                                          for rb in SIDE_SHIFT_ROWS if rb <= buf4.shape[1])
                            if jb is not None), None)
                if job is not None:
                    jobs.append(job)
                    roles.append(("shift", slot))
            pending_shifts = [p for p in pending_shifts if ("shift", p[2]) not in roles]
        outs = _linear(hp, ffn_w_in, layer=layer, n=d_ff, col_map=ident, col_map2=lambda j: nff + j,
                       out_dtype=BF16, bm=bm, bn=bnf, mode="swiglu", xs=hs, side_jobs=jobs, name="ffn_in")
        actp, acts = outs[0], outs[1]
        w_dn = None
        for (role, key), val in zip(roles, outs[2:]):
            if role == "w_dn":
                w_dn, dn_layer = val[None], 0
            elif role == "mod":
                mod_rows[key] = val
            else:
                kv_s[key] = val.reshape(1, nb, val.shape[1], 2, nheads, HEAD_DIM)
        if w_dn is None:
            if w_dn_all is None:
                w_dn_all = _to_bf16(ffn_w_out, kb=_pick(d_ff, (d_ff // 16, d_ff // 8, d_ff // 4, d_ff // 2)))
            w_dn, dn_layer = w_dn_all, layer
        half_ff = d_ff // 2
        bk = half_ff if (d_ff % 2 == 0 and half_ff % V7X_LANES == 0) else d_ff
        xp, xs = _linear(actp, w_dn, layer=dn_layer, n=d, col_map=ident, out_dtype=F32,
                         bm=bm, bn=bn, bk=bk, mode="resid", res=xp, gate=mp, gate_chunk=5,
                         xs=acts, res_s=xs, gate_s=msd, name="ffn_out")
    for buf4, new4, slot in pending_shifts:
        w = buf4.shape[1]
        kv_s[slot] = _shift_append(buf4, new4, rb=min(w, WINDOW_CHUNK)).reshape(1, nb, w, 2, nheads, HEAD_DIM)
    yp = _final_norm(xp, final_g, bm=bm_e)
    ys = _final_norm(xs, final_g, bm=ms)
    return yp, ys, kv_p, kv_s, st_p, st_s


def kernel(x_prompt, x_sample, state_a_kv_w128, state_a_kv_w512, state_a_kv_w2048, state_b_rec,
           c_prompt, c_sample, ada_w, ada_b, norm_mix_g, norm_ffn_g, a_w_qkv, a_w_o,
           b_w_in, b_lb_logits, b_norm_g, b_w_o, ffn_w_in, ffn_w_out, final_g):
    bp, sp, d = x_prompt.shape
    bs, ts, _ = x_sample.shape
    depth = ada_w.shape[0]
    weights = (norm_mix_g, norm_ffn_g, a_w_qkv, a_w_o, b_w_in, b_lb_logits, b_norm_g, b_w_o,
               ffn_w_in, ffn_w_out, final_g)

    rpad = -(bp + bs) % V7X_SUBLANES
    c_all = jnp.pad(jnp.concatenate([c_prompt, c_sample], axis=0), ((0, rpad), (0, 0)))
    y_p, y_s, kv_p, kv_s, st_p, st_s = _forward(
        x_prompt, x_sample.reshape(1, bs * ts, d), c_all, ada_w, ada_b,
        (state_a_kv_w128, state_a_kv_w512, state_a_kv_w2048), state_b_rec, weights, nb=bs, t_seq=ts)
    return (y_p, y_s.reshape(bs, ts, d), kv_p[0], kv_s[0], kv_p[1], kv_s[1], kv_p[2], kv_s[2],
            jnp.stack(st_p, axis=0), jnp.stack(st_s, axis=0))
```

```python
import functools
import math
from typing import Callable, NamedTuple

import jax
import jax.numpy as jnp
from jax import lax
from jax.experimental import pallas as pl
from jax.experimental.pallas import tpu as pltpu

F32 = jnp.float32
BF16 = jnp.bfloat16

EPS = 1e-6
DILATIONS = (1, 4, 16)
N_BACK = 128
HEAD_DIM = 128
HGRN_DK = 128
HGRN_CHUNK = 16
LOG2_E = 1.4426950408889634
N_MIXERS = 2
V7X_LANES = 128
V7X_SUBLANES = 8
V7X_SCOPED_VMEM_BYTES = 60000 * 1024
V7X_MIN_SCOPED_VMEM_BYTES = 16 * 1024 * 1024

ROW_TILES = (1024, 512, 256, 128, 64, 32, 16, 8)
ROW_TILES_LONG_K = ROW_TILES[1:]
NORM_ROW_TILES = (256, 128, 64, 32, 16, 8)
N_TILE = 512
N_TILE_SWIGLU = 256
ADA_N_TILE = 512
HGRN_HEADS_PER_STEP = (16, 8, 4, 2, 1)
HGRN_ROW_TILE = 512
WINDOW_CHUNK = 256
SIDE_SHIFT_ROWS = (64, 128, 256, 512, 1024)
MERGE_ROW_TILE = 256


def _params(nbytes, n_axes):
    limit = int(min(V7X_SCOPED_VMEM_BYTES, max(2 * nbytes, V7X_MIN_SCOPED_VMEM_BYTES)))
    return pltpu.CompilerParams(dimension_semantics=("arbitrary",) * n_axes, vmem_limit_bytes=limit)


def _nbytes(shape, dtype):
    return math.prod(shape) * jnp.dtype(dtype).itemsize


def _silu(x):
    return x * jax.nn.sigmoid(x)


def _pick(total, prefs):
    for p in prefs:
        if total % p == 0:
            return p
    return total


class _SideJob(NamedTuple):
    args: tuple
    in_blocks: tuple
    out_shapes: tuple
    out_blocks: tuple
    body: Callable
    nbytes: int

    @property
    def n_in(self):
        return len(self.in_blocks)

    @property
    def n_out(self):
        return len(self.out_blocks)


def _ada_kernel(c_ref, w_ref, b_ref, o_ref):
    s = _silu(c_ref[...])
    acc = jnp.dot(s.astype(BF16), w_ref[...].astype(BF16), preferred_element_type=F32)
    o_ref[...] = acc + b_ref[...]


def _ada_job(c_all, ada_w, ada_b3, layer, steps):
    _, d, n = ada_w.shape
    r = c_all.shape[0]
    cw = next((c for c in range(V7X_LANES, 4 * V7X_LANES + 1, V7X_LANES) if n % c == 0 and n // c <= steps), None)
    if cw is None:
        return None
    nact = n // cw
    col = lambda s: jnp.minimum(s, nact - 1)

    def body(step, ins, outs):
        @pl.when(step < nact)
        def _():
            _ada_kernel(*ins, outs[0])

    return _SideJob(
        args=(c_all, ada_w, ada_b3),
        in_blocks=(((r, d), lambda s: (0, 0)),
                   ((None, d, cw), lambda s: (layer, 0, col(s))),
                   ((None, 1, cw), lambda s: (layer, 0, col(s)))),
        out_shapes=(jax.ShapeDtypeStruct((r, n), F32),),
        out_blocks=(((r, cw), lambda s: (0, col(s))),),
        body=body,
        nbytes=2 * (_nbytes((d, cw), F32) + _nbytes((r, cw), F32)) + _nbytes((d, cw), BF16) + _nbytes((r, d), F32))


def _ada_modulation(c_all, ada_w, ada_b3, *, nlayers, bn):
    _, d, n = ada_w.shape
    depth = nlayers
    r = c_all.shape[0]
    est = 2 * (_nbytes((d, bn), F32) + _nbytes((r, bn), F32)) + _nbytes((d, bn), BF16) + _nbytes((r, d), F32)
    return pl.pallas_call(
        _ada_kernel,
        out_shape=jax.ShapeDtypeStruct((depth, r, n), F32),
        grid=(depth, n // bn),
        in_specs=[
            pl.BlockSpec((r, d), lambda l, j: (0, 0)),
            pl.BlockSpec((None, d, bn), lambda l, j: (l, 0, j)),
            pl.BlockSpec((None, 1, bn), lambda l, j: (l, 0, j)),
        ],
        out_specs=pl.BlockSpec((None, r, bn), lambda l, j: (l, 0, j)),
        compiler_params=_params(est, 2),
        name="ada_modulation",
    )(c_all, ada_w, ada_b3)


def _norm_mod_kernel(x_ref, g_ref, sc_ref, sh_ref, o_ref):
    x = x_ref[...]
    y = x * lax.rsqrt(jnp.mean(x * x, axis=-1, keepdims=True) + EPS)
    h = (y * g_ref[...]) * (1.0 + sc_ref[...]) + sh_ref[...]
    o_ref[...] = h.astype(o_ref.dtype)


def _norm_mod(x, g, mod, sc_chunk, sh_chunk, *, bm):
    bn_, t, d = x.shape
    tg = mod.shape[1]
    mg = 1 if tg == 1 else bm

    def mod_map(chunk):
        return lambda b, i: (b, 0 if tg == 1 else i, chunk)

    est = 2 * (_nbytes((bm, d), F32) + _nbytes((bm, d), BF16)) + 2 * _nbytes((bm, d), F32)
    return pl.pallas_call(
        _norm_mod_kernel,
        out_shape=jax.ShapeDtypeStruct((bn_, t, d), BF16),
        grid=(bn_, t // bm),
        in_specs=[
            pl.BlockSpec((None, bm, d), lambda b, i: (b, i, 0)),
            pl.BlockSpec((1, d), lambda b, i: (0, 0)),
            pl.BlockSpec((None, mg, d), mod_map(sc_chunk)),
            pl.BlockSpec((None, mg, d), mod_map(sh_chunk)),
        ],
        out_specs=pl.BlockSpec((None, bm, d), lambda b, i: (b, i, 0)),
        compiler_params=_params(est, 2),
        name="norm_mod",
    )(x, g.reshape(1, d), mod, mod)


def _final_norm_kernel(x_ref, g_ref, o_ref):
    x = x_ref[...]
    y = x * lax.rsqrt(jnp.mean(x * x, axis=-1, keepdims=True) + EPS)
    o_ref[...] = y * g_ref[...]


def _final_norm(x, g, *, bm):
    bn_, t, d = x.shape
    est = 6 * _nbytes((bm, d), F32)
    return pl.pallas_call(
        _final_norm_kernel,
        out_shape=jax.ShapeDtypeStruct((bn_, t, d), F32),
        grid=(bn_, t // bm),
        in_specs=[
            pl.BlockSpec((None, bm, d), lambda b, i: (b, i, 0)),
            pl.BlockSpec((1, d), lambda b, i: (0, 0)),
        ],
        out_specs=pl.BlockSpec((None, bm, d), lambda b, i: (b, i, 0)),
        compiler_params=_params(est, 2),
        name="final_norm",
    )(x, g.reshape(1, d))


def _to_bf16_kernel(w_ref, o_ref):
    o_ref[...] = w_ref[...].astype(BF16)


def _to_bf16(w, *, kb):
    l, k, n = w.shape
    assert k % kb == 0
    est = 2 * (_nbytes((kb, n), F32) + _nbytes((kb, n), BF16))
    spec = pl.BlockSpec((None, kb, n), lambda li, i: (li, i, 0))
    return pl.pallas_call(
        _to_bf16_kernel,
        out_shape=jax.ShapeDtypeStruct(w.shape, BF16),
        grid=(l, k // kb),
        in_specs=[spec],
        out_specs=spec,
        compiler_params=_params(est, 2),
        name="weight_to_bf16",
    )(w)


def _cast_job(w, layer, steps):
    kc, nc = w.shape[1:]
    if kc % steps or (kc // steps) % 16:
        return None
    slab = kc // steps

    def body(step, ins, outs):
        _to_bf16_kernel(ins[0], outs[0])

    return _SideJob(
        args=(w,),
        in_blocks=(((None, slab, nc), lambda s: (layer, s, 0)),),
        out_shapes=(jax.ShapeDtypeStruct((kc, nc), BF16),),
        out_blocks=(((slab, nc), lambda s: (s, 0)),),
        body=body,
        nbytes=2 * (_nbytes((slab, nc), F32) + _nbytes((slab, nc), BF16)))


def _linear_kernel(*refs, nk, mode, extra, side_jobs):
    refs = list(refs)
    x_ref, w_ref = refs[0], refs[1]
    pos = 2
    w2_ref = res_ref = gate_ref = xs_ref = ress_ref = gates_ref = os_ref = None
    if mode == "swiglu":
        w2_ref = refs[pos]
        pos += 1
    if mode == "resid":
        res_ref, gate_ref = refs[pos], refs[pos + 1]
        pos += 2
    if extra:
        xs_ref = refs[pos]
        pos += 1
        if mode == "resid":
            ress_ref, gates_ref = refs[pos], refs[pos + 1]
            pos += 2
    job_ins = []
    for job in side_jobs:
        job_ins.append(refs[pos:pos + job.n_in])
        pos += job.n_in
    o_ref = refs[pos]
    pos += 1
    if extra:
        os_ref = refs[pos]
        pos += 1
    step = (pl.program_id(0) * pl.num_programs(1) + pl.program_id(1)) * pl.num_programs(2) + pl.program_id(2)
    for job, ins in zip(side_jobs, job_ins):
        job.body(step, ins, refs[pos:pos + job.n_out])
        pos += job.n_out
    scratch = refs[pos:]

    def finalize(out_ref, r_ref, g_ref, acc, acc2=None):
        if mode == "swiglu":
            out_ref[...] = (_silu(acc) * acc2).astype(out_ref.dtype)
        elif mode == "resid":
            out_ref[...] = (r_ref[...] + g_ref[...] * acc).astype(out_ref.dtype)
        else:
            out_ref[...] = acc.astype(out_ref.dtype)

    cast = w_ref.dtype != BF16
    first = jnp.logical_and(pl.program_id(1) == 0, pl.program_id(2) == 0)
    if nk == 1:
        wb_ref, wb2_ref = w_ref, w2_ref
        if cast:
            wb_ref = scratch[0]
            wb2_ref = scratch[1] if mode == "swiglu" else None

        def first_step():
            if cast:
                wb_ref[...] = w_ref[...].astype(BF16)
                if mode == "swiglu":
                    wb2_ref[...] = w2_ref[...].astype(BF16)
            if extra:
                xs = xs_ref[...]
                accs = jnp.dot(xs, wb_ref[...], preferred_element_type=F32)
                accs2 = None
                if mode == "swiglu":
                    accs2 = jnp.dot(xs, wb2_ref[...], preferred_element_type=F32)
                finalize(os_ref, ress_ref, gates_ref, accs, accs2)

        if cast or extra:
            pl.when(first)(first_step)

        x = x_ref[...]
        acc = jnp.dot(x, wb_ref[...], preferred_element_type=F32)
        acc2 = None
        if mode == "swiglu":
            acc2 = jnp.dot(x, wb2_ref[...], preferred_element_type=F32)
        finalize(o_ref, res_ref, gate_ref, acc, acc2)
    else:
        acc_ref = scratch[0]
        k = pl.program_id(3)
        w = w_ref[...].astype(BF16) if cast else w_ref[...]
        part = jnp.dot(x_ref[...], w, preferred_element_type=F32)

        @pl.when(k == 0)
        def _():
            acc_ref[...] = part

        @pl.when(k > 0)
        def _():
            acc_ref[...] += part

        @pl.when(k == nk - 1)
        def _():
            finalize(o_ref, res_ref, gate_ref, acc_ref[...])

        if extra:
            accs_ref = scratch[1]

            @pl.when(first)
            def _():
                parts = jnp.dot(xs_ref[...], w, preferred_element_type=F32)

                @pl.when(k == 0)
                def _():
                    accs_ref[...] = parts

                @pl.when(k > 0)
                def _():
                    accs_ref[...] += parts

                @pl.when(k == nk - 1)
                def _():
                    finalize(os_ref, ress_ref, gates_ref, accs_ref[...])


def _linear(x, w, *, layer, n, col_map, out_dtype, bm, bn, bk=None, mode="plain",
            col_map2=None, res=None, gate=None, gate_chunk=0, xs=None, res_s=None, gate_s=None,
            side_jobs=(), name="linear"):
    bn_, t, kdim = x.shape
    bk = kdim if bk is None else bk
    nk = kdim // bk
    assert kdim % bk == 0 and t % bm == 0 and n % bn == 0
    assert mode != "swiglu" or nk == 1
    grid = (n // bn, bn_, t // bm, nk)

    in_specs = [
        pl.BlockSpec((None, bm, bk), lambda j, b, i, k: (b, i, k)),
        pl.BlockSpec((None, bk, bn), lambda j, b, i, k: (layer, k, col_map(j))),
    ]
    args = [x, w]
    est = 2 * (_nbytes((bm, bk), BF16) + _nbytes((bk, bn), w.dtype) + _nbytes((bm, bn), out_dtype))
    est += _nbytes((bk, bn), BF16) + 2 * _nbytes((bm, bn), F32)
    if mode == "swiglu":
        in_specs.append(pl.BlockSpec((None, bk, bn), lambda j, b, i, k: (layer, k, col_map2(j))))
        args.append(w)
        est += 2 * _nbytes((bk, bn), F32) + _nbytes((bk, bn), BF16) + _nbytes((bm, bn), F32)
    if mode == "resid":
        tg = gate.shape[1]
        mg = 1 if tg == 1 else bm
        nb = n // bn
        in_specs.append(pl.BlockSpec((None, bm, bn), lambda j, b, i, k: (b, i, j)))
        in_specs.append(pl.BlockSpec(
            (None, mg, bn), lambda j, b, i, k: (b, 0 if tg == 1 else i, gate_chunk * nb + j)))
        args += [res, gate]
        est += 4 * _nbytes((bm, bn), F32)

    out_shape = jax.ShapeDtypeStruct((bn_, t, n), out_dtype)
    out_specs = pl.BlockSpec((None, bm, bn), lambda j, b, i, k: (b, i, j))
    ms = 0
    if xs is not None:
        ms = xs.shape[1]
        in_specs.append(pl.BlockSpec((None, ms, bk), lambda j, b, i, k: (0, 0, k)))
        args.append(xs)
        if mode == "resid":
            nb = n // bn
            in_specs.append(pl.BlockSpec((None, ms, bn), lambda j, b, i, k: (0, 0, j)))
            in_specs.append(pl.BlockSpec((None, ms, bn), lambda j, b, i, k: (0, 0, gate_chunk * nb + j)))
            args += [res_s, gate_s]
        out_shape = (out_shape, jax.ShapeDtypeStruct((1, ms, n), out_dtype))
        out_specs = (out_specs, pl.BlockSpec((None, ms, bn), lambda j, b, i, k: (0, 0, j)))
        est += 2 * _nbytes((ms, bk), BF16) + 8 * _nbytes((ms, bn), F32)

    if side_jobs:
        assert nk == 1
        nt_ = t // bm
        out_shape = tuple(out_shape) if isinstance(out_shape, tuple) else (out_shape,)
        out_specs = tuple(out_specs) if isinstance(out_specs, tuple) else (out_specs,)

        def at_step(index_of_step):
            return lambda j, b, i, k: index_of_step((j * bn_ + b) * nt_ + i)

        for job in side_jobs:
            in_specs += [pl.BlockSpec(blk, at_step(f)) for blk, f in job.in_blocks]
            args += list(job.args)
            out_shape += tuple(job.out_shapes)
            out_specs += tuple(pl.BlockSpec(blk, at_step(f)) for blk, f in job.out_blocks)
            est += job.nbytes

    if nk > 1:
        scratch = [pltpu.VMEM((bm, bn), F32)] + ([pltpu.VMEM((ms, bn), F32)] if ms else [])
    elif w.dtype != BF16:
        scratch = [pltpu.VMEM((bk, bn), BF16)] * (2 if mode == "swiglu" else 1)
    else:
        scratch = []

    return pl.pallas_call(
        functools.partial(_linear_kernel, nk=nk, mode=mode, extra=xs is not None,
                          side_jobs=tuple(side_jobs)),
        out_shape=out_shape,
        grid=grid,
        in_specs=in_specs,
        out_specs=out_specs,
        scratch_shapes=scratch,
        compiler_params=_params(est, 4),
        name=name,
    )(*args)


def _attn_prompt_kernel(*refs, ngroups):
    q_refs, k_refs, v_refs = refs[:ngroups], refs[ngroups:2 * ngroups], refs[2 * ngroups:3 * ngroups]
    o_ref, og_ref, lg_ref, bias_ref = refs[3 * ngroups:]
    seq = o_ref.shape[0]
    scale = HEAD_DIM ** -0.5
    nt = (((1,), (1,)), ((), ()))
    neg = -jnp.inf
    row1 = lax.broadcasted_iota(jnp.int32, (N_BACK, N_BACK), 0)
    col1 = lax.broadcasted_iota(jnp.int32, (N_BACK, N_BACK), 1)
    row2 = lax.broadcasted_iota(jnp.int32, (N_BACK, 2 * N_BACK), 0)
    col2 = lax.broadcasted_iota(jnp.int32, (N_BACK, 2 * N_BACK), 1)
    ok_first = col1 <= row1
    ok_band = jnp.logical_or(jnp.logical_and(col2 < N_BACK, col2 >= row2),
                             jnp.logical_and(col2 >= N_BACK, col2 - N_BACK <= row2))
    bias_ref[:, :N_BACK] = jnp.where(ok_first, 0.0, neg)
    bias_ref[:, N_BACK:] = jnp.where(ok_band, 0.0, neg)
    ones = jnp.ones((N_BACK, HEAD_DIM), BF16)

    for g in range(ngroups):
        dil = DILATIONS[g]
        length = seq // dil
        nblk = length // N_BACK
        for r in range(dil):
            sub_rows = slice(0, length) if dil == 1 else pl.ds(r, length, stride=dil)
            q = q_refs[g][sub_rows, :].astype(BF16)
            k = k_refs[g][sub_rows, :].astype(BF16)
            v = v_refs[g][sub_rows, :].astype(BF16)
            scores = []
            for i in range(nblk):
                qi = q[i * N_BACK:(i + 1) * N_BACK]
                keys = k[max(i - 1, 0) * N_BACK:(i + 1) * N_BACK]
                scores.append(lax.dot_general(qi, keys, nt, preferred_element_type=F32) * scale)
            probs, maxes = [], []
            for i in range(nblk):
                s = scores[i] + (bias_ref[:, :N_BACK] if i == 0 else bias_ref[:, N_BACK:])
                m = jnp.max(s, axis=-1, keepdims=True)
                probs.append(jnp.exp(s - m).astype(BF16))
                maxes.append(m)
            for i in range(nblk):
                vals = v[max(i - 1, 0) * N_BACK:(i + 1) * N_BACK]
                vext = jnp.concatenate([vals, jnp.concatenate([ones] * (vals.shape[0] // N_BACK), axis=0)], axis=1)
                acc = jnp.dot(probs[i], vext, preferred_element_type=F32)
                den = acc[:, HEAD_DIM:]
                start = r + i * N_BACK * dil
                out_rows = slice(start, start + N_BACK) if dil == 1 else pl.ds(start, N_BACK, stride=dil)
                og_ref.at[g][out_rows, :] = acc[:, :HEAD_DIM] / den
                lg_ref.at[g][out_rows, :] = maxes[i] + jnp.log(den)

    rb = MERGE_ROW_TILE
    for c in range(seq // rb):
        rows = slice(c * rb, (c + 1) * rb)
        ls = [lg_ref[g, rows, :] for g in range(ngroups)]
        m = functools.reduce(jnp.maximum, ls)
        es = [jnp.exp(l - m) for l in ls]
        tot = functools.reduce(lambda a, b: a + b, es)
        num = functools.reduce(lambda a, b: a + b, [es[g] * og_ref[g, rows, :] for g in range(ngroups)])
        o_ref[rows, :] = (num / tot).astype(o_ref.dtype)


def _attn_prompt(q, kvs):
    ngroups = len(DILATIONS)
    b, s, qc = q.shape
    hw = qc // ngroups
    nh = hw // HEAD_DIM
    assert all(s % (d * N_BACK) == 0 for d in DILATIONS) and s % MERGE_ROW_TILE == 0
    blk = (None, s, HEAD_DIM)
    in_specs = ([pl.BlockSpec(blk, (lambda bi, h, g=g: (bi, 0, g * nh + h))) for g in range(ngroups)]
                + [pl.BlockSpec(blk, lambda bi, h: (bi, 0, h))] * ngroups
                + [pl.BlockSpec(blk, lambda bi, h: (bi, 0, nh + h))] * ngroups)
    est = (2 * (3 * ngroups * _nbytes((s, HEAD_DIM), F32) + _nbytes((s, HEAD_DIM), BF16))
           + 2 * ngroups * _nbytes((s, HEAD_DIM), F32) + 6 * _nbytes((s, HEAD_DIM), F32))
    return pl.pallas_call(
        functools.partial(_attn_prompt_kernel, ngroups=ngroups),
        out_shape=jax.ShapeDtypeStruct((b, s, hw), BF16),
        grid=(b, nh),
        in_specs=in_specs,
        out_specs=pl.BlockSpec(blk, lambda bi, h: (bi, 0, h)),
        scratch_shapes=[pltpu.VMEM((ngroups, s, HEAD_DIM), F32), pltpu.VMEM((ngroups, s, HEAD_DIM), F32),
                        pltpu.VMEM((N_BACK, 3 * N_BACK), F32)],
        compiler_params=_params(est, 2),
        name="attn_prompt",
    )(*([q] * ngroups), *kvs, *kvs)


def _attn_cached_kernel(*refs, dil, nheads, has_carry, final):
    if has_carry:
        q_ref, kn_ref, vn_ref, kb_ref, vb_ref, m_in, l_in, a_in = refs[:8]
        outs = refs[8:]
    else:
        q_ref, kn_ref, vn_ref, kb_ref, vb_ref = refs[:5]
        outs = refs[5:]
    if final:
        o_ref, m_sc, l_sc, a_sc = outs
    else:
        m_out, l_out, a_out, m_sc, l_sc, a_sc = outs
    c = pl.program_id(1)
    nc = pl.num_programs(1)
    restricted = len(kb_ref.shape) == 4
    res = kb_ref.shape[1] if restricted else 1
    period = dil if restricted else 1
    wc = kb_ref.shape[0] * res
    w_total = kb_ref.shape[0] * period * nc
    nq = q_ref.shape[0]
    hshift = nheads.bit_length() - 1
    rshift = res.bit_length() - 1
    scale = HEAD_DIM ** -0.5
    nt = (((1,), (1,)), ((), ()))
    neg = -jnp.inf
    q = q_ref[...].astype(BF16)

    def attend(k, v, ok):
        s = lax.dot_general(q, k.astype(BF16), nt, preferred_element_type=F32) * scale
        s = jnp.where(ok, s, neg)
        m_old = m_sc[...]
        m_new = jnp.maximum(m_old, jnp.max(s, axis=-1, keepdims=True))
        alpha = jnp.exp(m_old - m_new)
        p = jnp.exp(s - m_new)
        l_sc[...] = alpha * l_sc[...] + jnp.sum(p, axis=-1, keepdims=True)
        a_sc[...] = alpha * a_sc[...] + jnp.dot(p.astype(BF16), v.astype(BF16), preferred_element_type=F32)
        m_sc[...] = m_new

    @pl.when(c == 0)
    def _():
        if has_carry:
            m_sc[...] = m_in[...]
            l_sc[...] = l_in[...]
            a_sc[...] = a_in[...]
        else:
            m_sc[...] = jnp.full(m_sc.shape, neg, F32)
            l_sc[...] = jnp.zeros(l_sc.shape, F32)
            a_sc[...] = jnp.zeros(a_sc.shape, F32)
        qr = lax.broadcasted_iota(jnp.int32, (nq, nq), 0)
        kr = lax.broadcasted_iota(jnp.int32, (nq, nq), 1)
        dt = (qr >> hshift) - (kr >> hshift)
        ok = ((qr & (nheads - 1)) == (kr & (nheads - 1))) & (dt >= 0) & ((dt & (dil - 1)) == 0)
        attend(kn_ref[...], vn_ref[...], ok)

    qr = lax.broadcasted_iota(jnp.int32, (nq, wc * nheads), 0)
    kr = lax.broadcasted_iota(jnp.int32, (nq, wc * nheads), 1)
    held = kr >> hshift
    w_row = (c * (wc // res) + (held >> rshift)) * period + (held & (res - 1))
    dist = w_total + (qr >> hshift) - w_row
    ok = (((qr & (nheads - 1)) == (kr & (nheads - 1))) & ((dist & (dil - 1)) == 0)
          & (dist <= dil * N_BACK))
    attend(kb_ref[...].reshape(wc * nheads, HEAD_DIM), vb_ref[...].reshape(wc * nheads, HEAD_DIM), ok)

    @pl.when(c == nc - 1)
    def _():
        if final:
            o_ref[...] = a_sc[...] / l_sc[...]
        else:
            m_out[...] = m_sc[...]
            l_out[...] = l_sc[...]
            a_out[...] = a_sc[...]


def _attn_cached(qf, knf, vnf, buf, g, carry, *, final, wc):
    dil = DILATIONS[g]
    b, nq, hd = qf.shape
    w, nheads = buf.shape[1], buf.shape[3]
    t_new = nq // nheads
    assert w % wc == 0 and nheads & (nheads - 1) == 0 and dil & (dil - 1) == 0
    row = pl.BlockSpec((None, nq, hd), lambda bi, c: (bi, 0, 0))
    stat = pl.BlockSpec((None, nq, 1), lambda bi, c: (bi, 0, 0))
    if dil > t_new and t_new & (t_new - 1) == 0 and w % dil == 0 and wc % dil == 0:
        wc = min(w, wc * dil // t_new)
        buf = buf.reshape(b, w // dil, dil, 2, nheads, hd)
        kv_blk = (None, wc // dil, t_new, None, nheads, hd)
        kv_specs = [pl.BlockSpec(kv_blk, lambda bi, c: (bi, c, 0, 0, 0, 0)),
                    pl.BlockSpec(kv_blk, lambda bi, c: (bi, c, 0, 1, 0, 0))]
        held = wc // dil * t_new
    else:
        kv_blk = (None, wc, None, nheads, hd)
        kv_specs = [pl.BlockSpec(kv_blk, lambda bi, c: (bi, c, 0, 0, 0)),
                    pl.BlockSpec(kv_blk, lambda bi, c: (bi, c, 1, 0, 0))]
        held = wc
    in_specs = [row, row, row] + kv_specs
    args = [qf, knf, vnf, buf, buf]
    if carry is not None:
        in_specs += [stat, stat, row]
        args += list(carry)
    if final:
        out_shape = jax.ShapeDtypeStruct((b, nq, hd), F32)
        out_specs = row
    else:
        out_shape = (jax.ShapeDtypeStruct((b, nq, 1), F32), jax.ShapeDtypeStruct((b, nq, 1), F32),
                     jax.ShapeDtypeStruct((b, nq, hd), F32))
        out_specs = (stat, stat, row)
    est = (4 * _nbytes((held, nheads, hd), F32) + 4 * _nbytes((nq, held * nheads), F32)
           + 16 * _nbytes((nq, hd), F32))
    return pl.pallas_call(
        functools.partial(_attn_cached_kernel, dil=dil, nheads=nheads, has_carry=carry is not None, final=final),
        out_shape=out_shape,
        grid=(b, w // wc),
        in_specs=in_specs,
        out_specs=out_specs,
        scratch_shapes=[pltpu.VMEM((nq, 1), F32), pltpu.VMEM((nq, 1), F32), pltpu.VMEM((nq, hd), F32)],
        compiler_params=_params(est, 2),
        name=f"attn_cached_g{g}",
    )(*args)


def _shift_block(cur_ref, nxt_ref, new_ref, o_ref, last):
    rb, t_new = cur_ref.shape[0], new_ref.shape[0]
    o_ref[0:rb - t_new] = cur_ref[t_new:rb]
    o_ref[rb - t_new:rb] = jnp.where(last, new_ref[...], nxt_ref[...])


def _shift_kernel(cur_ref, nxt_ref, new_ref, o_ref):
    _shift_block(cur_ref, nxt_ref, new_ref, o_ref, pl.program_id(1) == pl.num_programs(1) - 1)


def _shift_job(buf, new, steps, *, rb):
    b, w, r, lanes = buf.shape
    t_new = new.shape[1]
    if w % rb or rb % t_new or rb <= t_new or b * (w // rb) > steps:
        return None
    nblk = w // rb
    nact = b * nblk
    last_blk = w // t_new - 1

    def where(s):
        s = jnp.minimum(s, nact - 1)
        return lax.div(s, nblk), lax.rem(s, nblk)

    def body(step, ins, outs):
        @pl.when(step < nact)
        def _():
            _shift_block(*ins, outs[0], lax.rem(step, nblk) == nblk - 1)

    blk = (None, rb, r, lanes)
    small = (None, t_new, r, lanes)
    return _SideJob(
        args=(buf, buf, new),
        in_blocks=((blk, lambda s: (*where(s), 0, 0)),
                   (small, lambda s: (where(s)[0],
                                      jnp.minimum((where(s)[1] + 1) * (rb // t_new), last_blk), 0, 0)),
                   (small, lambda s: (where(s)[0], 0, 0, 0))),
        out_shapes=(jax.ShapeDtypeStruct(buf.shape, F32),),
        out_blocks=((blk, lambda s: (*where(s), 0, 0)),),
        body=body,
        nbytes=4 * _nbytes((rb, r, lanes), F32) + 4 * _nbytes((t_new, r, lanes), F32))


def _shift_append(buf, new, *, rb):
    b, w, r, lanes = buf.shape
    t_new = new.shape[1]
    assert w % rb == 0 and rb % t_new == 0 and rb > t_new
    last_blk = w // t_new - 1
    est = 4 * _nbytes((rb, r, lanes), F32) + 6 * _nbytes((t_new, r, lanes), F32)
    return pl.pallas_call(
        _shift_kernel,
        out_shape=jax.ShapeDtypeStruct(buf.shape, F32),
        grid=(b, w // rb),
        in_specs=[
            pl.BlockSpec((None, rb, r, lanes), lambda bi, j: (bi, j, 0, 0)),
            pl.BlockSpec((None, t_new, r, lanes),
                         lambda bi, j: (bi, jnp.minimum((j + 1) * (rb // t_new), last_blk), 0, 0)),
            pl.BlockSpec((None, t_new, r, lanes), lambda bi, j: (bi, 0, 0, 0)),
        ],
        out_specs=pl.BlockSpec((None, rb, r, lanes), lambda bi, j: (bi, j, 0, 0)),
        compiler_params=_params(est, 2),
        name="kv_shift_append",
    )(buf, buf, new)


def _hgrn_kernel(*refs, heads, chunk, layer, has_state):
    if has_state:
        zq_ref, zf_ref, zv_ref, zg_ref, lb_ref, ng_ref, s0_ref = refs[:7]
        rest = refs[7:]
    else:
        zq_ref, zf_ref, zv_ref, zg_ref, lb_ref, ng_ref = refs[:6]
        s0_ref = None
        rest = refs[6:]
    y_ref, st_out_ref, st_ref = rest[:3]
    bc_ref = rest[3] if len(rest) > 3 else None
    tstep = pl.program_id(2)
    nsteps = pl.num_programs(2)
    tb = zq_ref.shape[0]

    @pl.when(tstep == 0)
    def _():
        for h in range(heads):
            if has_state:
                st_ref[h] = s0_ref[h].T
            else:
                st_ref[h] = jnp.zeros((HGRN_DK, HGRN_DK), F32)

    logits = lb_ref[...]
    ex = jnp.exp(logits - jnp.max(logits, axis=0, keepdims=True))
    sm = ex / jnp.sum(ex, axis=0, keepdims=True)
    lb = jnp.zeros_like(sm[0:1])
    for l in range(1, layer + 1):
        lb = lb + sm[l:l + 1]
    ng = ng_ref[...]

    sub = V7X_SUBLANES
    rblk = sub if chunk % sub == 0 else chunk
    nblk = chunk // rblk
    rowi = lax.broadcasted_iota(jnp.int32, (rblk, HGRN_DK), 0)
    rowi_all = lax.broadcasted_iota(jnp.int32, (chunk, zq_ref.shape[1]), 0)
    nt = (((1,), (1,)), ((), ()))
    tn = (((0,), (0,)), ((), ()))

    def body(ci, carry):
        rows = pl.ds(pl.multiple_of(ci * chunk, chunk), chunk)
        zq = zq_ref[rows, :]
        zf = zf_ref[rows, :]
        zv = zv_ref[rows, :]
        zg = zg_ref[rows, :]
        q_all = _silu(zq)
        f_all = lb + (1.0 - lb) * jax.nn.sigmoid(zf)
        lf_all = jnp.log(f_all)
        k_all = 1.0 - f_all
        og_all = ng * _silu(zg)
        if chunk % sub == 0:
            a_all = lf_all
            shift = 1
            while shift < chunk:
                a_all = a_all + jnp.where(rowi_all >= shift, pltpu.roll(a_all, shift, 0), 0.0)
                shift *= 2
        else:
            a_all = jnp.zeros_like(lf_all)
            for s in range(chunk):
                a_all = a_all + jnp.where(rowi_all >= s, lf_all[s:s + 1, :], 0.0)
        a_all = a_all * LOG2_E
        qe_all = q_all * jnp.exp2(a_all)
        a_last_all = a_all[chunk - 1:chunk, :]
        kd_all = k_all * jnp.exp2(a_last_all - a_all)
        dec_all = jnp.exp2(a_last_all)
        if bc_ref is not None:
            for h in range(heads):
                lanes = slice(h * HGRN_DK, (h + 1) * HGRN_DK)
                bc_ref[0, h] = a_all[:, lanes]
                bc_ref[1, h] = k_all[:, lanes]
                bc_ref[2, h] = zv[:, lanes]
        o_inter, scores, states_in = [], [], []
        for h in range(heads):
            lanes = slice(h * HGRN_DK, (h + 1) * HGRN_DK)
            q, k, a = q_all[:, lanes], k_all[:, lanes], a_all[:, lanes]
            st = st_ref[h]
            states_in.append(st)
            o_inter.append(lax.dot_general(qe_all[:, lanes].astype(BF16), st.astype(BF16), nt,
                                           preferred_element_type=F32))
            sc = None
            for j in range(nblk - 1):
                r0, r1 = j * rblk, (j + 1) * rblk
                b_j = a[r1 - 1:r1, :]
                qt = jnp.concatenate([jnp.zeros((r1, HGRN_DK), F32), q[r1:] * jnp.exp2(a[r1:] - b_j)], axis=0)
                kh = [k[r0:r1] * jnp.exp2(b_j - a[r0:r1])]
                if r0 > 0:
                    kh.insert(0, jnp.zeros((r0, HGRN_DK), F32))
                kh.append(jnp.zeros((chunk - r1, HGRN_DK), F32))
                sc_j = lax.dot_general(qt.astype(BF16), jnp.concatenate(kh, axis=0).astype(BF16), nt,
                                       preferred_element_type=F32)
                sc = sc_j if sc is None else sc + sc_j
            scores.append(sc)
        for h in range(heads):
            lanes = slice(h * HGRN_DK, (h + 1) * HGRN_DK)
            q, k, v, a = q_all[:, lanes], k_all[:, lanes], zv[:, lanes], a_all[:, lanes]
            st = states_in[h]
            o = o_inter[h]
            if scores[h] is not None:
                o = o + jnp.dot(scores[h].astype(BF16), v.astype(BF16), preferred_element_type=F32)
            ob = [None] * nblk
            for s in range(chunk):
                bi = s // rblk
                r0 = bi * rblk
                if bc_ref is not None:
                    a_s, k_s, v_s = (jnp.broadcast_to(bc_ref[c, h, s:s + 1, :], (rblk, HGRN_DK))
                                     for c in range(3))
                else:
                    a_s, k_s, v_s = a[s:s + 1, :], k[s:s + 1, :], v[s:s + 1, :]
                e = jnp.where(rowi >= s - r0, jnp.exp2(a[r0:r0 + rblk] - a_s), 0.0)
                wgt = (q[r0:r0 + rblk] * k_s) * e
                term = jnp.sum(wgt, axis=-1, keepdims=True) * v_s
                ob[bi] = term if ob[bi] is None else ob[bi] + term
            o = o + (ob[0] if nblk == 1 else jnp.concatenate(ob, axis=0))
            upd = lax.dot_general(v.astype(BF16), kd_all[:, lanes].astype(BF16), tn,
                                  preferred_element_type=F32)
            st_ref[h] = st * dec_all[:, lanes] + upd
            on = o * lax.rsqrt(jnp.mean(o * o, axis=-1, keepdims=True) + EPS)
            y_ref[rows, lanes] = (on * og_all[:, lanes]).astype(y_ref.dtype)
        return carry

    lax.fori_loop(0, tb // chunk, body, 0)

    @pl.when(tstep == nsteps - 1)
    def _():
        for h in range(heads):
            st_out_ref[h] = st_ref[h].T


def _hgrn(z, lb_logits, norm_g, s0, *, layer, heads, tb, out_dtype):
    b, t, d4 = z.shape
    d = d4 // 4
    nh = d // HGRN_DK
    nhg = nh // heads
    hb = heads * HGRN_DK
    chunk = math.gcd(t, HGRN_CHUNK)
    depth = lb_logits.shape[0]
    assert t % tb == 0 and tb % chunk == 0 and nh % heads == 0

    def zspec(part):
        return pl.BlockSpec((None, tb, hb), lambda bi, hg, i: (bi, i, part * nhg + hg))

    in_specs = [zspec(0), zspec(1), zspec(2), zspec(3),
                pl.BlockSpec((depth, hb), lambda bi, hg, i: (0, hg)),
                pl.BlockSpec((1, hb), lambda bi, hg, i: (0, hg))]
    args = [z, z, z, z, lb_logits, norm_g.reshape(1, d)]
    sspec = pl.BlockSpec((None, heads, HGRN_DK, HGRN_DK), lambda bi, hg, i: (bi, hg, 0, 0))
    if s0 is not None:
        in_specs.append(sspec)
        args.append(s0)
    est = (2 * (4 * _nbytes((tb, hb), F32) + _nbytes((tb, hb), out_dtype))
           + 5 * _nbytes((heads, HGRN_DK, HGRN_DK), F32))
    return pl.pallas_call(
        functools.partial(_hgrn_kernel, heads=heads, chunk=chunk, layer=layer, has_state=s0 is not None),
        out_shape=(jax.ShapeDtypeStruct((b, t, d), out_dtype),
                   jax.ShapeDtypeStruct((b, nh, HGRN_DK, HGRN_DK), F32)),
        grid=(b, nhg, t // tb),
        in_specs=in_specs,
        out_specs=(pl.BlockSpec((None, tb, hb), lambda bi, hg, i: (bi, i, hg)), sspec),
        scratch_shapes=([pltpu.VMEM((heads, HGRN_DK, HGRN_DK), F32)]
                        + ([pltpu.VMEM((3, heads, chunk, HGRN_DK), F32)] if chunk % V7X_SUBLANES == 0 else [])),
        compiler_params=_params(est, 3),
        name="hgrn2",
    )(*args)


def _forward(xp, xs, c_all, ada_w, ada_b, a_bufs, b_states, weights, *, nb, t_seq):
    (norm_mix_g, norm_ffn_g, a_w_qkv, a_w_o, b_w_in, b_lb_logits, b_norm_g, b_w_o,
     ffn_w_in, ffn_w_out, final_g) = weights
    bp, t, d = xp.shape
    ms = xs.shape[1]
    depth = norm_mix_g.shape[0]
    d_ff = ffn_w_out.shape[1]
    ngroups = len(DILATIONS)
    hw = a_w_o.shape[1]
    nheads = hw // HEAD_DIM
    bm = _pick(t, ROW_TILES)
    bm_e = _pick(t, NORM_ROW_TILES)
    bn = N_TILE
    ident = lambda j: j
    kv_p, kv_s, st_p, st_s = [], [], [], []
    w_dn_all = None
    bnf = N_TILE_SWIGLU
    nff = d_ff // bnf
    ffn_steps = nff * bp * (t // bm)
    ada_b3 = ada_b.reshape(depth, 1, 6 * d)
    ride = depth > 1 and _ada_job(c_all, ada_w, ada_b3, 1, ffn_steps) is not None
    mod0 = _ada_modulation(c_all, ada_w, ada_b3, nlayers=1 if ride else depth, bn=ADA_N_TILE)
    mod_rows = {l: mod0[l] for l in range(mod0.shape[0])}
    pending_shifts = []

    for layer in range(depth):
        i = layer // N_MIXERS
        mod_l = mod_rows[layer]
        mp = mod_l[:bp].reshape(bp, 1, 6 * d)
        msd = jnp.repeat(mod_l[bp:bp + nb], t_seq, axis=0).reshape(1, ms, 6 * d)
        hp = _norm_mod(xp, norm_mix_g[layer], mp, 1, 0, bm=bm_e)
        hs = _norm_mod(xs, norm_mix_g[layer], msd, 1, 0, bm=ms)
        if layer % N_MIXERS == 0:
            qp, qs = _linear(hp, a_w_qkv, layer=i, n=ngroups * hw, col_map=ident, out_dtype=F32,
                             bm=bm, bn=bn, xs=hs, name="a_q")
            half = hw // bn
            q5 = qs.reshape(nb, t_seq, ngroups, nheads, HEAD_DIM)
            flat = lambda a: a.reshape(nb, t_seq * nheads, HEAD_DIM)
            kvs = []
            carry = None
            for g in range(ngroups):
                def kv_cols(j, g=g):
                    return jnp.where(j < half, (ngroups + g) * half + j, (2 * ngroups + g) * half + (j - half))
                kvp, kvn = _linear(hp, a_w_qkv, layer=i, n=2 * hw, col_map=kv_cols, out_dtype=F32,
                                   bm=bm, bn=bn, xs=hs, name=f"a_kv{g}")
                kvs.append(kvp)
                keep = min(DILATIONS[g] * N_BACK, t)
                kv_keep = kvp if keep == t else kvp[:, t - keep:]
                kv_p.append(kv_keep.reshape(1, bp, keep, 2, nheads, HEAD_DIM))
                buf = a_bufs[g][i]
                w = buf.shape[1]
                kvn = kvn.reshape(nb, t_seq, 2, nheads, HEAD_DIM)
                carry = _attn_cached(flat(q5[:, :, g]), flat(kvn[:, :, 0]), flat(kvn[:, :, 1]), buf, g,
                                     carry, final=g == ngroups - 1, wc=min(w, WINDOW_CHUNK))
                pending_shifts.append((buf.reshape(nb, w, 2 * nheads, HEAD_DIM),
                                       kvn.reshape(nb, t_seq, 2 * nheads, HEAD_DIM), len(kv_s)))
                kv_s.append(None)
            yp = _attn_prompt(qp, kvs)
            ys = carry.reshape(1, ms, hw).astype(BF16)
            w_o = a_w_o
        else:
            zp, zs = _linear(hp, b_w_in, layer=i, n=4 * d, col_map=ident, out_dtype=F32, bm=bm, bn=bn,
                             xs=hs, name="b_in")
            hg_heads = _pick(d // HGRN_DK, HGRN_HEADS_PER_STEP)
            yp, s_p = _hgrn(zp, b_lb_logits, b_norm_g[i], None, layer=layer, heads=hg_heads,
                            tb=min(t, HGRN_ROW_TILE), out_dtype=BF16)
            ysb, s_s = _hgrn(zs.reshape(nb, t_seq, 4 * d), b_lb_logits, b_norm_g[i], b_states[i],
                             layer=layer, heads=hg_heads, tb=t_seq, out_dtype=F32)
            ys = ysb.reshape(1, ms, d).astype(BF16)
            st_p.append(s_p)
            st_s.append(s_s)
            w_o = b_w_o
        xp, xs = _linear(yp, w_o, layer=i, n=d, col_map=ident, out_dtype=F32, bm=bm, bn=bn,
                         mode="resid", res=xp, gate=mp, gate_chunk=2, xs=ys, res_s=xs, gate_s=msd,
                         name="mix_o")
        hp = _norm_mod(xp, norm_ffn_g[layer], mp, 4, 3, bm=bm_e)
        hs = _norm_mod(xs, norm_ffn_g[layer], msd, 4, 3, bm=ms)
        jobs, roles = [], []
        cast = _cast_job(ffn_w_out, layer, ffn_steps)
        if cast is not None:
            jobs.append(cast)
            roles.append(("w_dn", None))
        if ride and layer + 1 < depth:
            jobs.append(_ada_job(c_all, ada_w, ada_b3, layer + 1, ffn_steps))
            roles.append(("mod", layer + 1))
        else:
            for buf4, new4, slot in pending_shifts:
                job = next((jb for jb in (_shift_job(buf4, new4, ffn_steps, rb=rb)
                                          for rb in SIDE_SHIFT_ROWS if rb <= buf4.shape[1])
                            if jb is not None), None)
                if job is not None:
                    jobs.append(job)
                    roles.append(("shift", slot))
            pending_shifts = [p for p in pending_shifts if ("shift", p[2]) not in roles]
        outs = _linear(hp, ffn_w_in, layer=layer, n=d_ff, col_map=ident, col_map2=lambda j: nff + j,
                       out_dtype=BF16, bm=bm, bn=bnf, mode="swiglu", xs=hs, side_jobs=jobs, name="ffn_in")
        actp, acts = outs[0], outs[1]
        w_dn = None
        for (role, key), val in zip(roles, outs[2:]):
            if role == "w_dn":
                w_dn, dn_layer = val[None], 0
            elif role == "mod":
                mod_rows[key] = val
            else:
                kv_s[key] = val.reshape(1, nb, val.shape[1], 2, nheads, HEAD_DIM)
        if w_dn is None:
            if w_dn_all is None:
                w_dn_all = _to_bf16(ffn_w_out, kb=_pick(d_ff, (d_ff // 16, d_ff // 8, d_ff // 4, d_ff // 2)))
            w_dn, dn_layer = w_dn_all, layer
        xp, xs = _linear(actp, w_dn, layer=dn_layer, n=d, col_map=ident, out_dtype=F32,
                         bm=_pick(t, ROW_TILES_LONG_K), bn=bn, mode="resid", res=xp, gate=mp, gate_chunk=5,
                         xs=acts, res_s=xs, gate_s=msd, name="ffn_out")
    for buf4, new4, slot in pending_shifts:
        w = buf4.shape[1]
        kv_s[slot] = _shift_append(buf4, new4, rb=min(w, WINDOW_CHUNK)).reshape(1, nb, w, 2, nheads, HEAD_DIM)
    yp = _final_norm(xp, final_g, bm=bm_e)
    ys = _final_norm(xs, final_g, bm=ms)
    return yp, ys, kv_p, kv_s, st_p, st_s


def kernel(x_prompt, x_sample, state_a_kv_w128, state_a_kv_w512, state_a_kv_w2048, state_b_rec,
           c_prompt, c_sample, ada_w, ada_b, norm_mix_g, norm_ffn_g, a_w_qkv, a_w_o,
           b_w_in, b_lb_logits, b_norm_g, b_w_o, ffn_w_in, ffn_w_out, final_g):
    bp, sp, d = x_prompt.shape
    bs, ts, _ = x_sample.shape
    depth = ada_w.shape[0]
    weights = (norm_mix_g, norm_ffn_g, a_w_qkv, a_w_o, b_w_in, b_lb_logits, b_norm_g, b_w_o,
               ffn_w_in, ffn_w_out, final_g)

    rpad = -(bp + bs) % V7X_SUBLANES
    c_all = jnp.pad(jnp.concatenate([c_prompt, c_sample], axis=0), ((0, rpad), (0, 0)))
    y_p, y_s, kv_p, kv_s, st_p, st_s = _forward(
        x_prompt, x_sample.reshape(1, bs * ts, d), c_all, ada_w, ada_b,
        (state_a_kv_w128, state_a_kv_w512, state_a_kv_w2048), state_b_rec, weights, nb=bs, t_seq=ts)
    return (y_p, y_s.reshape(bs, ts, d), kv_p[0], kv_s[0], kv_p[1], kv_s[1], kv_p[2], kv_s[2],
            jnp.stack(st_p, axis=0), jnp.stack(st_s, axis=0))
```

```python
import functools
import math
from typing import Callable, NamedTuple

import jax
import jax.numpy as jnp
from jax import lax
from jax.experimental import pallas as pl
from jax.experimental.pallas import tpu as pltpu

F32 = jnp.float32
BF16 = jnp.bfloat16

EPS = 1e-6
DILATIONS = (1, 4, 16)
N_BACK = 128
HEAD_DIM = 128
HGRN_DK = 128
HGRN_CHUNK = 16
LOG2_E = 1.4426950408889634
N_MIXERS = 2
V7X_LANES = 128
V7X_SUBLANES = 8
V7X_SCOPED_VMEM_BYTES = 60000 * 1024
V7X_MIN_SCOPED_VMEM_BYTES = 16 * 1024 * 1024

ROW_TILES = (1024, 512, 256, 128, 64, 32, 16, 8)
ROW_TILES_LONG_K = ROW_TILES[1:]
NORM_ROW_TILES = (512, 256, 128, 64, 32, 16, 8)
N_TILE = 512
N_TILE_SHORT_K = 1024
SHORT_K = 2048
N_TILE_SWIGLU = 256
ADA_N_TILE = 512
HGRN_HEADS_PER_STEP = (16, 8, 4, 2, 1)
HGRN_ROW_TILE = 512
WINDOW_CHUNK = 256
SIDE_SHIFT_ROWS = (64, 128, 256, 512, 1024)
MERGE_ROW_TILE = 256


def _params(nbytes, n_axes):
    limit = int(min(V7X_SCOPED_VMEM_BYTES, max(2 * nbytes, V7X_MIN_SCOPED_VMEM_BYTES)))
    return pltpu.CompilerParams(dimension_semantics=("arbitrary",) * n_axes, vmem_limit_bytes=limit)


def _nbytes(shape, dtype):
    return math.prod(shape) * jnp.dtype(dtype).itemsize


def _silu(x):
    return x * jax.nn.sigmoid(x)


def _pick(total, prefs):
    for p in prefs:
        if total % p == 0:
            return p
    return total


class _SideJob(NamedTuple):
    args: tuple
    in_blocks: tuple
    out_shapes: tuple
    out_blocks: tuple
    body: Callable
    nbytes: int

    @property
    def n_in(self):
        return len(self.in_blocks)

    @property
    def n_out(self):
        return len(self.out_blocks)


def _ada_kernel(c_ref, w_ref, b_ref, o_ref):
    s = _silu(c_ref[...])
    acc = jnp.dot(s.astype(BF16), w_ref[...].astype(BF16), preferred_element_type=F32)
    o_ref[...] = acc + b_ref[...]


def _ada_job(c_all, ada_w, ada_b3, layer, steps):
    _, d, n = ada_w.shape
    r = c_all.shape[0]
    cw = next((c for c in range(V7X_LANES, 4 * V7X_LANES + 1, V7X_LANES) if n % c == 0 and n // c <= steps), None)
    if cw is None:
        return None
    nact = n // cw
    col = lambda s: jnp.minimum(s, nact - 1)

    def body(step, ins, outs):
        @pl.when(step < nact)
        def _():
            _ada_kernel(*ins, outs[0])

    return _SideJob(
        args=(c_all, ada_w, ada_b3),
        in_blocks=(((r, d), lambda s: (0, 0)),
                   ((None, d, cw), lambda s: (layer, 0, col(s))),
                   ((None, 1, cw), lambda s: (layer, 0, col(s)))),
        out_shapes=(jax.ShapeDtypeStruct((r, n), F32),),
        out_blocks=(((r, cw), lambda s: (0, col(s))),),
        body=body,
        nbytes=2 * (_nbytes((d, cw), F32) + _nbytes((r, cw), F32)) + _nbytes((d, cw), BF16) + _nbytes((r, d), F32))


def _ada_modulation(c_all, ada_w, ada_b3, *, nlayers, bn):
    _, d, n = ada_w.shape
    depth = nlayers
    r = c_all.shape[0]
    est = 2 * (_nbytes((d, bn), F32) + _nbytes((r, bn), F32)) + _nbytes((d, bn), BF16) + _nbytes((r, d), F32)
    return pl.pallas_call(
        _ada_kernel,
        out_shape=jax.ShapeDtypeStruct((depth, r, n), F32),
        grid=(depth, n // bn),
        in_specs=[
            pl.BlockSpec((r, d), lambda l, j: (0, 0)),
            pl.BlockSpec((None, d, bn), lambda l, j: (l, 0, j)),
            pl.BlockSpec((None, 1, bn), lambda l, j: (l, 0, j)),
        ],
        out_specs=pl.BlockSpec((None, r, bn), lambda l, j: (l, 0, j)),
        compiler_params=_params(est, 2),
        name="ada_modulation",
    )(c_all, ada_w, ada_b3)


def _norm_mod_kernel(x_ref, g_ref, sc_ref, sh_ref, o_ref):
    x = x_ref[...]
    y = x * lax.rsqrt(jnp.mean(x * x, axis=-1, keepdims=True) + EPS)
    h = (y * g_ref[...]) * (1.0 + sc_ref[...]) + sh_ref[...]
    o_ref[...] = h.astype(o_ref.dtype)


def _norm_mod(x, g, mod, sc_chunk, sh_chunk, *, bm):
    bn_, t, d = x.shape
    tg = mod.shape[1]
    mg = 1 if tg == 1 else bm

    def mod_map(chunk):
        return lambda b, i: (b, 0 if tg == 1 else i, chunk)

    est = 2 * (_nbytes((bm, d), F32) + _nbytes((bm, d), BF16)) + 2 * _nbytes((bm, d), F32)
    return pl.pallas_call(
        _norm_mod_kernel,
        out_shape=jax.ShapeDtypeStruct((bn_, t, d), BF16),
        grid=(bn_, t // bm),
        in_specs=[
            pl.BlockSpec((None, bm, d), lambda b, i: (b, i, 0)),
            pl.BlockSpec((1, d), lambda b, i: (0, 0)),
            pl.BlockSpec((None, mg, d), mod_map(sc_chunk)),
            pl.BlockSpec((None, mg, d), mod_map(sh_chunk)),
        ],
        out_specs=pl.BlockSpec((None, bm, d), lambda b, i: (b, i, 0)),
        compiler_params=_params(est, 2),
        name="norm_mod",
    )(x, g.reshape(1, d), mod, mod)


def _final_norm_kernel(x_ref, g_ref, o_ref):
    x = x_ref[...]
    y = x * lax.rsqrt(jnp.mean(x * x, axis=-1, keepdims=True) + EPS)
    o_ref[...] = y * g_ref[...]


def _final_norm(x, g, *, bm):
    bn_, t, d = x.shape
    est = 6 * _nbytes((bm, d), F32)
    return pl.pallas_call(
        _final_norm_kernel,
        out_shape=jax.ShapeDtypeStruct((bn_, t, d), F32),
        grid=(bn_, t // bm),
        in_specs=[
            pl.BlockSpec((None, bm, d), lambda b, i: (b, i, 0)),
            pl.BlockSpec((1, d), lambda b, i: (0, 0)),
        ],
        out_specs=pl.BlockSpec((None, bm, d), lambda b, i: (b, i, 0)),
        compiler_params=_params(est, 2),
        name="final_norm",
    )(x, g.reshape(1, d))


def _to_bf16_kernel(w_ref, o_ref):
    o_ref[...] = w_ref[...].astype(BF16)


def _to_bf16(w, *, kb):
    l, k, n = w.shape
    assert k % kb == 0
    est = 2 * (_nbytes((kb, n), F32) + _nbytes((kb, n), BF16))
    spec = pl.BlockSpec((None, kb, n), lambda li, i: (li, i, 0))
    return pl.pallas_call(
        _to_bf16_kernel,
        out_shape=jax.ShapeDtypeStruct(w.shape, BF16),
        grid=(l, k // kb),
        in_specs=[spec],
        out_specs=spec,
        compiler_params=_params(est, 2),
        name="weight_to_bf16",
    )(w)


def _cast_job(w, layer, steps):
    kc, nc = w.shape[1:]
    if kc % steps or (kc // steps) % 16:
        return None
    slab = kc // steps

    def body(step, ins, outs):
        _to_bf16_kernel(ins[0], outs[0])

    return _SideJob(
        args=(w,),
        in_blocks=(((None, slab, nc), lambda s: (layer, s, 0)),),
        out_shapes=(jax.ShapeDtypeStruct((kc, nc), BF16),),
        out_blocks=(((slab, nc), lambda s: (s, 0)),),
        body=body,
        nbytes=2 * (_nbytes((slab, nc), F32) + _nbytes((slab, nc), BF16)))


def _linear_kernel(*refs, nk, mode, extra, side_jobs):
    refs = list(refs)
    x_ref, w_ref = refs[0], refs[1]
    pos = 2
    w2_ref = res_ref = gate_ref = xs_ref = ress_ref = gates_ref = os_ref = None
    if mode == "swiglu":
        w2_ref = refs[pos]
        pos += 1
    if mode == "resid":
        res_ref, gate_ref = refs[pos], refs[pos + 1]
        pos += 2
    if extra:
        xs_ref = refs[pos]
        pos += 1
        if mode == "resid":
            ress_ref, gates_ref = refs[pos], refs[pos + 1]
            pos += 2
    job_ins = []
    for job in side_jobs:
        job_ins.append(refs[pos:pos + job.n_in])
        pos += job.n_in
    o_ref = refs[pos]
    pos += 1
    if extra:
        os_ref = refs[pos]
        pos += 1
    step = (pl.program_id(0) * pl.num_programs(1) + pl.program_id(1)) * pl.num_programs(2) + pl.program_id(2)
    for job, ins in zip(side_jobs, job_ins):
        job.body(step, ins, refs[pos:pos + job.n_out])
        pos += job.n_out
    scratch = refs[pos:]

    def finalize(out_ref, r_ref, g_ref, acc, acc2=None):
        if mode == "swiglu":
            out_ref[...] = (_silu(acc) * acc2).astype(out_ref.dtype)
        elif mode == "resid":
            out_ref[...] = (r_ref[...] + g_ref[...] * acc).astype(out_ref.dtype)
        else:
            out_ref[...] = acc.astype(out_ref.dtype)

    cast = w_ref.dtype != BF16
    first = jnp.logical_and(pl.program_id(1) == 0, pl.program_id(2) == 0)
    if nk == 1:
        wb_ref, wb2_ref = w_ref, w2_ref
        if cast:
            wb_ref = scratch[0]
            wb2_ref = scratch[1] if mode == "swiglu" else None

        def first_step():
            if cast:
                wb_ref[...] = w_ref[...].astype(BF16)
                if mode == "swiglu":
                    wb2_ref[...] = w2_ref[...].astype(BF16)
            if extra:
                xs = xs_ref[...]
                accs = jnp.dot(xs, wb_ref[...], preferred_element_type=F32)
                accs2 = None
                if mode == "swiglu":
                    accs2 = jnp.dot(xs, wb2_ref[...], preferred_element_type=F32)
                finalize(os_ref, ress_ref, gates_ref, accs, accs2)

        if cast or extra:
            pl.when(first)(first_step)

        x = x_ref[...]
        acc = jnp.dot(x, wb_ref[...], preferred_element_type=F32)
        acc2 = None
        if mode == "swiglu":
            acc2 = jnp.dot(x, wb2_ref[...], preferred_element_type=F32)
        finalize(o_ref, res_ref, gate_ref, acc, acc2)
    else:
        acc_ref = scratch[0]
        k = pl.program_id(3)
        w = w_ref[...].astype(BF16) if cast else w_ref[...]
        part = jnp.dot(x_ref[...], w, preferred_element_type=F32)

        @pl.when(k == 0)
        def _():
            acc_ref[...] = part

        @pl.when(k > 0)
        def _():
            acc_ref[...] += part

        @pl.when(k == nk - 1)
        def _():
            finalize(o_ref, res_ref, gate_ref, acc_ref[...])

        if extra:
            accs_ref = scratch[1]

            @pl.when(first)
            def _():
                parts = jnp.dot(xs_ref[...], w, preferred_element_type=F32)

                @pl.when(k == 0)
                def _():
                    accs_ref[...] = parts

                @pl.when(k > 0)
                def _():
                    accs_ref[...] += parts

                @pl.when(k == nk - 1)
                def _():
                    finalize(os_ref, ress_ref, gates_ref, accs_ref[...])


def _linear(x, w, *, layer, n, col_map, out_dtype, bm, bn, bk=None, mode="plain",
            col_map2=None, res=None, gate=None, gate_chunk=0, xs=None, res_s=None, gate_s=None,
            side_jobs=(), name="linear"):
    bn_, t, kdim = x.shape
    bk = kdim if bk is None else bk
    nk = kdim // bk
    assert kdim % bk == 0 and t % bm == 0 and n % bn == 0
    assert mode != "swiglu" or nk == 1
    grid = (n // bn, bn_, t // bm, nk)

    in_specs = [
        pl.BlockSpec((None, bm, bk), lambda j, b, i, k: (b, i, k)),
        pl.BlockSpec((None, bk, bn), lambda j, b, i, k: (layer, k, col_map(j))),
    ]
    args = [x, w]
    est = 2 * (_nbytes((bm, bk), BF16) + _nbytes((bk, bn), w.dtype) + _nbytes((bm, bn), out_dtype))
    est += _nbytes((bk, bn), BF16) + 2 * _nbytes((bm, bn), F32)
    if mode == "swiglu":
        in_specs.append(pl.BlockSpec((None, bk, bn), lambda j, b, i, k: (layer, k, col_map2(j))))
        args.append(w)
        est += 2 * _nbytes((bk, bn), F32) + _nbytes((bk, bn), BF16) + _nbytes((bm, bn), F32)
    if mode == "resid":
        tg = gate.shape[1]
        mg = 1 if tg == 1 else bm
        nb = n // bn
        in_specs.append(pl.BlockSpec((None, bm, bn), lambda j, b, i, k: (b, i, j)))
        in_specs.append(pl.BlockSpec(
            (None, mg, bn), lambda j, b, i, k: (b, 0 if tg == 1 else i, gate_chunk * nb + j)))
        args += [res, gate]
        est += 4 * _nbytes((bm, bn), F32)

    out_shape = jax.ShapeDtypeStruct((bn_, t, n), out_dtype)
    out_specs = pl.BlockSpec((None, bm, bn), lambda j, b, i, k: (b, i, j))
    ms = 0
    if xs is not None:
        ms = xs.shape[1]
        in_specs.append(pl.BlockSpec((None, ms, bk), lambda j, b, i, k: (0, 0, k)))
        args.append(xs)
        if mode == "resid":
            nb = n // bn
            in_specs.append(pl.BlockSpec((None, ms, bn), lambda j, b, i, k: (0, 0, j)))
            in_specs.append(pl.BlockSpec((None, ms, bn), lambda j, b, i, k: (0, 0, gate_chunk * nb + j)))
            args += [res_s, gate_s]
        out_shape = (out_shape, jax.ShapeDtypeStruct((1, ms, n), out_dtype))
        out_specs = (out_specs, pl.BlockSpec((None, ms, bn), lambda j, b, i, k: (0, 0, j)))
        est += 2 * _nbytes((ms, bk), BF16) + 8 * _nbytes((ms, bn), F32)

    if side_jobs:
        assert nk == 1
        nt_ = t // bm
        out_shape = tuple(out_shape) if isinstance(out_shape, tuple) else (out_shape,)
        out_specs = tuple(out_specs) if isinstance(out_specs, tuple) else (out_specs,)

        def at_step(index_of_step):
            return lambda j, b, i, k: index_of_step((j * bn_ + b) * nt_ + i)

        for job in side_jobs:
            in_specs += [pl.BlockSpec(blk, at_step(f)) for blk, f in job.in_blocks]
            args += list(job.args)
            out_shape += tuple(job.out_shapes)
            out_specs += tuple(pl.BlockSpec(blk, at_step(f)) for blk, f in job.out_blocks)
            est += job.nbytes

    if nk > 1:
        scratch = [pltpu.VMEM((bm, bn), F32)] + ([pltpu.VMEM((ms, bn), F32)] if ms else [])
    elif w.dtype != BF16:
        scratch = [pltpu.VMEM((bk, bn), BF16)] * (2 if mode == "swiglu" else 1)
    else:
        scratch = []

    return pl.pallas_call(
        functools.partial(_linear_kernel, nk=nk, mode=mode, extra=xs is not None,
                          side_jobs=tuple(side_jobs)),
        out_shape=out_shape,
        grid=grid,
        in_specs=in_specs,
        out_specs=out_specs,
        scratch_shapes=scratch,
        compiler_params=_params(est, 4),
        name=name,
    )(*args)


def _attn_prompt_kernel(*refs, ngroups):
    q_refs, k_refs, v_refs = refs[:ngroups], refs[ngroups:2 * ngroups], refs[2 * ngroups:3 * ngroups]
    o_ref, og_ref, lg_ref, bias_ref = refs[3 * ngroups:]
    seq = o_ref.shape[0]
    scale = HEAD_DIM ** -0.5
    nt = (((1,), (1,)), ((), ()))
    neg = -jnp.inf
    row1 = lax.broadcasted_iota(jnp.int32, (N_BACK, N_BACK), 0)
    col1 = lax.broadcasted_iota(jnp.int32, (N_BACK, N_BACK), 1)
    row2 = lax.broadcasted_iota(jnp.int32, (N_BACK, 2 * N_BACK), 0)
    col2 = lax.broadcasted_iota(jnp.int32, (N_BACK, 2 * N_BACK), 1)
    ok_first = col1 <= row1
    ok_band = jnp.logical_or(jnp.logical_and(col2 < N_BACK, col2 >= row2),
                             jnp.logical_and(col2 >= N_BACK, col2 - N_BACK <= row2))
    bias_ref[:, :N_BACK] = jnp.where(ok_first, 0.0, neg)
    bias_ref[:, N_BACK:] = jnp.where(ok_band, 0.0, neg)
    ones = jnp.ones((N_BACK, HEAD_DIM), BF16)

    for g in range(ngroups):
        dil = DILATIONS[g]
        length = seq // dil
        nblk = length // N_BACK
        for r in range(dil):
            sub_rows = slice(0, length) if dil == 1 else pl.ds(r, length, stride=dil)
            q = q_refs[g][sub_rows, :].astype(BF16)
            k = k_refs[g][sub_rows, :].astype(BF16)
            v = v_refs[g][sub_rows, :].astype(BF16)
            scores = []
            for i in range(nblk):
                qi = q[i * N_BACK:(i + 1) * N_BACK]
                keys = k[max(i - 1, 0) * N_BACK:(i + 1) * N_BACK]
                scores.append(lax.dot_general(qi, keys, nt, preferred_element_type=F32) * scale)
            probs, maxes = [], []
            for i in range(nblk):
                s = scores[i] + (bias_ref[:, :N_BACK] if i == 0 else bias_ref[:, N_BACK:])
                m = jnp.max(s, axis=-1, keepdims=True)
                probs.append(jnp.exp(s - m).astype(BF16))
                maxes.append(m)
            for i in range(nblk):
                vals = v[max(i - 1, 0) * N_BACK:(i + 1) * N_BACK]
                vext = jnp.concatenate([vals, jnp.concatenate([ones] * (vals.shape[0] // N_BACK), axis=0)], axis=1)
                acc = jnp.dot(probs[i], vext, preferred_element_type=F32)
                den = acc[:, HEAD_DIM:]
                start = r + i * N_BACK * dil
                out_rows = slice(start, start + N_BACK) if dil == 1 else pl.ds(start, N_BACK, stride=dil)
                og_ref.at[g][out_rows, :] = acc[:, :HEAD_DIM] / den
                lg_ref.at[g][out_rows, :] = maxes[i] + jnp.log(den)

    rb = MERGE_ROW_TILE
    for c in range(seq // rb):
        rows = slice(c * rb, (c + 1) * rb)
        ls = [lg_ref[g, rows, :] for g in range(ngroups)]
        m = functools.reduce(jnp.maximum, ls)
        es = [jnp.exp(l - m) for l in ls]
        tot = functools.reduce(lambda a, b: a + b, es)
        num = functools.reduce(lambda a, b: a + b, [es[g] * og_ref[g, rows, :] for g in range(ngroups)])
        o_ref[rows, :] = (num / tot).astype(o_ref.dtype)


def _attn_prompt(q, kvs):
    ngroups = len(DILATIONS)
    b, s, qc = q.shape
    hw = qc // ngroups
    nh = hw // HEAD_DIM
    assert all(s % (d * N_BACK) == 0 for d in DILATIONS) and s % MERGE_ROW_TILE == 0
    blk = (None, s, HEAD_DIM)
    in_specs = ([pl.BlockSpec(blk, (lambda bi, h, g=g: (bi, 0, g * nh + h))) for g in range(ngroups)]
                + [pl.BlockSpec(blk, lambda bi, h: (bi, 0, h))] * ngroups
                + [pl.BlockSpec(blk, lambda bi, h: (bi, 0, nh + h))] * ngroups)
    est = (2 * (3 * ngroups * _nbytes((s, HEAD_DIM), F32) + _nbytes((s, HEAD_DIM), BF16))
           + 2 * ngroups * _nbytes((s, HEAD_DIM), F32) + 6 * _nbytes((s, HEAD_DIM), F32))
    return pl.pallas_call(
        functools.partial(_attn_prompt_kernel, ngroups=ngroups),
        out_shape=jax.ShapeDtypeStruct((b, s, hw), BF16),
        grid=(b, nh),
        in_specs=in_specs,
        out_specs=pl.BlockSpec(blk, lambda bi, h: (bi, 0, h)),
        scratch_shapes=[pltpu.VMEM((ngroups, s, HEAD_DIM), F32), pltpu.VMEM((ngroups, s, HEAD_DIM), F32),
                        pltpu.VMEM((N_BACK, 3 * N_BACK), F32)],
        compiler_params=_params(est, 2),
        name="attn_prompt",
    )(*([q] * ngroups), *kvs, *kvs)


def _attn_cached_kernel(*refs, dil, nheads, has_carry, final):
    if has_carry:
        q_ref, kn_ref, vn_ref, kb_ref, vb_ref, m_in, l_in, a_in = refs[:8]
        outs = refs[8:]
    else:
        q_ref, kn_ref, vn_ref, kb_ref, vb_ref = refs[:5]
        outs = refs[5:]
    if final:
        o_ref, m_sc, l_sc, a_sc = outs
    else:
        m_out, l_out, a_out, m_sc, l_sc, a_sc = outs
    c = pl.program_id(1)
    nc = pl.num_programs(1)
    restricted = len(kb_ref.shape) == 4
    res = kb_ref.shape[1] if restricted else 1
    period = dil if restricted else 1
    wc = kb_ref.shape[0] * res
    w_total = kb_ref.shape[0] * period * nc
    nq = q_ref.shape[0]
    hshift = nheads.bit_length() - 1
    rshift = res.bit_length() - 1
    scale = HEAD_DIM ** -0.5
    nt = (((1,), (1,)), ((), ()))
    neg = -jnp.inf
    q = q_ref[...].astype(BF16)

    def attend(k, v, ok):
        s = lax.dot_general(q, k.astype(BF16), nt, preferred_element_type=F32) * scale
        s = jnp.where(ok, s, neg)
        m_old = m_sc[...]
        m_new = jnp.maximum(m_old, jnp.max(s, axis=-1, keepdims=True))
        alpha = jnp.exp(m_old - m_new)
        p = jnp.exp(s - m_new)
        l_sc[...] = alpha * l_sc[...] + jnp.sum(p, axis=-1, keepdims=True)
        a_sc[...] = alpha * a_sc[...] + jnp.dot(p.astype(BF16), v.astype(BF16), preferred_element_type=F32)
        m_sc[...] = m_new

    @pl.when(c == 0)
    def _():
        if has_carry:
            m_sc[...] = m_in[...]
            l_sc[...] = l_in[...]
            a_sc[...] = a_in[...]
        else:
            m_sc[...] = jnp.full(m_sc.shape, neg, F32)
            l_sc[...] = jnp.zeros(l_sc.shape, F32)
            a_sc[...] = jnp.zeros(a_sc.shape, F32)
        qr = lax.broadcasted_iota(jnp.int32, (nq, nq), 0)
        kr = lax.broadcasted_iota(jnp.int32, (nq, nq), 1)
        dt = (qr >> hshift) - (kr >> hshift)
        ok = ((qr & (nheads - 1)) == (kr & (nheads - 1))) & (dt >= 0) & ((dt & (dil - 1)) == 0)
        attend(kn_ref[...], vn_ref[...], ok)

    qr = lax.broadcasted_iota(jnp.int32, (nq, wc * nheads), 0)
    kr = lax.broadcasted_iota(jnp.int32, (nq, wc * nheads), 1)
    held = kr >> hshift
    w_row = (c * (wc // res) + (held >> rshift)) * period + (held & (res - 1))
    dist = w_total + (qr >> hshift) - w_row
    ok = (((qr & (nheads - 1)) == (kr & (nheads - 1))) & ((dist & (dil - 1)) == 0)
          & (dist <= dil * N_BACK))
    attend(kb_ref[...].reshape(wc * nheads, HEAD_DIM), vb_ref[...].reshape(wc * nheads, HEAD_DIM), ok)

    @pl.when(c == nc - 1)
    def _():
        if final:
            o_ref[...] = a_sc[...] / l_sc[...]
        else:
            m_out[...] = m_sc[...]
            l_out[...] = l_sc[...]
            a_out[...] = a_sc[...]


def _attn_cached(qf, knf, vnf, buf, g, carry, *, final, wc):
    dil = DILATIONS[g]
    b, nq, hd = qf.shape
    w, nheads = buf.shape[1], buf.shape[3]
    t_new = nq // nheads
    assert w % wc == 0 and nheads & (nheads - 1) == 0 and dil & (dil - 1) == 0
    row = pl.BlockSpec((None, nq, hd), lambda bi, c: (bi, 0, 0))
    stat = pl.BlockSpec((None, nq, 1), lambda bi, c: (bi, 0, 0))
    if dil > t_new and t_new & (t_new - 1) == 0 and w % dil == 0 and wc % dil == 0:
        wc = min(w, wc * dil // t_new)
        buf = buf.reshape(b, w // dil, dil, 2, nheads, hd)
        kv_blk = (None, wc // dil, t_new, None, nheads, hd)
        kv_specs = [pl.BlockSpec(kv_blk, lambda bi, c: (bi, c, 0, 0, 0, 0)),
                    pl.BlockSpec(kv_blk, lambda bi, c: (bi, c, 0, 1, 0, 0))]
        held = wc // dil * t_new
    else:
        kv_blk = (None, wc, None, nheads, hd)
        kv_specs = [pl.BlockSpec(kv_blk, lambda bi, c: (bi, c, 0, 0, 0)),
                    pl.BlockSpec(kv_blk, lambda bi, c: (bi, c, 1, 0, 0))]
        held = wc
    in_specs = [row, row, row] + kv_specs
    args = [qf, knf, vnf, buf, buf]
    if carry is not None:
        in_specs += [stat, stat, row]
        args += list(carry)
    if final:
        out_shape = jax.ShapeDtypeStruct((b, nq, hd), F32)
        out_specs = row
    else:
        out_shape = (jax.ShapeDtypeStruct((b, nq, 1), F32), jax.ShapeDtypeStruct((b, nq, 1), F32),
                     jax.ShapeDtypeStruct((b, nq, hd), F32))
        out_specs = (stat, stat, row)
    est = (4 * _nbytes((held, nheads, hd), F32) + 4 * _nbytes((nq, held * nheads), F32)
           + 16 * _nbytes((nq, hd), F32))
    return pl.pallas_call(
        functools.partial(_attn_cached_kernel, dil=dil, nheads=nheads, has_carry=carry is not None, final=final),
        out_shape=out_shape,
        grid=(b, w // wc),
        in_specs=in_specs,
        out_specs=out_specs,
        scratch_shapes=[pltpu.VMEM((nq, 1), F32), pltpu.VMEM((nq, 1), F32), pltpu.VMEM((nq, hd), F32)],
        compiler_params=_params(est, 2),
        name=f"attn_cached_g{g}",
    )(*args)


def _shift_block(cur_ref, nxt_ref, new_ref, o_ref, last):
    rb, t_new = cur_ref.shape[0], new_ref.shape[0]
    o_ref[0:rb - t_new] = cur_ref[t_new:rb]
    o_ref[rb - t_new:rb] = jnp.where(last, new_ref[...], nxt_ref[...])


def _shift_kernel(cur_ref, nxt_ref, new_ref, o_ref):
    _shift_block(cur_ref, nxt_ref, new_ref, o_ref, pl.program_id(1) == pl.num_programs(1) - 1)


def _shift_job(buf, new, steps, *, rb):
    b, w, r, lanes = buf.shape
    t_new = new.shape[1]
    if w % rb or rb % t_new or rb <= t_new or b * (w // rb) > steps:
        return None
    nblk = w // rb
    nact = b * nblk
    last_blk = w // t_new - 1

    def where(s):
        s = jnp.minimum(s, nact - 1)
        return lax.div(s, nblk), lax.rem(s, nblk)

    def body(step, ins, outs):
        @pl.when(step < nact)
        def _():
            _shift_block(*ins, outs[0], lax.rem(step, nblk) == nblk - 1)

    blk = (None, rb, r, lanes)
    small = (None, t_new, r, lanes)
    return _SideJob(
        args=(buf, buf, new),
        in_blocks=((blk, lambda s: (*where(s), 0, 0)),
                   (small, lambda s: (where(s)[0],
                                      jnp.minimum((where(s)[1] + 1) * (rb // t_new), last_blk), 0, 0)),
                   (small, lambda s: (where(s)[0], 0, 0, 0))),
        out_shapes=(jax.ShapeDtypeStruct(buf.shape, F32),),
        out_blocks=((blk, lambda s: (*where(s), 0, 0)),),
        body=body,
        nbytes=4 * _nbytes((rb, r, lanes), F32) + 4 * _nbytes((t_new, r, lanes), F32))


def _shift_append(buf, new, *, rb):
    b, w, r, lanes = buf.shape
    t_new = new.shape[1]
    assert w % rb == 0 and rb % t_new == 0 and rb > t_new
    last_blk = w // t_new - 1
    est = 4 * _nbytes((rb, r, lanes), F32) + 6 * _nbytes((t_new, r, lanes), F32)
    return pl.pallas_call(
        _shift_kernel,
        out_shape=jax.ShapeDtypeStruct(buf.shape, F32),
        grid=(b, w // rb),
        in_specs=[
            pl.BlockSpec((None, rb, r, lanes), lambda bi, j: (bi, j, 0, 0)),
            pl.BlockSpec((None, t_new, r, lanes),
                         lambda bi, j: (bi, jnp.minimum((j + 1) * (rb // t_new), last_blk), 0, 0)),
            pl.BlockSpec((None, t_new, r, lanes), lambda bi, j: (bi, 0, 0, 0)),
        ],
        out_specs=pl.BlockSpec((None, rb, r, lanes), lambda bi, j: (bi, j, 0, 0)),
        compiler_params=_params(est, 2),
        name="kv_shift_append",
    )(buf, buf, new)


def _hgrn_kernel(*refs, heads, chunk, layer, has_state):
    if has_state:
        zq_ref, zf_ref, zv_ref, zg_ref, lb_ref, ng_ref, s0_ref = refs[:7]
        rest = refs[7:]
    else:
        zq_ref, zf_ref, zv_ref, zg_ref, lb_ref, ng_ref = refs[:6]
        s0_ref = None
        rest = refs[6:]
    y_ref, st_out_ref, st_ref = rest[:3]
    bc_ref = rest[3] if len(rest) > 3 else None
    tstep = pl.program_id(2)
    nsteps = pl.num_programs(2)
    tb = zq_ref.shape[0]

    @pl.when(tstep == 0)
    def _():
        for h in range(heads):
            if has_state:
                st_ref[h] = s0_ref[h].T
            else:
                st_ref[h] = jnp.zeros((HGRN_DK, HGRN_DK), F32)

    logits = lb_ref[...]
    ex = jnp.exp(logits - jnp.max(logits, axis=0, keepdims=True))
    sm = ex / jnp.sum(ex, axis=0, keepdims=True)
    lb = jnp.zeros_like(sm[0:1])
    for l in range(1, layer + 1):
        lb = lb + sm[l:l + 1]
    ng = ng_ref[...]

    sub = V7X_SUBLANES
    rblk = sub if chunk % sub == 0 else chunk
    nblk = chunk // rblk
    rowi = lax.broadcasted_iota(jnp.int32, (rblk, HGRN_DK), 0)
    rowi_all = lax.broadcasted_iota(jnp.int32, (chunk, zq_ref.shape[1]), 0)
    nt = (((1,), (1,)), ((), ()))
    tn = (((0,), (0,)), ((), ()))

    def body(ci, carry):
        rows = pl.ds(pl.multiple_of(ci * chunk, chunk), chunk)
        zq = zq_ref[rows, :]
        zf = zf_ref[rows, :]
        zv = zv_ref[rows, :]
        zg = zg_ref[rows, :]
        q_all = _silu(zq)
        f_all = lb + (1.0 - lb) * jax.nn.sigmoid(zf)
        lf_all = jnp.log(f_all)
        k_all = 1.0 - f_all
        og_all = ng * _silu(zg)
        if chunk % sub == 0:
            a_all = lf_all
            shift = 1
            while shift < chunk:
                a_all = a_all + jnp.where(rowi_all >= shift, pltpu.roll(a_all, shift, 0), 0.0)
                shift *= 2
        else:
            a_all = jnp.zeros_like(lf_all)
            for s in range(chunk):
                a_all = a_all + jnp.where(rowi_all >= s, lf_all[s:s + 1, :], 0.0)
        a_all = a_all * LOG2_E
        qe_all = q_all * jnp.exp2(a_all)
        a_last_all = a_all[chunk - 1:chunk, :]
        kd_all = k_all * jnp.exp2(a_last_all - a_all)
        dec_all = jnp.exp2(a_last_all)
        if bc_ref is not None:
            for h in range(heads):
                lanes = slice(h * HGRN_DK, (h + 1) * HGRN_DK)
                bc_ref[0, h] = a_all[:, lanes]
                bc_ref[1, h] = k_all[:, lanes]
                bc_ref[2, h] = zv[:, lanes]
        o_inter, scores, states_in = [], [], []
        for h in range(heads):
            lanes = slice(h * HGRN_DK, (h + 1) * HGRN_DK)
            q, k, a = q_all[:, lanes], k_all[:, lanes], a_all[:, lanes]
            st = st_ref[h]
            states_in.append(st)
            o_inter.append(lax.dot_general(qe_all[:, lanes].astype(BF16), st.astype(BF16), nt,
                                           preferred_element_type=F32))
            sc = None
            for j in range(nblk - 1):
                r0, r1 = j * rblk, (j + 1) * rblk
                b_j = a[r1 - 1:r1, :]
                qt = jnp.concatenate([jnp.zeros((r1, HGRN_DK), F32), q[r1:] * jnp.exp2(a[r1:] - b_j)], axis=0)
                kh = [k[r0:r1] * jnp.exp2(b_j - a[r0:r1])]
                if r0 > 0:
                    kh.insert(0, jnp.zeros((r0, HGRN_DK), F32))
                kh.append(jnp.zeros((chunk - r1, HGRN_DK), F32))
                sc_j = lax.dot_general(qt.astype(BF16), jnp.concatenate(kh, axis=0).astype(BF16), nt,
                                       preferred_element_type=F32)
                sc = sc_j if sc is None else sc + sc_j
            scores.append(sc)
        for h in range(heads):
            lanes = slice(h * HGRN_DK, (h + 1) * HGRN_DK)
            q, k, v, a = q_all[:, lanes], k_all[:, lanes], zv[:, lanes], a_all[:, lanes]
            st = states_in[h]
            o = o_inter[h]
            if scores[h] is not None:
                o = o + jnp.dot(scores[h].astype(BF16), v.astype(BF16), preferred_element_type=F32)
            ob = [None] * nblk
            for s in range(chunk):
                bi = s // rblk
                r0 = bi * rblk
                if bc_ref is not None:
                    a_s, k_s, v_s = (jnp.broadcast_to(bc_ref[c, h, s:s + 1, :], (rblk, HGRN_DK))
                                     for c in range(3))
                else:
                    a_s, k_s, v_s = a[s:s + 1, :], k[s:s + 1, :], v[s:s + 1, :]
                e = jnp.where(rowi >= s - r0, jnp.exp2(a[r0:r0 + rblk] - a_s), 0.0)
                wgt = (q[r0:r0 + rblk] * k_s) * e
                term = jnp.sum(wgt, axis=-1, keepdims=True) * v_s
                ob[bi] = term if ob[bi] is None else ob[bi] + term
            o = o + (ob[0] if nblk == 1 else jnp.concatenate(ob, axis=0))
            upd = lax.dot_general(v.astype(BF16), kd_all[:, lanes].astype(BF16), tn,
                                  preferred_element_type=F32)
            st_ref[h] = st * dec_all[:, lanes] + upd
            on = o * lax.rsqrt(jnp.mean(o * o, axis=-1, keepdims=True) + EPS)
            y_ref[rows, lanes] = (on * og_all[:, lanes]).astype(y_ref.dtype)
        return carry

    lax.fori_loop(0, tb // chunk, body, 0)

    @pl.when(tstep == nsteps - 1)
    def _():
        for h in range(heads):
            st_out_ref[h] = st_ref[h].T


def _hgrn(z, lb_logits, norm_g, s0, *, layer, heads, tb, out_dtype):
    b, t, d4 = z.shape
    d = d4 // 4
    nh = d // HGRN_DK
    nhg = nh // heads
    hb = heads * HGRN_DK
    chunk = math.gcd(t, HGRN_CHUNK)
    depth = lb_logits.shape[0]
    assert t % tb == 0 and tb % chunk == 0 and nh % heads == 0

    def zspec(part):
        return pl.BlockSpec((None, tb, hb), lambda bi, hg, i: (bi, i, part * nhg + hg))

    in_specs = [zspec(0), zspec(1), zspec(2), zspec(3),
                pl.BlockSpec((depth, hb), lambda bi, hg, i: (0, hg)),
                pl.BlockSpec((1, hb), lambda bi, hg, i: (0, hg))]
    args = [z, z, z, z, lb_logits, norm_g.reshape(1, d)]
    sspec = pl.BlockSpec((None, heads, HGRN_DK, HGRN_DK), lambda bi, hg, i: (bi, hg, 0, 0))
    if s0 is not None:
        in_specs.append(sspec)
        args.append(s0)
    est = (2 * (4 * _nbytes((tb, hb), F32) + _nbytes((tb, hb), out_dtype))
           + 5 * _nbytes((heads, HGRN_DK, HGRN_DK), F32))
    return pl.pallas_call(
        functools.partial(_hgrn_kernel, heads=heads, chunk=chunk, layer=layer, has_state=s0 is not None),
        out_shape=(jax.ShapeDtypeStruct((b, t, d), out_dtype),
                   jax.ShapeDtypeStruct((b, nh, HGRN_DK, HGRN_DK), F32)),
        grid=(b, nhg, t // tb),
        in_specs=in_specs,
        out_specs=(pl.BlockSpec((None, tb, hb), lambda bi, hg, i: (bi, i, hg)), sspec),
        scratch_shapes=([pltpu.VMEM((heads, HGRN_DK, HGRN_DK), F32)]
                        + ([pltpu.VMEM((3, heads, chunk, HGRN_DK), F32)] if chunk % V7X_SUBLANES == 0 else [])),
        compiler_params=_params(est, 3),
        name="hgrn2",
    )(*args)


def _forward(xp, xs, c_all, ada_w, ada_b, a_bufs, b_states, weights, *, nb, t_seq):
    (norm_mix_g, norm_ffn_g, a_w_qkv, a_w_o, b_w_in, b_lb_logits, b_norm_g, b_w_o,
     ffn_w_in, ffn_w_out, final_g) = weights
    bp, t, d = xp.shape
    ms = xs.shape[1]
    depth = norm_mix_g.shape[0]
    d_ff = ffn_w_out.shape[1]
    ngroups = len(DILATIONS)
    hw = a_w_o.shape[1]
    nheads = hw // HEAD_DIM
    bm = _pick(t, ROW_TILES)
    bm_e = _pick(t, NORM_ROW_TILES)
    bn = N_TILE
    ident = lambda j: j
    kv_p, kv_s, st_p, st_s = [], [], [], []
    w_dn_all = None
    bnf = N_TILE_SWIGLU
    nff = d_ff // bnf
    ffn_steps = nff * bp * (t // bm)
    ada_b3 = ada_b.reshape(depth, 1, 6 * d)
    ride = depth > 1 and _ada_job(c_all, ada_w, ada_b3, 1, ffn_steps) is not None
    mod0 = _ada_modulation(c_all, ada_w, ada_b3, nlayers=1 if ride else depth, bn=ADA_N_TILE)
    mod_rows = {l: mod0[l] for l in range(mod0.shape[0])}
    pending_shifts = []

    for layer in range(depth):
        i = layer // N_MIXERS
        mod_l = mod_rows[layer]
        mp = mod_l[:bp].reshape(bp, 1, 6 * d)
        msd = jnp.repeat(mod_l[bp:bp + nb], t_seq, axis=0).reshape(1, ms, 6 * d)
        hp = _norm_mod(xp, norm_mix_g[layer], mp, 1, 0, bm=bm_e)
        hs = _norm_mod(xs, norm_mix_g[layer], msd, 1, 0, bm=ms)
        if layer % N_MIXERS == 0:
            qp, qs = _linear(hp, a_w_qkv, layer=i, n=ngroups * hw, col_map=ident, out_dtype=F32,
                             bm=bm, bn=bn, xs=hs, name="a_q")
            half = hw // bn
            q5 = qs.reshape(nb, t_seq, ngroups, nheads, HEAD_DIM)
            flat = lambda a: a.reshape(nb, t_seq * nheads, HEAD_DIM)
            kvs = []
            carry = None
            for g in range(ngroups):
                def kv_cols(j, g=g):
                    return jnp.where(j < half, (ngroups + g) * half + j, (2 * ngroups + g) * half + (j - half))
                kvp, kvn = _linear(hp, a_w_qkv, layer=i, n=2 * hw, col_map=kv_cols, out_dtype=F32,
                                   bm=bm, bn=bn, xs=hs, name=f"a_kv{g}")
                kvs.append(kvp)
                keep = min(DILATIONS[g] * N_BACK, t)
                kv_keep = kvp if keep == t else kvp[:, t - keep:]
                kv_p.append(kv_keep.reshape(1, bp, keep, 2, nheads, HEAD_DIM))
                buf = a_bufs[g][i]
                w = buf.shape[1]
                kvn = kvn.reshape(nb, t_seq, 2, nheads, HEAD_DIM)
                carry = _attn_cached(flat(q5[:, :, g]), flat(kvn[:, :, 0]), flat(kvn[:, :, 1]), buf, g,
                                     carry, final=g == ngroups - 1, wc=min(w, WINDOW_CHUNK))
                pending_shifts.append((buf.reshape(nb, w, 2 * nheads, HEAD_DIM),
                                       kvn.reshape(nb, t_seq, 2 * nheads, HEAD_DIM), len(kv_s)))
                kv_s.append(None)
            yp = _attn_prompt(qp, kvs)
            ys = carry.reshape(1, ms, hw).astype(BF16)
            w_o = a_w_o
        else:
            zp, zs = _linear(hp, b_w_in, layer=i, n=4 * d, col_map=ident, out_dtype=F32, bm=bm, bn=bn,
                             xs=hs, name="b_in")
            hg_heads = _pick(d // HGRN_DK, HGRN_HEADS_PER_STEP)
            yp, s_p = _hgrn(zp, b_lb_logits, b_norm_g[i], None, layer=layer, heads=hg_heads,
                            tb=min(t, HGRN_ROW_TILE), out_dtype=BF16)
            ysb, s_s = _hgrn(zs.reshape(nb, t_seq, 4 * d), b_lb_logits, b_norm_g[i], b_states[i],
                             layer=layer, heads=hg_heads, tb=t_seq, out_dtype=F32)
            ys = ysb.reshape(1, ms, d).astype(BF16)
            st_p.append(s_p)
            st_s.append(s_s)
            w_o = b_w_o
        bn_o = N_TILE_SHORT_K if (w_o.shape[1] <= SHORT_K and d % N_TILE_SHORT_K == 0) else bn
        xp, xs = _linear(yp, w_o, layer=i, n=d, col_map=ident, out_dtype=F32, bm=bm, bn=bn_o,
                         mode="resid", res=xp, gate=mp, gate_chunk=2, xs=ys, res_s=xs, gate_s=msd,
                         name="mix_o")
        hp = _norm_mod(xp, norm_ffn_g[layer], mp, 4, 3, bm=bm_e)
        hs = _norm_mod(xs, norm_ffn_g[layer], msd, 4, 3, bm=ms)
        jobs, roles = [], []
        cast = _cast_job(ffn_w_out, layer, ffn_steps)
        if cast is not None:
            jobs.append(cast)
            roles.append(("w_dn", None))
        if ride and layer + 1 < depth:
            jobs.append(_ada_job(c_all, ada_w, ada_b3, layer + 1, ffn_steps))
            roles.append(("mod", layer + 1))
        else:
            for buf4, new4, slot in pending_shifts:
                job = next((jb for jb in (_shift_job(buf4, new4, ffn_steps, rb=rb)
                                          for rb in SIDE_SHIFT_ROWS if rb <= buf4.shape[1])
                            if jb is not None), None)
                if job is not None:
                    jobs.append(job)
                    roles.append(("shift", slot))
            pending_shifts = [p for p in pending_shifts if ("shift", p[2]) not in roles]
        outs = _linear(hp, ffn_w_in, layer=layer, n=d_ff, col_map=ident, col_map2=lambda j: nff + j,
                       out_dtype=BF16, bm=bm, bn=bnf, mode="swiglu", xs=hs, side_jobs=jobs, name="ffn_in")
        actp, acts = outs[0], outs[1]
        w_dn = None
        for (role, key), val in zip(roles, outs[2:]):
            if role == "w_dn":
                w_dn, dn_layer = val[None], 0
            elif role == "mod":
                mod_rows[key] = val
            else:
                kv_s[key] = val.reshape(1, nb, val.shape[1], 2, nheads, HEAD_DIM)
        if w_dn is None:
            if w_dn_all is None:
                w_dn_all = _to_bf16(ffn_w_out, kb=_pick(d_ff, (d_ff // 16, d_ff // 8, d_ff // 4, d_ff // 2)))
            w_dn, dn_layer = w_dn_all, layer
        xp, xs = _linear(actp, w_dn, layer=dn_layer, n=d, col_map=ident, out_dtype=F32,
                         bm=_pick(t, ROW_TILES_LONG_K), bn=bn, mode="resid", res=xp, gate=mp, gate_chunk=5,
                         xs=acts, res_s=xs, gate_s=msd, name="ffn_out")
    for buf4, new4, slot in pending_shifts:
        w = buf4.shape[1]
        kv_s[slot] = _shift_append(buf4, new4, rb=min(w, WINDOW_CHUNK)).reshape(1, nb, w, 2, nheads, HEAD_DIM)
    yp = _final_norm(xp, final_g, bm=bm_e)
    ys = _final_norm(xs, final_g, bm=ms)
    return yp, ys, kv_p, kv_s, st_p, st_s


def kernel(x_prompt, x_sample, state_a_kv_w128, state_a_kv_w512, state_a_kv_w2048, state_b_rec,
           c_prompt, c_sample, ada_w, ada_b, norm_mix_g, norm_ffn_g, a_w_qkv, a_w_o,
           b_w_in, b_lb_logits, b_norm_g, b_w_o, ffn_w_in, ffn_w_out, final_g):
    bp, sp, d = x_prompt.shape
    bs, ts, _ = x_sample.shape
    depth = ada_w.shape[0]
    weights = (norm_mix_g, norm_ffn_g, a_w_qkv, a_w_o, b_w_in, b_lb_logits, b_norm_g, b_w_o,
               ffn_w_in, ffn_w_out, final_g)

    rpad = -(bp + bs) % V7X_SUBLANES
    c_all = jnp.pad(jnp.concatenate([c_prompt, c_sample], axis=0), ((0, rpad), (0, 0)))
    y_p, y_s, kv_p, kv_s, st_p, st_s = _forward(
        x_prompt, x_sample.reshape(1, bs * ts, d), c_all, ada_w, ada_b,
        (state_a_kv_w128, state_a_kv_w512, state_a_kv_w2048), state_b_rec, weights, nb=bs, t_seq=ts)
    return (y_p, y_s.reshape(bs, ts, d), kv_p[0], kv_s[0], kv_p[1], kv_s[1], kv_p[2], kv_s[2],
            jnp.stack(st_p, axis=0), jnp.stack(st_s, axis=0))
```

```python
import functools
import math
from typing import Callable, NamedTuple

import jax
import jax.numpy as jnp
from jax import lax
from jax.experimental import pallas as pl
from jax.experimental.pallas import tpu as pltpu

F32 = jnp.float32
BF16 = jnp.bfloat16

EPS = 1e-6
DILATIONS = (1, 4, 16)
N_BACK = 128
HEAD_DIM = 128
HGRN_DK = 128
HGRN_CHUNK = 16
LOG2_E = 1.4426950408889634
N_MIXERS = 2
V7X_LANES = 128
V7X_SUBLANES = 8
V7X_SCOPED_VMEM_BYTES = 60000 * 1024
V7X_MIN_SCOPED_VMEM_BYTES = 16 * 1024 * 1024

ROW_TILES = (1024, 512, 256, 128, 64, 32, 16, 8)
ROW_TILES_LONG_K = ROW_TILES[1:]
NORM_ROW_TILES = (512, 256, 128, 64, 32, 16, 8)
NORM_COL_CHUNK = 512
N_TILE = 512
N_TILE_SHORT_K = 1024
SHORT_K = 2048
N_TILE_SWIGLU = 256
ADA_N_TILE = 512
HGRN_HEADS_PER_STEP = (16, 8, 4, 2, 1)
HGRN_ROW_TILE = 512
WINDOW_CHUNK = 256
SIDE_SHIFT_ROWS = (64, 128, 256, 512, 1024)
MERGE_ROW_TILE = 256


def _params(nbytes, n_axes):
    limit = int(min(V7X_SCOPED_VMEM_BYTES, max(2 * nbytes, V7X_MIN_SCOPED_VMEM_BYTES)))
    return pltpu.CompilerParams(dimension_semantics=("arbitrary",) * n_axes, vmem_limit_bytes=limit)


def _nbytes(shape, dtype):
    return math.prod(shape) * jnp.dtype(dtype).itemsize


def _silu(x):
    return x * jax.nn.sigmoid(x)


def _pick(total, prefs):
    for p in prefs:
        if total % p == 0:
            return p
    return total


class _SideJob(NamedTuple):
    args: tuple
    in_blocks: tuple
    out_shapes: tuple
    out_blocks: tuple
    body: Callable
    nbytes: int

    @property
    def n_in(self):
        return len(self.in_blocks)

    @property
    def n_out(self):
        return len(self.out_blocks)


def _ada_kernel(c_ref, w_ref, b_ref, o_ref):
    s = _silu(c_ref[...])
    acc = jnp.dot(s.astype(BF16), w_ref[...].astype(BF16), preferred_element_type=F32)
    o_ref[...] = acc + b_ref[...]


def _ada_job(c_all, ada_w, ada_b3, layer, steps):
    _, d, n = ada_w.shape
    r = c_all.shape[0]
    cw = next((c for c in range(V7X_LANES, 4 * V7X_LANES + 1, V7X_LANES) if n % c == 0 and n // c <= steps), None)
    if cw is None:
        return None
    nact = n // cw
    col = lambda s: jnp.minimum(s, nact - 1)

    def body(step, ins, outs):
        @pl.when(step < nact)
        def _():
            _ada_kernel(*ins, outs[0])

    return _SideJob(
        args=(c_all, ada_w, ada_b3),
        in_blocks=(((r, d), lambda s: (0, 0)),
                   ((None, d, cw), lambda s: (layer, 0, col(s))),
                   ((None, 1, cw), lambda s: (layer, 0, col(s)))),
        out_shapes=(jax.ShapeDtypeStruct((r, n), F32),),
        out_blocks=(((r, cw), lambda s: (0, col(s))),),
        body=body,
        nbytes=2 * (_nbytes((d, cw), F32) + _nbytes((r, cw), F32)) + _nbytes((d, cw), BF16) + _nbytes((r, d), F32))


def _ada_modulation(c_all, ada_w, ada_b3, *, nlayers, bn):
    _, d, n = ada_w.shape
    depth = nlayers
    r = c_all.shape[0]
    est = 2 * (_nbytes((d, bn), F32) + _nbytes((r, bn), F32)) + _nbytes((d, bn), BF16) + _nbytes((r, d), F32)
    return pl.pallas_call(
        _ada_kernel,
        out_shape=jax.ShapeDtypeStruct((depth, r, n), F32),
        grid=(depth, n // bn),
        in_specs=[
            pl.BlockSpec((r, d), lambda l, j: (0, 0)),
            pl.BlockSpec((None, d, bn), lambda l, j: (l, 0, j)),
            pl.BlockSpec((None, 1, bn), lambda l, j: (l, 0, j)),
        ],
        out_specs=pl.BlockSpec((None, r, bn), lambda l, j: (l, 0, j)),
        compiler_params=_params(est, 2),
        name="ada_modulation",
    )(c_all, ada_w, ada_b3)


def _norm_mod_kernel(x_ref, g_ref, sc_ref, sh_ref, o_ref):
    d = x_ref.shape[1]
    cw = NORM_COL_CHUNK if d % NORM_COL_CHUNK == 0 else d
    ss = None
    for c in range(0, d, cw):
        xc = x_ref[:, c:c + cw]
        part = jnp.sum(xc * xc, axis=-1, keepdims=True)
        ss = part if ss is None else ss + part
    r = lax.rsqrt(ss * (1.0 / d) + EPS)
    for c in range(0, d, cw):
        cols = slice(c, c + cw)
        y = x_ref[:, cols] * r
        h = (y * g_ref[:, cols]) * (1.0 + sc_ref[:, cols]) + sh_ref[:, cols]
        o_ref[:, cols] = h.astype(o_ref.dtype)


def _norm_mod(x, g, mod, sc_chunk, sh_chunk, *, bm):
    bn_, t, d = x.shape
    tg = mod.shape[1]
    mg = 1 if tg == 1 else bm

    def mod_map(chunk):
        return lambda b, i: (b, 0 if tg == 1 else i, chunk)

    est = 2 * (_nbytes((bm, d), F32) + _nbytes((bm, d), BF16)) + 2 * _nbytes((bm, d), F32)
    return pl.pallas_call(
        _norm_mod_kernel,
        out_shape=jax.ShapeDtypeStruct((bn_, t, d), BF16),
        grid=(bn_, t // bm),
        in_specs=[
            pl.BlockSpec((None, bm, d), lambda b, i: (b, i, 0)),
            pl.BlockSpec((1, d), lambda b, i: (0, 0)),
            pl.BlockSpec((None, mg, d), mod_map(sc_chunk)),
            pl.BlockSpec((None, mg, d), mod_map(sh_chunk)),
        ],
        out_specs=pl.BlockSpec((None, bm, d), lambda b, i: (b, i, 0)),
        compiler_params=_params(est, 2),
        name="norm_mod",
    )(x, g.reshape(1, d), mod, mod)


def _final_norm_kernel(x_ref, g_ref, o_ref):
    x = x_ref[...]
    y = x * lax.rsqrt(jnp.mean(x * x, axis=-1, keepdims=True) + EPS)
    o_ref[...] = y * g_ref[...]


def _final_norm(x, g, *, bm):
    bn_, t, d = x.shape
    est = 6 * _nbytes((bm, d), F32)
    return pl.pallas_call(
        _final_norm_kernel,
        out_shape=jax.ShapeDtypeStruct((bn_, t, d), F32),
        grid=(bn_, t // bm),
        in_specs=[
            pl.BlockSpec((None, bm, d), lambda b, i: (b, i, 0)),
            pl.BlockSpec((1, d), lambda b, i: (0, 0)),
        ],
        out_specs=pl.BlockSpec((None, bm, d), lambda b, i: (b, i, 0)),
        compiler_params=_params(est, 2),
        name="final_norm",
    )(x, g.reshape(1, d))


def _to_bf16_kernel(w_ref, o_ref):
    o_ref[...] = w_ref[...].astype(BF16)


def _to_bf16(w, *, kb):
    l, k, n = w.shape
    assert k % kb == 0
    est = 2 * (_nbytes((kb, n), F32) + _nbytes((kb, n), BF16))
    spec = pl.BlockSpec((None, kb, n), lambda li, i: (li, i, 0))
    return pl.pallas_call(
        _to_bf16_kernel,
        out_shape=jax.ShapeDtypeStruct(w.shape, BF16),
        grid=(l, k // kb),
        in_specs=[spec],
        out_specs=spec,
        compiler_params=_params(est, 2),
        name="weight_to_bf16",
    )(w)


def _cast_job(w, layer, steps):
    kc, nc = w.shape[1:]
    if kc % steps or (kc // steps) % 16:
        return None
    slab = kc // steps

    def body(step, ins, outs):
        _to_bf16_kernel(ins[0], outs[0])

    return _SideJob(
        args=(w,),
        in_blocks=(((None, slab, nc), lambda s: (layer, s, 0)),),
        out_shapes=(jax.ShapeDtypeStruct((kc, nc), BF16),),
        out_blocks=(((slab, nc), lambda s: (s, 0)),),
        body=body,
        nbytes=2 * (_nbytes((slab, nc), F32) + _nbytes((slab, nc), BF16)))


def _linear_kernel(*refs, nk, mode, extra, side_jobs):
    refs = list(refs)
    x_ref, w_ref = refs[0], refs[1]
    pos = 2
    w2_ref = res_ref = gate_ref = xs_ref = ress_ref = gates_ref = os_ref = None
    if mode == "swiglu":
        w2_ref = refs[pos]
        pos += 1
    if mode == "resid":
        res_ref, gate_ref = refs[pos], refs[pos + 1]
        pos += 2
    if extra:
        xs_ref = refs[pos]
        pos += 1
        if mode == "resid":
            ress_ref, gates_ref = refs[pos], refs[pos + 1]
            pos += 2
    job_ins = []
    for job in side_jobs:
        job_ins.append(refs[pos:pos + job.n_in])
        pos += job.n_in
    o_ref = refs[pos]
    pos += 1
    if extra:
        os_ref = refs[pos]
        pos += 1
    step = (pl.program_id(0) * pl.num_programs(1) + pl.program_id(1)) * pl.num_programs(2) + pl.program_id(2)
    for job, ins in zip(side_jobs, job_ins):
        job.body(step, ins, refs[pos:pos + job.n_out])
        pos += job.n_out
    scratch = refs[pos:]

    def finalize(out_ref, r_ref, g_ref, acc, acc2=None):
        if mode == "swiglu":
            out_ref[...] = (_silu(acc) * acc2).astype(out_ref.dtype)
        elif mode == "resid":
            out_ref[...] = (r_ref[...] + g_ref[...] * acc).astype(out_ref.dtype)
        else:
            out_ref[...] = acc.astype(out_ref.dtype)

    cast = w_ref.dtype != BF16
    first = jnp.logical_and(pl.program_id(1) == 0, pl.program_id(2) == 0)
    if nk == 1:
        wb_ref, wb2_ref = w_ref, w2_ref
        if cast:
            wb_ref = scratch[0]
            wb2_ref = scratch[1] if mode == "swiglu" else None

        def first_step():
            if cast:
                wb_ref[...] = w_ref[...].astype(BF16)
                if mode == "swiglu":
                    wb2_ref[...] = w2_ref[...].astype(BF16)
            if extra:
                xs = xs_ref[...]
                accs = jnp.dot(xs, wb_ref[...], preferred_element_type=F32)
                accs2 = None
                if mode == "swiglu":
                    accs2 = jnp.dot(xs, wb2_ref[...], preferred_element_type=F32)
                finalize(os_ref, ress_ref, gates_ref, accs, accs2)

        if cast or extra:
            pl.when(first)(first_step)

        x = x_ref[...]
        acc = jnp.dot(x, wb_ref[...], preferred_element_type=F32)
        acc2 = None
        if mode == "swiglu":
            acc2 = jnp.dot(x, wb2_ref[...], preferred_element_type=F32)
        finalize(o_ref, res_ref, gate_ref, acc, acc2)
    else:
        acc_ref = scratch[0]
        k = pl.program_id(3)
        w = w_ref[...].astype(BF16) if cast else w_ref[...]
        part = jnp.dot(x_ref[...], w, preferred_element_type=F32)

        @pl.when(k == 0)
        def _():
            acc_ref[...] = part

        @pl.when(k > 0)
        def _():
            acc_ref[...] += part

        @pl.when(k == nk - 1)
        def _():
            finalize(o_ref, res_ref, gate_ref, acc_ref[...])

        if extra:
            accs_ref = scratch[1]

            @pl.when(first)
            def _():
                parts = jnp.dot(xs_ref[...], w, preferred_element_type=F32)

                @pl.when(k == 0)
                def _():
                    accs_ref[...] = parts

                @pl.when(k > 0)
                def _():
                    accs_ref[...] += parts

                @pl.when(k == nk - 1)
                def _():
                    finalize(os_ref, ress_ref, gates_ref, accs_ref[...])


def _linear(x, w, *, layer, n, col_map, out_dtype, bm, bn, bk=None, mode="plain",
            col_map2=None, res=None, gate=None, gate_chunk=0, xs=None, res_s=None, gate_s=None,
            side_jobs=(), name="linear"):
    bn_, t, kdim = x.shape
    bk = kdim if bk is None else bk
    nk = kdim // bk
    assert kdim % bk == 0 and t % bm == 0 and n % bn == 0
    assert mode != "swiglu" or nk == 1
    grid = (n // bn, bn_, t // bm, nk)

    in_specs = [
        pl.BlockSpec((None, bm, bk), lambda j, b, i, k: (b, i, k)),
        pl.BlockSpec((None, bk, bn), lambda j, b, i, k: (layer, k, col_map(j))),
    ]
    args = [x, w]
    est = 2 * (_nbytes((bm, bk), BF16) + _nbytes((bk, bn), w.dtype) + _nbytes((bm, bn), out_dtype))
    est += _nbytes((bk, bn), BF16) + 2 * _nbytes((bm, bn), F32)
    if mode == "swiglu":
        in_specs.append(pl.BlockSpec((None, bk, bn), lambda j, b, i, k: (layer, k, col_map2(j))))
        args.append(w)
        est += 2 * _nbytes((bk, bn), F32) + _nbytes((bk, bn), BF16) + _nbytes((bm, bn), F32)
    if mode == "resid":
        tg = gate.shape[1]
        mg = 1 if tg == 1 else bm
        nb = n // bn
        in_specs.append(pl.BlockSpec((None, bm, bn), lambda j, b, i, k: (b, i, j)))
        in_specs.append(pl.BlockSpec(
            (None, mg, bn), lambda j, b, i, k: (b, 0 if tg == 1 else i, gate_chunk * nb + j)))
        args += [res, gate]
        est += 4 * _nbytes((bm, bn), F32)

    out_shape = jax.ShapeDtypeStruct((bn_, t, n), out_dtype)
    out_specs = pl.BlockSpec((None, bm, bn), lambda j, b, i, k: (b, i, j))
    ms = 0
    if xs is not None:
        ms = xs.shape[1]
        in_specs.append(pl.BlockSpec((None, ms, bk), lambda j, b, i, k: (0, 0, k)))
        args.append(xs)
        if mode == "resid":
            nb = n // bn
            in_specs.append(pl.BlockSpec((None, ms, bn), lambda j, b, i, k: (0, 0, j)))
            in_specs.append(pl.BlockSpec((None, ms, bn), lambda j, b, i, k: (0, 0, gate_chunk * nb + j)))
            args += [res_s, gate_s]
        out_shape = (out_shape, jax.ShapeDtypeStruct((1, ms, n), out_dtype))
        out_specs = (out_specs, pl.BlockSpec((None, ms, bn), lambda j, b, i, k: (0, 0, j)))
        est += 2 * _nbytes((ms, bk), BF16) + 8 * _nbytes((ms, bn), F32)

    if side_jobs:
        assert nk == 1
        nt_ = t // bm
        out_shape = tuple(out_shape) if isinstance(out_shape, tuple) else (out_shape,)
        out_specs = tuple(out_specs) if isinstance(out_specs, tuple) else (out_specs,)

        def at_step(index_of_step):
            return lambda j, b, i, k: index_of_step((j * bn_ + b) * nt_ + i)

        for job in side_jobs:
            in_specs += [pl.BlockSpec(blk, at_step(f)) for blk, f in job.in_blocks]
            args += list(job.args)
            out_shape += tuple(job.out_shapes)
            out_specs += tuple(pl.BlockSpec(blk, at_step(f)) for blk, f in job.out_blocks)
            est += job.nbytes

    if nk > 1:
        scratch = [pltpu.VMEM((bm, bn), F32)] + ([pltpu.VMEM((ms, bn), F32)] if ms else [])
    elif w.dtype != BF16:
        scratch = [pltpu.VMEM((bk, bn), BF16)] * (2 if mode == "swiglu" else 1)
    else:
        scratch = []

    return pl.pallas_call(
        functools.partial(_linear_kernel, nk=nk, mode=mode, extra=xs is not None,
                          side_jobs=tuple(side_jobs)),
        out_shape=out_shape,
        grid=grid,
        in_specs=in_specs,
        out_specs=out_specs,
        scratch_shapes=scratch,
        compiler_params=_params(est, 4),
        name=name,
    )(*args)


def _attn_prompt_kernel(*refs, ngroups):
    q_refs, k_refs, v_refs = refs[:ngroups], refs[ngroups:2 * ngroups], refs[2 * ngroups:3 * ngroups]
    o_ref, og_ref, lg_ref, bias_ref = refs[3 * ngroups:]
    seq = o_ref.shape[0]
    scale = HEAD_DIM ** -0.5
    nt = (((1,), (1,)), ((), ()))
    neg = -jnp.inf
    row1 = lax.broadcasted_iota(jnp.int32, (N_BACK, N_BACK), 0)
    col1 = lax.broadcasted_iota(jnp.int32, (N_BACK, N_BACK), 1)
    row2 = lax.broadcasted_iota(jnp.int32, (N_BACK, 2 * N_BACK), 0)
    col2 = lax.broadcasted_iota(jnp.int32, (N_BACK, 2 * N_BACK), 1)
    ok_first = col1 <= row1
    ok_band = jnp.logical_or(jnp.logical_and(col2 < N_BACK, col2 >= row2),
                             jnp.logical_and(col2 >= N_BACK, col2 - N_BACK <= row2))
    bias_ref[:, :N_BACK] = jnp.where(ok_first, 0.0, neg)
    bias_ref[:, N_BACK:] = jnp.where(ok_band, 0.0, neg)
    ones = jnp.ones((N_BACK, HEAD_DIM), BF16)

    for g in range(ngroups):
        dil = DILATIONS[g]
        length = seq // dil
        nblk = length // N_BACK
        for r in range(dil):
            sub_rows = slice(0, length) if dil == 1 else pl.ds(r, length, stride=dil)
            q = q_refs[g][sub_rows, :].astype(BF16)
            k = k_refs[g][sub_rows, :].astype(BF16)
            v = v_refs[g][sub_rows, :].astype(BF16)
            scores = []
            for i in range(nblk):
                qi = q[i * N_BACK:(i + 1) * N_BACK]
                keys = k[max(i - 1, 0) * N_BACK:(i + 1) * N_BACK]
                scores.append(lax.dot_general(qi, keys, nt, preferred_element_type=F32) * scale)
            probs, maxes = [], []
            for i in range(nblk):
                s = scores[i] + (bias_ref[:, :N_BACK] if i == 0 else bias_ref[:, N_BACK:])
                m = jnp.max(s, axis=-1, keepdims=True)
                probs.append(jnp.exp(s - m).astype(BF16))
                maxes.append(m)
            for i in range(nblk):
                vals = v[max(i - 1, 0) * N_BACK:(i + 1) * N_BACK]
                vext = jnp.concatenate([vals, jnp.concatenate([ones] * (vals.shape[0] // N_BACK), axis=0)], axis=1)
                acc = jnp.dot(probs[i], vext, preferred_element_type=F32)
                den = acc[:, HEAD_DIM:]
                start = r + i * N_BACK * dil
                out_rows = slice(start, start + N_BACK) if dil == 1 else pl.ds(start, N_BACK, stride=dil)
                og_ref.at[g][out_rows, :] = acc[:, :HEAD_DIM] / den
                lg_ref.at[g][out_rows, :] = maxes[i] + jnp.log(den)

    rb = MERGE_ROW_TILE
    for c in range(seq // rb):
        rows = slice(c * rb, (c + 1) * rb)
        ls = [lg_ref[g, rows, :] for g in range(ngroups)]
        m = functools.reduce(jnp.maximum, ls)
        es = [jnp.exp(l - m) for l in ls]
        tot = functools.reduce(lambda a, b: a + b, es)
        num = functools.reduce(lambda a, b: a + b, [es[g] * og_ref[g, rows, :] for g in range(ngroups)])
        o_ref[rows, :] = (num / tot).astype(o_ref.dtype)


def _attn_prompt(q, kvs):
    ngroups = len(DILATIONS)
    b, s, qc = q.shape
    hw = qc // ngroups
    nh = hw // HEAD_DIM
    assert all(s % (d * N_BACK) == 0 for d in DILATIONS) and s % MERGE_ROW_TILE == 0
    blk = (None, s, HEAD_DIM)
    in_specs = ([pl.BlockSpec(blk, (lambda bi, h, g=g: (bi, 0, g * nh + h))) for g in range(ngroups)]
                + [pl.BlockSpec(blk, lambda bi, h: (bi, 0, h))] * ngroups
                + [pl.BlockSpec(blk, lambda bi, h: (bi, 0, nh + h))] * ngroups)
    est = (2 * (3 * ngroups * _nbytes((s, HEAD_DIM), F32) + _nbytes((s, HEAD_DIM), BF16))
           + 2 * ngroups * _nbytes((s, HEAD_DIM), F32) + 6 * _nbytes((s, HEAD_DIM), F32))
    return pl.pallas_call(
        functools.partial(_attn_prompt_kernel, ngroups=ngroups),
        out_shape=jax.ShapeDtypeStruct((b, s, hw), BF16),
        grid=(b, nh),
        in_specs=in_specs,
        out_specs=pl.BlockSpec(blk, lambda bi, h: (bi, 0, h)),
        scratch_shapes=[pltpu.VMEM((ngroups, s, HEAD_DIM), F32), pltpu.VMEM((ngroups, s, HEAD_DIM), F32),
                        pltpu.VMEM((N_BACK, 3 * N_BACK), F32)],
        compiler_params=_params(est, 2),
        name="attn_prompt",
    )(*([q] * ngroups), *kvs, *kvs)


def _attn_cached_kernel(*refs, dil, nheads, has_carry, final):
    if has_carry:
        q_ref, kn_ref, vn_ref, kb_ref, vb_ref, m_in, l_in, a_in = refs[:8]
        outs = refs[8:]
    else:
        q_ref, kn_ref, vn_ref, kb_ref, vb_ref = refs[:5]
        outs = refs[5:]
    if final:
        o_ref, m_sc, l_sc, a_sc = outs
    else:
        m_out, l_out, a_out, m_sc, l_sc, a_sc = outs
    c = pl.program_id(1)
    nc = pl.num_programs(1)
    restricted = len(kb_ref.shape) == 4
    res = kb_ref.shape[1] if restricted else 1
    period = dil if restricted else 1
    wc = kb_ref.shape[0] * res
    w_total = kb_ref.shape[0] * period * nc
    nq = q_ref.shape[0]
    hshift = nheads.bit_length() - 1
    rshift = res.bit_length() - 1
    scale = HEAD_DIM ** -0.5
    nt = (((1,), (1,)), ((), ()))
    neg = -jnp.inf
    q = q_ref[...].astype(BF16)

    def attend(k, v, ok):
        s = lax.dot_general(q, k.astype(BF16), nt, preferred_element_type=F32) * scale
        s = jnp.where(ok, s, neg)
        m_old = m_sc[...]
        m_new = jnp.maximum(m_old, jnp.max(s, axis=-1, keepdims=True))
        alpha = jnp.exp(m_old - m_new)
        p = jnp.exp(s - m_new)
        l_sc[...] = alpha * l_sc[...] + jnp.sum(p, axis=-1, keepdims=True)
        a_sc[...] = alpha * a_sc[...] + jnp.dot(p.astype(BF16), v.astype(BF16), preferred_element_type=F32)
        m_sc[...] = m_new

    @pl.when(c == 0)
    def _():
        if has_carry:
            m_sc[...] = m_in[...]
            l_sc[...] = l_in[...]
            a_sc[...] = a_in[...]
        else:
            m_sc[...] = jnp.full(m_sc.shape, neg, F32)
            l_sc[...] = jnp.zeros(l_sc.shape, F32)
            a_sc[...] = jnp.zeros(a_sc.shape, F32)
        qr = lax.broadcasted_iota(jnp.int32, (nq, nq), 0)
        kr = lax.broadcasted_iota(jnp.int32, (nq, nq), 1)
        dt = (qr >> hshift) - (kr >> hshift)
        ok = ((qr & (nheads - 1)) == (kr & (nheads - 1))) & (dt >= 0) & ((dt & (dil - 1)) == 0)
        attend(kn_ref[...], vn_ref[...], ok)

    qr = lax.broadcasted_iota(jnp.int32, (nq, wc * nheads), 0)
    kr = lax.broadcasted_iota(jnp.int32, (nq, wc * nheads), 1)
    held = kr >> hshift
    w_row = (c * (wc // res) + (held >> rshift)) * period + (held & (res - 1))
    dist = w_total + (qr >> hshift) - w_row
    ok = (((qr & (nheads - 1)) == (kr & (nheads - 1))) & ((dist & (dil - 1)) == 0)
          & (dist <= dil * N_BACK))
    attend(kb_ref[...].reshape(wc * nheads, HEAD_DIM), vb_ref[...].reshape(wc * nheads, HEAD_DIM), ok)

    @pl.when(c == nc - 1)
    def _():
        if final:
            o_ref[...] = a_sc[...] / l_sc[...]
        else:
            m_out[...] = m_sc[...]
            l_out[...] = l_sc[...]
            a_out[...] = a_sc[...]


def _attn_cached(qf, knf, vnf, buf, g, carry, *, final, wc):
    dil = DILATIONS[g]
    b, nq, hd = qf.shape
    w, nheads = buf.shape[1], buf.shape[3]
    t_new = nq // nheads
    assert w % wc == 0 and nheads & (nheads - 1) == 0 and dil & (dil - 1) == 0
    row = pl.BlockSpec((None, nq, hd), lambda bi, c: (bi, 0, 0))
    stat = pl.BlockSpec((None, nq, 1), lambda bi, c: (bi, 0, 0))
    if dil > t_new and t_new & (t_new - 1) == 0 and w % dil == 0 and wc % dil == 0:
        wc = min(w, wc * dil // t_new)
        buf = buf.reshape(b, w // dil, dil, 2, nheads, hd)
        kv_blk = (None, wc // dil, t_new, None, nheads, hd)
        kv_specs = [pl.BlockSpec(kv_blk, lambda bi, c: (bi, c, 0, 0, 0, 0)),
                    pl.BlockSpec(kv_blk, lambda bi, c: (bi, c, 0, 1, 0, 0))]
        held = wc // dil * t_new
    else:
        kv_blk = (None, wc, None, nheads, hd)
        kv_specs = [pl.BlockSpec(kv_blk, lambda bi, c: (bi, c, 0, 0, 0)),
                    pl.BlockSpec(kv_blk, lambda bi, c: (bi, c, 1, 0, 0))]
        held = wc
    in_specs = [row, row, row] + kv_specs
    args = [qf, knf, vnf, buf, buf]
    if carry is not None:
        in_specs += [stat, stat, row]
        args += list(carry)
    if final:
        out_shape = jax.ShapeDtypeStruct((b, nq, hd), F32)
        out_specs = row
    else:
        out_shape = (jax.ShapeDtypeStruct((b, nq, 1), F32), jax.ShapeDtypeStruct((b, nq, 1), F32),
                     jax.ShapeDtypeStruct((b, nq, hd), F32))
        out_specs = (stat, stat, row)
    est = (4 * _nbytes((held, nheads, hd), F32) + 4 * _nbytes((nq, held * nheads), F32)
           + 16 * _nbytes((nq, hd), F32))
    return pl.pallas_call(
        functools.partial(_attn_cached_kernel, dil=dil, nheads=nheads, has_carry=carry is not None, final=final),
        out_shape=out_shape,
        grid=(b, w // wc),
        in_specs=in_specs,
        out_specs=out_specs,
        scratch_shapes=[pltpu.VMEM((nq, 1), F32), pltpu.VMEM((nq, 1), F32), pltpu.VMEM((nq, hd), F32)],
        compiler_params=_params(est, 2),
        name=f"attn_cached_g{g}",
    )(*args)


def _shift_block(cur_ref, nxt_ref, new_ref, o_ref, last):
    rb, t_new = cur_ref.shape[0], new_ref.shape[0]
    o_ref[0:rb - t_new] = cur_ref[t_new:rb]
    o_ref[rb - t_new:rb] = jnp.where(last, new_ref[...], nxt_ref[...])


def _shift_kernel(cur_ref, nxt_ref, new_ref, o_ref):
    _shift_block(cur_ref, nxt_ref, new_ref, o_ref, pl.program_id(1) == pl.num_programs(1) - 1)


def _shift_job(buf, new, steps, *, rb):
    b, w, r, lanes = buf.shape
    t_new = new.shape[1]
    if w % rb or rb % t_new or rb <= t_new or b * (w // rb) > steps:
        return None
    nblk = w // rb
    nact = b * nblk
    last_blk = w // t_new - 1

    def where(s):
        s = jnp.minimum(s, nact - 1)
        return lax.div(s, nblk), lax.rem(s, nblk)

    def body(step, ins, outs):
        @pl.when(step < nact)
        def _():
            _shift_block(*ins, outs[0], lax.rem(step, nblk) == nblk - 1)

    blk = (None, rb, r, lanes)
    small = (None, t_new, r, lanes)
    return _SideJob(
        args=(buf, buf, new),
        in_blocks=((blk, lambda s: (*where(s), 0, 0)),
                   (small, lambda s: (where(s)[0],
                                      jnp.minimum((where(s)[1] + 1) * (rb // t_new), last_blk), 0, 0)),
                   (small, lambda s: (where(s)[0], 0, 0, 0))),
        out_shapes=(jax.ShapeDtypeStruct(buf.shape, F32),),
        out_blocks=((blk, lambda s: (*where(s), 0, 0)),),
        body=body,
        nbytes=4 * _nbytes((rb, r, lanes), F32) + 4 * _nbytes((t_new, r, lanes), F32))


def _shift_append(buf, new, *, rb):
    b, w, r, lanes = buf.shape
    t_new = new.shape[1]
    assert w % rb == 0 and rb % t_new == 0 and rb > t_new
    last_blk = w // t_new - 1
    est = 4 * _nbytes((rb, r, lanes), F32) + 6 * _nbytes((t_new, r, lanes), F32)
    return pl.pallas_call(
        _shift_kernel,
        out_shape=jax.ShapeDtypeStruct(buf.shape, F32),
        grid=(b, w // rb),
        in_specs=[
            pl.BlockSpec((None, rb, r, lanes), lambda bi, j: (bi, j, 0, 0)),
            pl.BlockSpec((None, t_new, r, lanes),
                         lambda bi, j: (bi, jnp.minimum((j + 1) * (rb // t_new), last_blk), 0, 0)),
            pl.BlockSpec((None, t_new, r, lanes), lambda bi, j: (bi, 0, 0, 0)),
        ],
        out_specs=pl.BlockSpec((None, rb, r, lanes), lambda bi, j: (bi, j, 0, 0)),
        compiler_params=_params(est, 2),
        name="kv_shift_append",
    )(buf, buf, new)


def _hgrn_kernel(*refs, heads, chunk, layer, has_state):
    if has_state:
        zq_ref, zf_ref, zv_ref, zg_ref, lb_ref, ng_ref, s0_ref = refs[:7]
        rest = refs[7:]
    else:
        zq_ref, zf_ref, zv_ref, zg_ref, lb_ref, ng_ref = refs[:6]
        s0_ref = None
        rest = refs[6:]
    y_ref, st_out_ref, st_ref = rest[:3]
    bc_ref = rest[3] if len(rest) > 3 else None
    tstep = pl.program_id(2)
    nsteps = pl.num_programs(2)
    tb = zq_ref.shape[0]

    @pl.when(tstep == 0)
    def _():
        for h in range(heads):
            if has_state:
                st_ref[h] = s0_ref[h].T
            else:
                st_ref[h] = jnp.zeros((HGRN_DK, HGRN_DK), F32)

    logits = lb_ref[...]
    ex = jnp.exp(logits - jnp.max(logits, axis=0, keepdims=True))
    sm = ex / jnp.sum(ex, axis=0, keepdims=True)
    lb = jnp.zeros_like(sm[0:1])
    for l in range(1, layer + 1):
        lb = lb + sm[l:l + 1]
    ng = ng_ref[...]

    sub = V7X_SUBLANES
    rblk = sub if chunk % sub == 0 else chunk
    nblk = chunk // rblk
    rowi = lax.broadcasted_iota(jnp.int32, (rblk, HGRN_DK), 0)
    rowi_all = lax.broadcasted_iota(jnp.int32, (chunk, zq_ref.shape[1]), 0)
    nt = (((1,), (1,)), ((), ()))
    tn = (((0,), (0,)), ((), ()))

    def body(ci, carry):
        rows = pl.ds(pl.multiple_of(ci * chunk, chunk), chunk)
        zq = zq_ref[rows, :]
        zf = zf_ref[rows, :]
        zv = zv_ref[rows, :]
        zg = zg_ref[rows, :]
        q_all = _silu(zq)
        f_all = lb + (1.0 - lb) * jax.nn.sigmoid(zf)
        lf_all = jnp.log(f_all)
        k_all = 1.0 - f_all
        og_all = ng * _silu(zg)
        if chunk % sub == 0:
            a_all = lf_all
            shift = 1
            while shift < chunk:
                a_all = a_all + jnp.where(rowi_all >= shift, pltpu.roll(a_all, shift, 0), 0.0)
                shift *= 2
        else:
            a_all = jnp.zeros_like(lf_all)
            for s in range(chunk):
                a_all = a_all + jnp.where(rowi_all >= s, lf_all[s:s + 1, :], 0.0)
        a_all = a_all * LOG2_E
        qe_all = q_all * jnp.exp2(a_all)
        a_last_all = a_all[chunk - 1:chunk, :]
        kd_all = k_all * jnp.exp2(a_last_all - a_all)
        dec_all = jnp.exp2(a_last_all)
        if bc_ref is not None:
            for h in range(heads):
                lanes = slice(h * HGRN_DK, (h + 1) * HGRN_DK)
                bc_ref[0, h] = a_all[:, lanes]
                bc_ref[1, h] = k_all[:, lanes]
                bc_ref[2, h] = zv[:, lanes]
        o_inter, scores, states_in = [], [], []
        for h in range(heads):
            lanes = slice(h * HGRN_DK, (h + 1) * HGRN_DK)
            q, k, a = q_all[:, lanes], k_all[:, lanes], a_all[:, lanes]
            st = st_ref[h]
            states_in.append(st)
            o_inter.append(lax.dot_general(qe_all[:, lanes].astype(BF16), st.astype(BF16), nt,
                                           preferred_element_type=F32))
            sc = None
            for j in range(nblk - 1):
                r0, r1 = j * rblk, (j + 1) * rblk
                b_j = a[r1 - 1:r1, :]
                qt = jnp.concatenate([jnp.zeros((r1, HGRN_DK), F32), q[r1:] * jnp.exp2(a[r1:] - b_j)], axis=0)
                kh = [k[r0:r1] * jnp.exp2(b_j - a[r0:r1])]
                if r0 > 0:
                    kh.insert(0, jnp.zeros((r0, HGRN_DK), F32))
                kh.append(jnp.zeros((chunk - r1, HGRN_DK), F32))
                sc_j = lax.dot_general(qt.astype(BF16), jnp.concatenate(kh, axis=0).astype(BF16), nt,
                                       preferred_element_type=F32)
                sc = sc_j if sc is None else sc + sc_j
            scores.append(sc)
        for h in range(heads):
            lanes = slice(h * HGRN_DK, (h + 1) * HGRN_DK)
            q, k, v, a = q_all[:, lanes], k_all[:, lanes], zv[:, lanes], a_all[:, lanes]
            st = states_in[h]
            o = o_inter[h]
            if scores[h] is not None:
                o = o + jnp.dot(scores[h].astype(BF16), v.astype(BF16), preferred_element_type=F32)
            ob = [None] * nblk
            for s in range(chunk):
                bi = s // rblk
                r0 = bi * rblk
                if bc_ref is not None:
                    a_s, k_s, v_s = (jnp.broadcast_to(bc_ref[c, h, s:s + 1, :], (rblk, HGRN_DK))
                                     for c in range(3))
                else:
                    a_s, k_s, v_s = a[s:s + 1, :], k[s:s + 1, :], v[s:s + 1, :]
                e = jnp.where(rowi >= s - r0, jnp.exp2(a[r0:r0 + rblk] - a_s), 0.0)
                wgt = (q[r0:r0 + rblk] * k_s) * e
                term = jnp.sum(wgt, axis=-1, keepdims=True) * v_s
                ob[bi] = term if ob[bi] is None else ob[bi] + term
            o = o + (ob[0] if nblk == 1 else jnp.concatenate(ob, axis=0))
            upd = lax.dot_general(v.astype(BF16), kd_all[:, lanes].astype(BF16), tn,
                                  preferred_element_type=F32)
            st_ref[h] = st * dec_all[:, lanes] + upd
            on = o * lax.rsqrt(jnp.mean(o * o, axis=-1, keepdims=True) + EPS)
            y_ref[rows, lanes] = (on * og_all[:, lanes]).astype(y_ref.dtype)
        return carry

    lax.fori_loop(0, tb // chunk, body, 0)

    @pl.when(tstep == nsteps - 1)
    def _():
        for h in range(heads):
            st_out_ref[h] = st_ref[h].T


def _hgrn(z, lb_logits, norm_g, s0, *, layer, heads, tb, out_dtype):
    b, t, d4 = z.shape
    d = d4 // 4
    nh = d // HGRN_DK
    nhg = nh // heads
    hb = heads * HGRN_DK
    chunk = math.gcd(t, HGRN_CHUNK)
    depth = lb_logits.shape[0]
    assert t % tb == 0 and tb % chunk == 0 and nh % heads == 0

    def zspec(part):
        return pl.BlockSpec((None, tb, hb), lambda bi, hg, i: (bi, i, part * nhg + hg))

    in_specs = [zspec(0), zspec(1), zspec(2), zspec(3),
                pl.BlockSpec((depth, hb), lambda bi, hg, i: (0, hg)),
                pl.BlockSpec((1, hb), lambda bi, hg, i: (0, hg))]
    args = [z, z, z, z, lb_logits, norm_g.reshape(1, d)]
    sspec = pl.BlockSpec((None, heads, HGRN_DK, HGRN_DK), lambda bi, hg, i: (bi, hg, 0, 0))
    if s0 is not None:
        in_specs.append(sspec)
        args.append(s0)
    est = (2 * (4 * _nbytes((tb, hb), F32) + _nbytes((tb, hb), out_dtype))
           + 5 * _nbytes((heads, HGRN_DK, HGRN_DK), F32))
    return pl.pallas_call(
        functools.partial(_hgrn_kernel, heads=heads, chunk=chunk, layer=layer, has_state=s0 is not None),
        out_shape=(jax.ShapeDtypeStruct((b, t, d), out_dtype),
                   jax.ShapeDtypeStruct((b, nh, HGRN_DK, HGRN_DK), F32)),
        grid=(b, nhg, t // tb),
        in_specs=in_specs,
        out_specs=(pl.BlockSpec((None, tb, hb), lambda bi, hg, i: (bi, i, hg)), sspec),
        scratch_shapes=([pltpu.VMEM((heads, HGRN_DK, HGRN_DK), F32)]
                        + ([pltpu.VMEM((3, heads, chunk, HGRN_DK), F32)] if chunk % V7X_SUBLANES == 0 else [])),
        compiler_params=_params(est, 3),
        name="hgrn2",
    )(*args)


def _forward(xp, xs, c_all, ada_w, ada_b, a_bufs, b_states, weights, *, nb, t_seq):
    (norm_mix_g, norm_ffn_g, a_w_qkv, a_w_o, b_w_in, b_lb_logits, b_norm_g, b_w_o,
     ffn_w_in, ffn_w_out, final_g) = weights
    bp, t, d = xp.shape
    ms = xs.shape[1]
    depth = norm_mix_g.shape[0]
    d_ff = ffn_w_out.shape[1]
    ngroups = len(DILATIONS)
    hw = a_w_o.shape[1]
    nheads = hw // HEAD_DIM
    bm = _pick(t, ROW_TILES)
    bm_e = _pick(t, NORM_ROW_TILES)
    bn = N_TILE
    ident = lambda j: j
    kv_p, kv_s, st_p, st_s = [], [], [], []
    w_dn_all = None
    bnf = N_TILE_SWIGLU
    nff = d_ff // bnf
    ffn_steps = nff * bp * (t // bm)
    ada_b3 = ada_b.reshape(depth, 1, 6 * d)
    ride = depth > 1 and _ada_job(c_all, ada_w, ada_b3, 1, ffn_steps) is not None
    mod0 = _ada_modulation(c_all, ada_w, ada_b3, nlayers=1 if ride else depth, bn=ADA_N_TILE)
    mod_rows = {l: mod0[l] for l in range(mod0.shape[0])}
    pending_shifts = []

    for layer in range(depth):
        i = layer // N_MIXERS
        mod_l = mod_rows[layer]
        mp = mod_l[:bp].reshape(bp, 1, 6 * d)
        msd = jnp.repeat(mod_l[bp:bp + nb], t_seq, axis=0).reshape(1, ms, 6 * d)
        hp = _norm_mod(xp, norm_mix_g[layer], mp, 1, 0, bm=bm_e)
        hs = _norm_mod(xs, norm_mix_g[layer], msd, 1, 0, bm=ms)
        if layer % N_MIXERS == 0:
            qp, qs = _linear(hp, a_w_qkv, layer=i, n=ngroups * hw, col_map=ident, out_dtype=F32,
                             bm=bm, bn=bn, xs=hs, name="a_q")
            half = hw // bn
            q5 = qs.reshape(nb, t_seq, ngroups, nheads, HEAD_DIM)
            flat = lambda a: a.reshape(nb, t_seq * nheads, HEAD_DIM)
            kvs = []
            carry = None
            for g in range(ngroups):
                def kv_cols(j, g=g):
                    return jnp.where(j < half, (ngroups + g) * half + j, (2 * ngroups + g) * half + (j - half))
                kvp, kvn = _linear(hp, a_w_qkv, layer=i, n=2 * hw, col_map=kv_cols, out_dtype=F32,
                                   bm=bm, bn=bn, xs=hs, name=f"a_kv{g}")
                kvs.append(kvp)
                keep = min(DILATIONS[g] * N_BACK, t)
                kv_keep = kvp if keep == t else kvp[:, t - keep:]
                kv_p.append(kv_keep.reshape(1, bp, keep, 2, nheads, HEAD_DIM))
                buf = a_bufs[g][i]
                w = buf.shape[1]
                kvn = kvn.reshape(nb, t_seq, 2, nheads, HEAD_DIM)
                carry = _attn_cached(flat(q5[:, :, g]), flat(kvn[:, :, 0]), flat(kvn[:, :, 1]), buf, g,
                                     carry, final=g == ngroups - 1, wc=min(w, WINDOW_CHUNK))
                pending_shifts.append((buf.reshape(nb, w, 2 * nheads, HEAD_DIM),
                                       kvn.reshape(nb, t_seq, 2 * nheads, HEAD_DIM), len(kv_s)))
                kv_s.append(None)
            yp = _attn_prompt(qp, kvs)
            ys = carry.reshape(1, ms, hw).astype(BF16)
            w_o = a_w_o
        else:
            zp, zs = _linear(hp, b_w_in, layer=i, n=4 * d, col_map=ident, out_dtype=F32, bm=bm, bn=bn,
                             xs=hs, name="b_in")
            hg_heads = _pick(d // HGRN_DK, HGRN_HEADS_PER_STEP)
            yp, s_p = _hgrn(zp, b_lb_logits, b_norm_g[i], None, layer=layer, heads=hg_heads,
                            tb=min(t, HGRN_ROW_TILE), out_dtype=BF16)
            ysb, s_s = _hgrn(zs.reshape(nb, t_seq, 4 * d), b_lb_logits, b_norm_g[i], b_states[i],
                             layer=layer, heads=hg_heads, tb=t_seq, out_dtype=F32)
            ys = ysb.reshape(1, ms, d).astype(BF16)
            st_p.append(s_p)
            st_s.append(s_s)
            w_o = b_w_o
        bn_o = N_TILE_SHORT_K if (w_o.shape[1] <= SHORT_K and d % N_TILE_SHORT_K == 0) else bn
        xp, xs = _linear(yp, w_o, layer=i, n=d, col_map=ident, out_dtype=F32, bm=bm, bn=bn_o,
                         mode="resid", res=xp, gate=mp, gate_chunk=2, xs=ys, res_s=xs, gate_s=msd,
                         name="mix_o")
        hp = _norm_mod(xp, norm_ffn_g[layer], mp, 4, 3, bm=bm_e)
        hs = _norm_mod(xs, norm_ffn_g[layer], msd, 4, 3, bm=ms)
        jobs, roles = [], []
        cast = _cast_job(ffn_w_out, layer, ffn_steps)
        if cast is not None:
            jobs.append(cast)
            roles.append(("w_dn", None))
        if ride and layer + 1 < depth:
            jobs.append(_ada_job(c_all, ada_w, ada_b3, layer + 1, ffn_steps))
            roles.append(("mod", layer + 1))
        else:
            for buf4, new4, slot in pending_shifts:
                job = next((jb for jb in (_shift_job(buf4, new4, ffn_steps, rb=rb)
                                          for rb in SIDE_SHIFT_ROWS if rb <= buf4.shape[1])
                            if jb is not None), None)
                if job is not None:
                    jobs.append(job)
                    roles.append(("shift", slot))
            pending_shifts = [p for p in pending_shifts if ("shift", p[2]) not in roles]
        outs = _linear(hp, ffn_w_in, layer=layer, n=d_ff, col_map=ident, col_map2=lambda j: nff + j,
                       out_dtype=BF16, bm=bm, bn=bnf, mode="swiglu", xs=hs, side_jobs=jobs, name="ffn_in")
        actp, acts = outs[0], outs[1]
        w_dn = None
        for (role, key), val in zip(roles, outs[2:]):
            if role == "w_dn":
                w_dn, dn_layer = val[None], 0
            elif role == "mod":
                mod_rows[key] = val
            else:
                kv_s[key] = val.reshape(1, nb, val.shape[1], 2, nheads, HEAD_DIM)
        if w_dn is None:
            if w_dn_all is None:
                w_dn_all = _to_bf16(ffn_w_out, kb=_pick(d_ff, (d_ff // 16, d_ff // 8, d_ff // 4, d_ff // 2)))
            w_dn, dn_layer = w_dn_all, layer
        xp, xs = _linear(actp, w_dn, layer=dn_layer, n=d, col_map=ident, out_dtype=F32,
                         bm=_pick(t, ROW_TILES_LONG_K), bn=bn, mode="resid", res=xp, gate=mp, gate_chunk=5,
                         xs=acts, res_s=xs, gate_s=msd, name="ffn_out")
    for buf4, new4, slot in pending_shifts:
        w = buf4.shape[1]
        kv_s[slot] = _shift_append(buf4, new4, rb=min(w, WINDOW_CHUNK)).reshape(1, nb, w, 2, nheads, HEAD_DIM)
    yp = _final_norm(xp, final_g, bm=bm_e)
    ys = _final_norm(xs, final_g, bm=ms)
    return yp, ys, kv_p, kv_s, st_p, st_s


def kernel(x_prompt, x_sample, state_a_kv_w128, state_a_kv_w512, state_a_kv_w2048, state_b_rec,
           c_prompt, c_sample, ada_w, ada_b, norm_mix_g, norm_ffn_g, a_w_qkv, a_w_o,
           b_w_in, b_lb_logits, b_norm_g, b_w_o, ffn_w_in, ffn_w_out, final_g):
    bp, sp, d = x_prompt.shape
    bs, ts, _ = x_sample.shape
    depth = ada_w.shape[0]
    weights = (norm_mix_g, norm_ffn_g, a_w_qkv, a_w_o, b_w_in, b_lb_logits, b_norm_g, b_w_o,
               ffn_w_in, ffn_w_out, final_g)

    rpad = -(bp + bs) % V7X_SUBLANES
    c_all = jnp.pad(jnp.concatenate([c_prompt, c_sample], axis=0), ((0, rpad), (0, 0)))
    y_p, y_s, kv_p, kv_s, st_p, st_s = _forward(
        x_prompt, x_sample.reshape(1, bs * ts, d), c_all, ada_w, ada_b,
        (state_a_kv_w128, state_a_kv_w512, state_a_kv_w2048), state_b_rec, weights, nb=bs, t_seq=ts)
    return (y_p, y_s.reshape(bs, ts, d), kv_p[0], kv_s[0], kv_p[1], kv_s[1], kv_p[2], kv_s[2],
            jnp.stack(st_p, axis=0), jnp.stack(st_s, axis=0))
```
